```python
import math
import jax
import jax.numpy as jnp
from jax import lax
import numpy as np

D_MODEL = 2048
BATCH = 32
SEQ = 256
DEPTH = 2
DEC_BATCH = 2
DEC_SEQ = 4096
PAST_LEN = 512

GRID_W = 64
H_A = D_MODEL // 256
DH_A = 64
DV_A = 2 * DH_A
W_A = H_A * DV_A
C_B = D_MODEL // 2
CONV_W = 31
N_C = 64
H_C = D_MODEL // (2 * N_C)
C_C = H_C * N_C
LORA_W = 64
LORA_A = 64
N_BRANCH = 3
Q_BLOCK = 128
ROPE_BASE = 10000.0
EPS = 1e-6
LN_EPS = 1e-5
GN_EPS = 64e-5
IN_SPLITS = (H_A * 2 * DH_A, H_A * 2 * DH_A, W_A, W_A, 2 * C_B, C_B, C_C, C_C, C_C, C_C,
             2 * LORA_W, 2 * LORA_A, N_BRANCH * D_MODEL)
IN_COLS = sum(IN_SPLITS)

kernel_name = 'hybrid_diffusion_diffattn_conformer_rwkv7_step'


def rms_norm(x, w):
    xf = x.astype(jnp.float32)
    y = xf * lax.rsqrt(jnp.mean(xf * xf, axis=-1, keepdims=True) + EPS)
    return (y * w.astype(jnp.float32)).astype(x.dtype)


def split_in(p):
    idx, acc = [], 0
    for s in IN_SPLITS[:-1]:
        acc += s
        idx.append(acc)
    return jnp.split(p, idx, axis=-1)


def axial_rope(x, row, col):
    half = DH_A // 2
    quarter = half // 2
    inv = ROPE_BASE ** (-jnp.arange(quarter, dtype=jnp.float32) / quarter)

    def rot(xp, pos):
        ang = pos[:, None] * inv[None, :]
        cos = jnp.cos(ang)[None, :, None, None, :]
        sin = jnp.sin(ang)[None, :, None, None, :]
        x1, x2 = xp[..., :quarter], xp[..., quarter:]
        return jnp.concatenate([x1 * cos - x2 * sin, x2 * cos + x1 * sin], axis=-1)

    xf = x.astype(jnp.float32)
    return jnp.concatenate([rot(xf[..., :half], row), rot(xf[..., half:], col)], axis=-1).astype(x.dtype)


def diff_attention(q, k, v, lam):
    B, L = q.shape[:2]
    nb = L // Q_BLOCK
    qb = jnp.moveaxis(q.reshape(B, nb, Q_BLOCK, H_A, 2, DH_A), 1, 0)
    scale = DH_A ** -0.5

    def block(qblk):
        s = jnp.einsum('bqhmd,bkhmd->bhmqk', qblk, k).astype(jnp.float32) * scale
        p = jax.nn.softmax(s, axis=-1)
        p = p[:, :, 0] - lam * p[:, :, 1]
        return jnp.einsum('bhqk,bkhe->bqhe', p.astype(v.dtype), v)

    o = lax.map(block, qb)
    return jnp.moveaxis(o, 0, 1).reshape(B, L, H_A, DV_A)


def conformer_conv(u, conv_w, conv_b, ln_w, ln_b):
    a, g = jnp.split(u, 2, axis=-1)
    z = a * jax.nn.sigmoid(g)
    z = lax.conv_general_dilated(z, conv_w[:, None, :], window_strides=(1,),
                                 padding=[(CONV_W // 2, CONV_W // 2)],
                                 dimension_numbers=('NWC', 'WIO', 'NWC'),
                                 feature_group_count=C_B) + conv_b
    zf = z.astype(jnp.float32)
    mu = jnp.mean(zf, axis=-1, keepdims=True)
    var = jnp.mean(jnp.square(zf - mu), axis=-1, keepdims=True)
    zn = (zf - mu) * lax.rsqrt(var + LN_EPS) * ln_w.astype(jnp.float32) + ln_b.astype(jnp.float32)
    return jax.nn.silu(zn).astype(u.dtype)


def rwkv7_bidir(r, k, v, xw, xa, w0, w_up, a0, a_up, k_k, k_a, r_k, gn_w, gn_b, s0):
    f32 = jnp.float32
    B, L, _ = r.shape
    r, k, v, xw, xa = (t.astype(f32) for t in (r, k, v, xw, xa))
    w_log = -jax.nn.softplus(-(w0 + jnp.einsum('bldr,drc->bldc', jnp.tanh(xw), w_up))) - 0.5
    decay = jnp.exp(-jnp.exp(w_log))
    a = jax.nn.sigmoid(a0 + jnp.einsum('bldr,drc->bldc', xa, a_up))
    kk = (k * k_k).reshape(B, L, H_C, N_C)
    kk = kk / jnp.maximum(jnp.linalg.norm(kk, axis=-1, keepdims=True), 1e-12)
    kk = kk.reshape(B, L, C_C)
    k_dir = k[:, :, None, :] * (1.0 + (a - 1.0) * k_a)
    b_dir = kk[:, :, None, :] * a

    def heads(t):
        return t.reshape(B, L, H_C, N_C).transpose(1, 0, 2, 3)

    r_s, v_s, kk_s = heads(r), heads(v), heads(kk)

    def step(S, inp):
        r_t, w_t, k_t, v_t, kk_t, b_t = inp
        sa = jnp.einsum('bhij,bhj->bhi', S, -kk_t)
        S = S * w_t[:, :, None, :] + sa[..., None] * b_t[:, :, None, :] + v_t[..., None] * k_t[:, :, None, :]
        return S, jnp.einsum('bhij,bhj->bhi', S, r_t)

    finals = []
    y = 0.0
    for d, rev in ((0, False), (1, True)):
        s_fin, ys = lax.scan(step, s0[:, d].astype(f32),
                             (r_s, heads(decay[:, :, d]), heads(k_dir[:, :, d]), v_s, kk_s,
                              heads(b_dir[:, :, d])), reverse=rev)
        finals.append(s_fin)
        y = y + ys
    y = y.transpose(1, 0, 2, 3)
    mu = jnp.mean(y, axis=-1, keepdims=True)
    var = jnp.mean(jnp.square(y - mu), axis=-1, keepdims=True)
    y = ((y - mu) * lax.rsqrt(var + GN_EPS)).reshape(B, L, C_C) * gn_w + gn_b
    bonus = jnp.sum(r.reshape(B, L, H_C, N_C) * (k_dir[:, :, 0] + k_dir[:, :, 1]).reshape(B, L, H_C, N_C)
                    * r_k, axis=-1, keepdims=True) * v.reshape(B, L, H_C, N_C)
    y = y + bonus.reshape(B, L, C_C)
    return y, jnp.stack(finals, axis=1)


def mixer(h, lp, lam_init, pos, ctx_k, ctx_v, s0):
    B, L, _ = h.shape
    (q, k, v, g_a, glu, g_b, r_c, k_c, v_c, g_c, xw, xa, mg) = split_in(h @ lp['w_in'])
    q = q.reshape(B, L, H_A, 2, DH_A)
    k = k.reshape(B, L, H_A, 2, DH_A)
    v = v.reshape(B, L, H_A, DV_A)
    if pos is not None:
        q = axial_rope(q, pos[0], pos[1])
        k = axial_rope(k, pos[0], pos[1])
    if ctx_k is None:
        keys, vals = k, v
    else:
        keys = jnp.concatenate([ctx_k.astype(k.dtype), k], axis=1)
        vals = jnp.concatenate([ctx_v.astype(v.dtype), v], axis=1)
    f32 = jnp.float32
    lam = (jnp.exp(jnp.sum(lp['lq1'].astype(f32) * lp['lk1'].astype(f32)))
           - jnp.exp(jnp.sum(lp['lq2'].astype(f32) * lp['lk2'].astype(f32))) + lam_init)
    o = diff_attention(q, keys, vals, lam)
    o = rms_norm(o, lp['subln_w']) * (1.0 - lam_init)
    y_a = (o.reshape(B, L, W_A) * jax.nn.silu(g_a)) @ lp['w_br_a']
    z = conformer_conv(glu, lp['conv_w'], lp['conv_b'], lp['conv_ln_w'], lp['conv_ln_b'])
    y_b = (z * jax.nn.silu(g_b)) @ lp['w_br_b']
    if s0 is None:
        s0 = jnp.zeros((B, 2, H_C, N_C, N_C), jnp.float32)
    yc, s_fin = rwkv7_bidir(r_c, k_c, v_c, xw.reshape(B, L, 2, LORA_W), xa.reshape(B, L, 2, LORA_A),
                            lp['w0'], lp['w_up'], lp['a0'], lp['a_up'], lp['k_k'], lp['k_a'],
                            lp['r_k'], lp['gn_w'], lp['gn_b'], s0)
    y_c = (yc.astype(h.dtype) * jax.nn.silu(g_c)) @ lp['w_br_c']
    gates = jax.nn.sigmoid(mg.astype(f32)).reshape(B, L, N_BRANCH, D_MODEL).astype(h.dtype)
    merged = gates[:, :, 0] * y_a + gates[:, :, 1] * y_b + gates[:, :, 2] * y_c
    return merged @ lp['w_out'], k, v, s_fin


def trunk_layer(x, cond, lp, lam_init, pos, ctx_k, ctx_v, s0):
    shift, scale, gate = jnp.split(jax.nn.silu(cond) @ lp['w_ada'] + lp['b_ada'], 3, axis=-1)
    h = rms_norm(x, lp['norm_w']) * (1.0 + scale[:, None]) + shift[:, None]
    out, k, v, s_fin = mixer(h, lp, lam_init, pos, ctx_k, ctx_v, s0)
    return x + gate[:, None] * out, k, v, s_fin


def setup_inputs(seed: int = 0) -> dict:
    key = jax.random.key(seed)
    ks = iter(jax.random.split(key, 40))
    f32 = jnp.float32

    def nrm(shape, s):
        return s * jax.random.normal(next(ks), shape, f32)

    D = D_MODEL
    return {
        'x_prompt': nrm((BATCH, SEQ, D), 1.0),
        'x_sample': nrm((DEC_BATCH, DEC_SEQ, D), 1.0),
        'cache_k': nrm((DEC_BATCH, DEPTH, PAST_LEN, H_A, 2, DH_A), 1.0),
        'cache_v': nrm((DEC_BATCH, DEPTH, PAST_LEN, H_A, DV_A), 1.0),
        'state_rwkv': nrm((DEC_BATCH, DEPTH, 2, H_C, N_C, N_C), 0.3),
        'c': nrm((DEC_BATCH, D), 1.0),
        'c_ctx': nrm((D,), 1.0),
        'w_ada': nrm((DEPTH, D, 3 * D), 0.5 * D ** -0.5),
        'b_ada': nrm((DEPTH, 3 * D), 0.02),
        'norm_w': 1.0 + nrm((DEPTH, D), 0.02),
        'w_in': nrm((DEPTH, D, IN_COLS), D ** -0.5),
        'lambda_q1': nrm((DEPTH, DH_A), 0.1),
        'lambda_k1': nrm((DEPTH, DH_A), 0.1),
        'lambda_q2': nrm((DEPTH, DH_A), 0.1),
        'lambda_k2': nrm((DEPTH, DH_A), 0.1),
        'subln_w': 1.0 + nrm((DEPTH, DV_A), 0.02),
        'conv_w': nrm((DEPTH, CONV_W, C_B), CONV_W ** -0.5),
        'conv_b': nrm((DEPTH, C_B), 0.02),
        'conv_ln_w': 1.0 + nrm((DEPTH, C_B), 0.02),
        'conv_ln_b': nrm((DEPTH, C_B), 0.02),
        'rwkv_w0': -3.5 + nrm((DEPTH, 2, C_C), 1.5),
        'rwkv_w_up': nrm((DEPTH, 2, LORA_W, C_C), 0.1),
        'rwkv_a0': nrm((DEPTH, 2, C_C), 0.5),
        'rwkv_a_up': nrm((DEPTH, 2, LORA_A, C_C), 0.1),
        'rwkv_k_k': 0.85 + nrm((DEPTH, C_C), 0.05),
        'rwkv_k_a': 1.0 + nrm((DEPTH, C_C), 0.05),
        'rwkv_r_k': nrm((DEPTH, H_C, N_C), 0.1),
        'rwkv_gn_w': 1.0 + nrm((DEPTH, C_C), 0.02),
        'rwkv_gn_b': nrm((DEPTH, C_C), 0.02),
        'w_br_a': nrm((DEPTH, W_A, D), W_A ** -0.5),
        'w_br_b': nrm((DEPTH, C_B, D), C_B ** -0.5),
        'w_br_c': nrm((DEPTH, C_C, D), C_C ** -0.5),
        'w_out': nrm((DEPTH, D, D), D ** -0.5),
        'final_norm_w': 1.0 + nrm((D,), 0.02),
    }


def reference(x_prompt, x_sample, cache_k, cache_v, state_rwkv, c, c_ctx, w_ada, b_ada, norm_w, w_in,
              lambda_q1, lambda_k1, lambda_q2, lambda_k2, subln_w, conv_w, conv_b, conv_ln_w, conv_ln_b,
              rwkv_w0, rwkv_w_up, rwkv_a0, rwkv_a_up, rwkv_k_k, rwkv_k_a, rwkv_r_k, rwkv_gn_w, rwkv_gn_b,
              w_br_a, w_br_b, w_br_c, w_out, final_norm_w):
    n_lat = x_sample.shape[1]
    rows = n_lat // GRID_W
    row = jnp.repeat(jnp.arange(rows, dtype=jnp.float32), GRID_W)
    col = jnp.tile(jnp.arange(GRID_W, dtype=jnp.float32), rows)
    pos = (row, col)

    xp, xs = x_prompt, x_sample
    ks, vs, ss = [], [], []
    for l in range(DEPTH):
        lp = dict(w_ada=w_ada[l], b_ada=b_ada[l], norm_w=norm_w[l], w_in=w_in[l],
                  lq1=lambda_q1[l], lk1=lambda_k1[l], lq2=lambda_q2[l], lk2=lambda_k2[l],
                  subln_w=subln_w[l], conv_w=conv_w[l], conv_b=conv_b[l],
                  conv_ln_w=conv_ln_w[l], conv_ln_b=conv_ln_b[l],
                  w0=rwkv_w0[l], w_up=rwkv_w_up[l], a0=rwkv_a0[l], a_up=rwkv_a_up[l],
                  k_k=rwkv_k_k[l], k_a=rwkv_k_a[l], r_k=rwkv_r_k[l],
                  gn_w=rwkv_gn_w[l], gn_b=rwkv_gn_b[l],
                  w_br_a=w_br_a[l], w_br_b=w_br_b[l], w_br_c=w_br_c[l], w_out=w_out[l])
        lam_init = 0.8 - 0.6 * math.exp(-0.3 * l)
        xp, k_l, v_l, s_l = trunk_layer(xp, c_ctx[None], lp, lam_init, None, None, None, None)
        ks.append(k_l)
        vs.append(v_l)
        ss.append(s_l.astype(x_prompt.dtype))
        xs, _, _, _ = trunk_layer(xs, c, lp, lam_init, pos, cache_k[:, l], cache_v[:, l], state_rwkv[:, l])

    y_prompt = rms_norm(xp, final_norm_w)
    y_sample = rms_norm(xs, final_norm_w)
    new_cache_k = jnp.stack(ks, axis=1)
    new_cache_v = jnp.stack(vs, axis=1)
    new_state_rwkv = jnp.stack(ss, axis=1)
    return (y_prompt, y_sample, new_cache_k, new_cache_v, new_state_rwkv)
```

```python
import functools
import math

import jax
import jax.numpy as jnp
from jax import lax
from jax.experimental import pallas as pl
from jax.experimental.pallas import tpu as pltpu

F32 = jnp.float32
BF16 = jnp.bfloat16

HEAD_A = 128
DH_A = 64
CONV_W = 31
CONV_HALO = 16
N_C = 64
LORA = 64
GRID_W = 64
ROPE_BASE = 10000.0
EPS = 1e-6
LN_EPS = 1e-5
GN_EPS = 64e-5

LANES = 128
SUBLANES = 8
MIB = 1024 * 1024


def _cparams(semantics, vmem_mib=48):
    return pltpu.CompilerParams(dimension_semantics=semantics, vmem_limit_bytes=vmem_mib * MIB)


def _split_bf16(x):
    hi = x.astype(BF16)
    lo = (x - hi.astype(F32)).astype(BF16)
    return hi, lo


def _mm(a, b):
    return jnp.dot(a, b, preferred_element_type=F32)


def _dot3(a, b):
    ah, al = _split_bf16(a)
    bh, bl = _split_bf16(b)
    return _mm(ah, bh) + _mm(ah, bl) + _mm(al, bh)


def _dot2(a, b_bf16):
    ah, al = _split_bf16(a)
    return _mm(ah, b_bf16) + _mm(al, b_bf16)


def _silu(x):
    return x * jax.nn.sigmoid(x)


def _head_sums(x, e2):
    parts = [_dot2(x[:, c * LANES:(c + 1) * LANES], e2) for c in range(x.shape[1] // LANES)]
    return jnp.concatenate(parts, axis=1)


def _ada_kernel(c_ref, w_ref, b_ref, o_ref):
    o_ref[...] = _dot3(_silu(c_ref[...]), w_ref[...]) + b_ref[...]


def _ada(cond8, w, b):
    d, n = w.shape
    tn = 512
    return pl.pallas_call(
        _ada_kernel,
        grid=(n // tn,),
        in_specs=[pl.BlockSpec((SUBLANES, d), lambda j: (0, 0)),
                  pl.BlockSpec((d, tn), lambda j: (0, j)),
                  pl.BlockSpec((1, tn), lambda j: (0, j))],
        out_specs=pl.BlockSpec((SUBLANES, tn), lambda j: (0, j)),
        out_shape=jax.ShapeDtypeStruct((SUBLANES, n), F32),
        compiler_params=_cparams(("arbitrary",)),
        name="ada_mod",
    )(cond8, w, b)


def _inproj_kernel(x_ref, mod_ref, nw_ref, w_ref, o_ref, h_scr):
    @pl.when(pl.program_id(1) == 0)
    def _():
        x = x_ref[...]
        y = x * lax.rsqrt(jnp.mean(x * x, axis=-1, keepdims=True) + EPS) * nw_ref[...]
        h = y * (1.0 + mod_ref[0, 1:2, :]) + mod_ref[0, 0:1, :]
        h_scr[...] = h.astype(BF16)

    o_ref[...] = _mm(h_scr[...], w_ref[...])


def _inproj(x, mod, norm_w, w_bf16, cond_of_tile, tm, tn):
    t, d = x.shape
    n = w_bf16.shape[1]
    return pl.pallas_call(
        _inproj_kernel,
        grid=(t // tm, n // tn),
        in_specs=[pl.BlockSpec((tm, d), lambda i, j: (i, 0)),
                  pl.BlockSpec((1, 3, d), lambda i, j: (cond_of_tile(i), 0, 0)),
                  pl.BlockSpec((1, d), lambda i, j: (0, 0)),
                  pl.BlockSpec((d, tn), lambda i, j: (0, j))],
        out_specs=pl.BlockSpec((tm, tn), lambda i, j: (i, j)),
        out_shape=jax.ShapeDtypeStruct((t, n), F32),
        scratch_shapes=[pltpu.VMEM((tm, d), BF16)],
        compiler_params=_cparams(("parallel", "arbitrary")),
        name="inproj",
    )(x, mod, norm_w, w_bf16)


def _rope_tab_kernel(inv_ref, c_ref, s1_ref, s2_ref, *, tr):
    t = lax.broadcasted_iota(jnp.int32, (tr, LANES), 0) + pl.program_id(0) * tr
    lane = lax.broadcasted_iota(jnp.int32, (tr, LANES), 1)
    row = (t // GRID_W).astype(F32)
    col = (t % GRID_W).astype(F32)
    pos = jnp.where((lane % DH_A) >= DH_A // 2, col, row)
    ang = pos * inv_ref[...]
    c = jnp.cos(ang)
    s = jnp.sin(ang)
    first = (lane % (DH_A // 2)) < DH_A // 4
    c_ref[...] = c
    s1_ref[...] = jnp.where(first, -s, 0.0)
    s2_ref[...] = jnp.where(first, 0.0, s)


def _rope_tables(ls):
    tr = min(ls, 512)
    quarter = DH_A // 4
    inv = ROPE_BASE ** (-jnp.arange(quarter, dtype=F32) / quarter)
    inv = jnp.tile(inv, LANES // quarter)[None, :]
    shp = jax.ShapeDtypeStruct((ls, LANES), F32)
    spec = pl.BlockSpec((tr, LANES), lambda i: (i, 0))
    return pl.pallas_call(
        functools.partial(_rope_tab_kernel, tr=tr),
        grid=(ls // tr,),
        in_specs=[pl.BlockSpec((1, LANES), lambda i: (0, 0))],
        out_specs=[spec, spec, spec],
        out_shape=[shp, shp, shp],
        compiler_params=_cparams(("arbitrary",)),
        name="rope_tables",
    )(inv)


def _rope_kernel(x_ref, c_ref, s1_ref, s2_ref, o_ref, *, n_q):
    x = x_ref[...]
    quarter = DH_A // 4
    y = (x * c_ref[...] + pltpu.roll(x, LANES - quarter, 1) * s1_ref[...]
         + pltpu.roll(x, quarter, 1) * s2_ref[...])
    scale = jnp.where(pl.program_id(1) < n_q, DH_A ** -0.5, 1.0)
    o_ref[...] = (y * scale).astype(BF16)


def _rope_qk(p, tabs, row0, ts, ls, col0, n_heads):
    tr = min(ls, 512)
    tab_spec = pl.BlockSpec((tr, LANES), lambda i, j: (i % (ls // tr), 0))
    return pl.pallas_call(
        functools.partial(_rope_kernel, n_q=n_heads),
        grid=(ts // tr, 2 * n_heads),
        in_specs=[pl.BlockSpec((tr, LANES), lambda i, j: (row0 // tr + i, col0 // LANES + j)),
                  tab_spec, tab_spec, tab_spec],
        out_specs=pl.BlockSpec((tr, LANES), lambda i, j: (i, j)),
        out_shape=jax.ShapeDtypeStruct((ts, 2 * n_heads * LANES), BF16),
        compiler_params=_cparams(("parallel", "arbitrary")),
        name="rope_qk",
    )(p, *tabs)


def _attn_kernel(lam_ref, sw_ref, q_ref, k_ref, v_ref, g_ref, o_ref, *, lam_init, q_scale):
    lp = lam_ref[...]
    lam = (jnp.exp(jnp.sum(lp[0:1, :] * lp[1:2, :], axis=1, keepdims=True))
           - jnp.exp(jnp.sum(lp[2:3, :] * lp[3:4, :], axis=1, keepdims=True)) + lam_init)
    q = q_ref[...]
    if q_scale != 1.0:
        q = q * q_scale
    q = q.astype(BF16)
    k = k_ref[...].astype(BF16)
    v = v_ref[...].astype(BF16)
    lane = lax.broadcasted_iota(jnp.int32, q.shape, 1)
    zero = jnp.zeros_like(q)
    nt = (((1,), (1,)), ((), ()))

    def softmax_map(qm):
        s = lax.dot_general(qm, k, nt, preferred_element_type=F32)
        e = jnp.exp(s - jnp.max(s, axis=-1, keepdims=True))
        return e, 1.0 / jnp.sum(e, axis=-1, keepdims=True)

    e1, r1 = softmax_map(jnp.where(lane < DH_A, q, zero))
    e2, r2 = softmax_map(jnp.where(lane < DH_A, zero, q))
    p = e1 * r1 - e2 * (lam * r2)
    o = _mm(p.astype(BF16), v)
    o = o * lax.rsqrt(jnp.mean(o * o, axis=-1, keepdims=True) + EPS) * sw_ref[...] * (1.0 - lam_init)
    o_ref[...] = (o * _silu(g_ref[...])).astype(BF16)


def _attention(lam_p, subln_w, q_arr, q_map, k_arr, k_map, v_arr, v_map, g_arr, g_map,
               n_b, n_heads, lq, lk, tq, lam_init, q_scale):
    nq = lq // tq
    return pl.pallas_call(
        functools.partial(_attn_kernel, lam_init=lam_init, q_scale=q_scale),
        grid=(n_b, n_heads, nq),
        in_specs=[pl.BlockSpec(lam_p.shape, lambda b, h, i: (0, 0)),
                  pl.BlockSpec((1, HEAD_A), lambda b, h, i: (0, 0)),
                  pl.BlockSpec((tq, HEAD_A), q_map),
                  pl.BlockSpec((lk, HEAD_A), k_map),
                  pl.BlockSpec((lk, HEAD_A), v_map),
                  pl.BlockSpec((tq, HEAD_A), g_map)],
        out_specs=pl.BlockSpec((tq, HEAD_A), lambda b, h, i: (b * nq + i, h)),
        out_shape=jax.ShapeDtypeStruct((n_b * lq, n_heads * HEAD_A), BF16),
        compiler_params=_cparams(("parallel", "parallel", "arbitrary")),
        name="diff_attn",
    )(lam_p, subln_w, q_arr, k_arr, v_arr, g_arr)


def _conv_kernel(u_ref, up_ref, un_ref, g_ref, cw_ref, cb_ref, lw_ref, lb_ref, o_ref, zp_scr, acc_scr,
                 *, tt, n_prompt_blocks, blocks_per_prompt_seq, blocks_per_sample_seq):
    cb_ = u_ref.shape[1] // 2
    blk = pl.program_id(0)
    in_prompt = blk < n_prompt_blocks
    pos = jnp.where(in_prompt, blk % blocks_per_prompt_seq, (blk - n_prompt_blocks) % blocks_per_sample_seq)
    per_seq = jnp.where(in_prompt, blocks_per_prompt_seq, blocks_per_sample_seq)
    keep_prev = (pos > 0).astype(F32)
    keep_next = (pos < per_seq - 1).astype(F32)

    def glu(ref):
        u = ref[...]
        return u[:, :cb_] * jax.nn.sigmoid(u[:, cb_:])

    h = CONV_HALO
    zp_scr[pl.ds(0, h), :] = glu(up_ref) * keep_prev
    zp_scr[pl.ds(h, tt), :] = glu(u_ref)
    zp_scr[pl.ds(h + tt, h), :] = glu(un_ref) * keep_next

    rc = 64
    off = h - CONV_W // 2

    def lane_chunk(c, carry):
        ls = pl.ds(pl.multiple_of(c * LANES, LANES), LANES)
        w = cw_ref[:, ls]
        for r0 in range(0, tt, rc):
            acc = jnp.zeros((rc, LANES), F32)
            for tau in range(CONV_W):
                acc = acc + zp_scr[pl.ds(r0 + tau + off, rc), ls] * w[tau:tau + 1, :]
            acc_scr[pl.ds(r0, rc), ls] = acc
        return carry

    lax.fori_loop(0, cb_ // LANES, lane_chunk, 0)

    z = acc_scr[...] + cb_ref[...]
    mu = jnp.mean(z, axis=-1, keepdims=True)
    d = z - mu
    var = jnp.mean(d * d, axis=-1, keepdims=True)
    zn = d * lax.rsqrt(var + LN_EPS) * lw_ref[...] + lb_ref[...]
    o_ref[...] = (_silu(zn) * _silu(g_ref[...])).astype(BF16)


def _conv(p, conv_w, conv_b, ln_w, ln_b, glu_col0, gb_col0, tt, n_prompt_blocks,
          blocks_per_prompt_seq, blocks_per_sample_seq):
    t = p.shape[0]
    cb_ = conv_w.shape[1]
    nblk = t // tt
    hb = tt // CONV_HALO
    n_hblk = t // CONV_HALO
    vec = pl.BlockSpec((1, cb_), lambda i: (0, 0))
    return pl.pallas_call(
        functools.partial(_conv_kernel, tt=tt, n_prompt_blocks=n_prompt_blocks,
                          blocks_per_prompt_seq=blocks_per_prompt_seq,
                          blocks_per_sample_seq=blocks_per_sample_seq),
        grid=(nblk,),
        in_specs=[pl.BlockSpec((tt, 2 * cb_), lambda i: (i, glu_col0 // (2 * cb_))),
                  pl.BlockSpec((CONV_HALO, 2 * cb_), lambda i: (jnp.maximum(i * hb - 1, 0), glu_col0 // (2 * cb_))),
                  pl.BlockSpec((CONV_HALO, 2 * cb_),
                               lambda i: (jnp.minimum((i + 1) * hb, n_hblk - 1), glu_col0 // (2 * cb_))),
                  pl.BlockSpec((tt, cb_), lambda i: (i, gb_col0 // cb_)),
                  pl.BlockSpec(conv_w.shape, lambda i: (0, 0)),
                  vec, vec, vec],
        out_specs=pl.BlockSpec((tt, cb_), lambda i: (i, 0)),
        out_shape=jax.ShapeDtypeStruct((t, cb_), BF16),
        scratch_shapes=[pltpu.VMEM((tt + 2 * CONV_HALO, cb_), F32), pltpu.VMEM((tt, cb_), F32)],
        compiler_params=_cparams(("parallel",)),
        name="conformer_conv",
    )(p, p, p, p, conv_w, conv_b, ln_w, ln_b)


def _rwkv_prep_kernel(k_ref, xw_ref, xa_ref, w0_ref, wup_ref, a0_ref, aup_ref, kk_ref, ka_ref, e2_ref,
                      okk_ref, odec_ref, okd_ref, obd_ref):
    k = k_ref[...]
    e2 = e2_ref[...]
    kkr = k * kk_ref[...]
    kk = kkr / jnp.maximum(jnp.sqrt(_head_sums(kkr * kkr, e2)), 1e-12)
    okk_ref[...] = kk
    txw = jnp.tanh(xw_ref[...])
    xa = xa_ref[...]
    for d in range(2):
        w_log = -jax.nn.softplus(-(w0_ref[d:d + 1, :] + _dot3(txw, wup_ref[d]))) - 0.5
        odec_ref[d] = jnp.exp(-jnp.exp(w_log))
        a = jax.nn.sigmoid(a0_ref[d:d + 1, :] + _dot3(xa, aup_ref[d]))
        okd_ref[d] = k * (1.0 + (a - 1.0) * ka_ref[...])
        obd_ref[d] = kk * a


def _rwkv_prep(p, kc_col0, xw_col0, xa_col0, w0, wup_pad, a0, aup_pad, k_k, k_a, e2, tm):
    t = p.shape[0]
    cc = w0.shape[1]
    vec = pl.BlockSpec((1, cc), lambda i: (0, 0))
    vec2 = pl.BlockSpec((2, cc), lambda i: (0, 0))
    mat = pl.BlockSpec((2, LANES, cc), lambda i: (0, 0, 0))
    one = jax.ShapeDtypeStruct((t, cc), F32)
    two = jax.ShapeDtypeStruct((2, t, cc), F32)
    two_spec = pl.BlockSpec((2, tm, cc), lambda i: (0, i, 0))
    return pl.pallas_call(
        _rwkv_prep_kernel,
        grid=(t // tm,),
        in_specs=[pl.BlockSpec((tm, cc), lambda i: (i, kc_col0 // cc)),
                  pl.BlockSpec((tm, LANES), lambda i: (i, xw_col0 // LANES)),
                  pl.BlockSpec((tm, LANES), lambda i: (i, xa_col0 // LANES)),
                  vec2, mat, vec2, mat, vec, vec,
                  pl.BlockSpec((LANES, LANES), lambda i: (0, 0))],
        out_specs=[pl.BlockSpec((tm, cc), lambda i: (i, 0)), two_spec, two_spec, two_spec],
        out_shape=[one, two, two, two],
        compiler_params=_cparams(("parallel",)),
        name="rwkv_prep",
    )(p, p, p, w0, wup_pad, a0, aup_pad, k_k, k_a, e2)


def _scan_kernel(w_ref, b_ref, k_ref, kk_ref, r_ref, v_ref, s0_ref, y_ref, sf_ref, s_scr, *, tb, ni):
    @pl.when(pl.program_id(1) == 0)
    def _():
        s_scr[...] = s0_ref[...]

    def step(t, carry):
        def i_group(ig, c2):
            rows = []
            for ii in range(SUBLANES):
                i = ig * SUBLANES + ii
                si = s_scr[i]
                sa = -jnp.sum(si * kk_ref[t], axis=0, keepdims=True)
                vi = v_ref[t, pl.ds(i, 1), :]
                sn = si * w_ref[t] + sa * b_ref[t] + vi * k_ref[t]
                s_scr[i] = sn
                rows.append(jnp.sum(sn * r_ref[t], axis=0, keepdims=True))
            y_ref[t, pl.ds(pl.multiple_of(ig * SUBLANES, SUBLANES), SUBLANES), :] = jnp.concatenate(rows, axis=0)
            return c2

        lax.fori_loop(0, ni // SUBLANES, i_group, 0)
        return carry

    lax.fori_loop(0, tb, step, 0)

    @pl.when(pl.program_id(1) == pl.num_programs(1) - 1)
    def _():
        sf_ref[...] = s_scr[...]


def _scan(w, b, k, kk, r, v, s0, tb):
    l, nj, nl = w.shape
    ni = v.shape[1]
    jspec = pl.BlockSpec((tb, nj, LANES), lambda g, t: (t, 0, g))
    ispec = pl.BlockSpec((tb, ni, LANES), lambda g, t: (t, 0, g))
    sspec = pl.BlockSpec((ni, nj, LANES), lambda g, t: (0, 0, g))
    return pl.pallas_call(
        functools.partial(_scan_kernel, tb=tb, ni=ni),
        grid=(nl // LANES, l // tb),
        in_specs=[jspec, jspec, jspec, jspec, jspec, ispec, sspec],
        out_specs=[ispec, sspec],
        out_shape=[jax.ShapeDtypeStruct((l, ni, nl), F32), jax.ShapeDtypeStruct((ni, nj, nl), F32)],
        scratch_shapes=[pltpu.VMEM((ni, nj, LANES), F32)],
        compiler_params=_cparams(("parallel", "arbitrary")),
        name="rwkv_scan",
    )(w, b, k, kk, r, v, s0)


def _rwkv_post_kernel(y_ref, r_ref, v_ref, kd_ref, g_ref, rk_ref, gw_ref, gb_ref, e2_ref, o_ref):
    e2 = e2_ref[...]
    y = y_ref[...]
    inv_n = 1.0 / N_C
    d = y - _head_sums(y, e2) * inv_n
    var = _head_sums(d * d, e2) * inv_n
    yn = d * lax.rsqrt(var + GN_EPS) * gw_ref[...] + gb_ref[...]
    bonus = _head_sums(r_ref[...] * (kd_ref[0] + kd_ref[1]) * rk_ref[...], e2) * v_ref[...]
    o_ref[...] = ((yn + bonus) * _silu(g_ref[...])).astype(BF16)


def _rwkv_post(y, p, kd, rc_col0, vc_col0, gc_col0, r_k, gn_w, gn_b, e2, tm):
    t, cc = y.shape
    vec = pl.BlockSpec((1, cc), lambda i: (0, 0))
    return pl.pallas_call(
        _rwkv_post_kernel,
        grid=(t // tm,),
        in_specs=[pl.BlockSpec((tm, cc), lambda i: (i, 0)),
                  pl.BlockSpec((tm, cc), lambda i: (i, rc_col0 // cc)),
                  pl.BlockSpec((tm, cc), lambda i: (i, vc_col0 // cc)),
                  pl.BlockSpec((2, tm, cc), lambda i: (0, i, 0)),
                  pl.BlockSpec((tm, cc), lambda i: (i, gc_col0 // cc)),
                  vec, vec, vec,
                  pl.BlockSpec((LANES, LANES), lambda i: (0, 0))],
        out_specs=pl.BlockSpec((tm, cc), lambda i: (i, 0)),
        out_shape=jax.ShapeDtypeStruct((t, cc), BF16),
        compiler_params=_cparams(("parallel",)),
        name="rwkv_post",
    )(y, p, p, kd, p, r_k, gn_w, gn_b, e2)


def _out_kernel(ba_ref, bb_ref, bc_ref, m0_ref, m1_ref, m2_ref, wa_ref, wb_ref, wc_ref, wo_ref,
                x_ref, mod_ref, fw_ref, o_ref, acc_scr, *, final):
    j = pl.program_id(1)
    merged = (jax.nn.sigmoid(m0_ref[...]) * _mm(ba_ref[...], wa_ref[...])
              + jax.nn.sigmoid(m1_ref[...]) * _mm(bb_ref[...], wb_ref[...])
              + jax.nn.sigmoid(m2_ref[...]) * _mm(bc_ref[...], wc_ref[...]))
    part = _mm(merged.astype(BF16), wo_ref[...])

    @pl.when(j == 0)
    def _():
        acc_scr[...] = part

    @pl.when(j > 0)
    def _():
        acc_scr[...] += part

    @pl.when(j == pl.num_programs(1) - 1)
    def _():
        x = x_ref[...] + mod_ref[0, 2:3, :] * acc_scr[...]
        if final:
            x = x * lax.rsqrt(jnp.mean(x * x, axis=-1, keepdims=True) + EPS) * fw_ref[...]
        o_ref[...] = x


def _out_proj(ba, bb, bc, p, wa, wb, wc, wo, x, mod, final_w, cond_of_tile, tm, tn, final):
    t, d = x.shape
    wbr = ba.shape[1]
    nj = d // tn
    br = pl.BlockSpec((tm, wbr), lambda i, j: (i, 0))
    wspec = pl.BlockSpec((wbr, tn), lambda i, j: (0, j))
    return pl.pallas_call(
        functools.partial(_out_kernel, final=final),
        grid=(t // tm, nj),
        in_specs=[br, br, br,
                  pl.BlockSpec((tm, tn), lambda i, j: (i, j)),
                  pl.BlockSpec((tm, tn), lambda i, j: (i, nj + j)),
                  pl.BlockSpec((tm, tn), lambda i, j: (i, 2 * nj + j)),
                  wspec, wspec, wspec,
                  pl.BlockSpec((tn, d), lambda i, j: (j, 0)),
                  pl.BlockSpec((tm, d), lambda i, j: (i, 0)),
                  pl.BlockSpec((1, 3, d), lambda i, j: (cond_of_tile(i), 0, 0)),
                  pl.BlockSpec((1, d), lambda i, j: (0, 0))],
        out_specs=pl.BlockSpec((tm, d), lambda i, j: (i, 0)),
        out_shape=jax.ShapeDtypeStruct((t, d), F32),
        scratch_shapes=[pltpu.VMEM((tm, d), F32)],
        compiler_params=_cparams(("parallel", "arbitrary")),
        name="out_proj",
    )(ba, bb, bc, p, p, p, wa, wb, wc, wo, x, mod, final_w)


def _to_chain(x, nb, l, nh):
    return x.reshape(nb, l, nh, N_C).transpose(1, 3, 0, 2).reshape(l, N_C, nb * nh)


def _both_dirs(xf, xb):
    return jnp.concatenate([xf, jnp.flip(xb, axis=0)], axis=-1)


def _from_chain(y, nb, l, nh):
    y = y.reshape(l, N_C, 2, nb * nh)
    y = y[:, :, 0] + jnp.flip(y[:, :, 1], axis=0)
    return y.reshape(l, N_C, nb, nh).transpose(2, 0, 3, 1).reshape(nb * l, nh * N_C)


def _largest_tile(n, cap):
    t = cap
    while n % t:
        t //= 2
    return t


def kernel(x_prompt, x_sample, cache_k, cache_v, state_rwkv, c, c_ctx, w_ada, b_ada, norm_w, w_in,
           lambda_q1, lambda_k1, lambda_q2, lambda_k2, subln_w, conv_w, conv_b, conv_ln_w, conv_ln_b,
           rwkv_w0, rwkv_w_up, rwkv_a0, rwkv_a_up, rwkv_k_k, rwkv_k_a, rwkv_r_k, rwkv_gn_w, rwkv_gn_b,
           w_br_a, w_br_b, w_br_c, w_out, final_norm_w):
    bp, lp_, d = x_prompt.shape
    bs, ls, _ = x_sample.shape
    depth = w_ada.shape[0]
    past = cache_k.shape[2]
    tp, ts = bp * lp_, bs * ls
    t_all = tp + ts
    n_ha = d // 256
    w_a = n_ha * HEAD_A
    c_b = d // 2
    n_hc = d // (2 * N_C)
    c_c = n_hc * N_C
    n_mg = 3 * d
    assert n_ha * 2 * DH_A == w_a and w_a == c_b == c_c
    assert bs + 1 <= SUBLANES

    col = {}
    off = 0
    for name, width in (("mg", n_mg), ("q", w_a), ("k", w_a), ("v", w_a), ("g_a", w_a), ("glu", 2 * c_b),
                        ("g_b", c_b), ("r_c", c_c), ("k_c", c_c), ("v_c", c_c), ("g_c", c_c),
                        ("xw", 2 * LORA), ("xa", 2 * LORA)):
        col[name] = off
        off += width
    in_cols = off
    n_lead = in_cols - n_mg

    tm_in = _largest_tile(math.gcd(tp, ls), 1024)
    tm_out = _largest_tile(math.gcd(tp, ls), 256)
    tn_in = 768 if in_cols % 768 == 0 else 256
    tn_out = 512
    tt = _largest_tile(math.gcd(lp_, ls), 256)
    tm_rw = _largest_tile(math.gcd(tp, ts), 256)

    def cond_of(tm):
        npt = tp // tm
        return lambda i: jnp.where(i < npt, 0, 1 + ((i - npt) * tm) // ls)

    x = jnp.concatenate([x_prompt.reshape(tp, d), x_sample.reshape(ts, d)], axis=0)
    cond8 = jnp.zeros((SUBLANES, d), F32).at[0].set(c_ctx).at[1:1 + bs].set(c)
    rope_tabs = _rope_tables(ls)
    blk = jnp.arange(LANES) // N_C
    e2 = (blk[:, None] == blk[None, :]).astype(BF16)
    zpad = jnp.zeros((LORA, c_c), F32)

    ks, vs, ss = [], [], []
    for l in range(depth):
        lam_init = 0.8 - 0.6 * math.exp(-0.3 * l)
        mod = _ada(cond8, w_ada[l], b_ada[l][None, :]).reshape(SUBLANES, 3, d)
        w_perm = jnp.concatenate([w_in[l][:, n_lead:], w_in[l][:, :n_lead]], axis=1).astype(BF16)
        p = _inproj(x, mod, norm_w[l][None, :], w_perm, cond_of(tm_in), tm_in, tn_in)

        lam_p = jnp.stack([lambda_q1[l], lambda_k1[l], lambda_q2[l], lambda_k2[l]], axis=0)
        sw = subln_w[l][None, :]
        cq, ck, cv, cg = (col[n] // HEAD_A for n in ("q", "k", "v", "g_a"))
        tq_p = _largest_tile(lp_, 256)
        npq = lp_ // tq_p
        ba_p = _attention(
            lam_p, sw,
            p, lambda b, h, i: (b * npq + i, cq + h),
            p, lambda b, h, i: (b, ck + h),
            p, lambda b, h, i: (b, cv + h),
            p, lambda b, h, i: (b * npq + i, cg + h),
            bp, n_ha, lp_, lp_, tq_p, lam_init, DH_A ** -0.5)
        qk_rot = _rope_qk(p, rope_tabs, tp, ts, ls, col["q"], n_ha)
        k_all = jnp.concatenate([cache_k[:, l].reshape(bs, past, w_a).astype(BF16),
                                 qk_rot[:, w_a:].reshape(bs, ls, w_a)], axis=1).reshape(bs * (past + ls), w_a)
        v_all = jnp.concatenate([cache_v[:, l].reshape(bs, past, w_a).astype(BF16),
                                 p[tp:, col["v"]:col["v"] + w_a].astype(BF16).reshape(bs, ls, w_a)],
                                axis=1).reshape(bs * (past + ls), w_a)
        tq_s = _largest_tile(ls, 256)
        nsq = ls // tq_s
        ba_s = _attention(
            lam_p, sw,
            qk_rot, lambda b, h, i: (b * nsq + i, h),
            k_all, lambda b, h, i: (b, h),
            v_all, lambda b, h, i: (b, h),
            p, lambda b, h, i: (tp // tq_s + b * nsq + i, cg + h),
            bs, n_ha, ls, past + ls, tq_s, lam_init, 1.0)
        ba = jnp.concatenate([ba_p, ba_s], axis=0)

        bb = _conv(p, conv_w[l], conv_b[l][None, :], conv_ln_w[l][None, :], conv_ln_b[l][None, :],
                   col["glu"], col["g_b"], tt, tp // tt, lp_ // tt, ls // tt)

        wup_pad = jnp.stack([jnp.concatenate([rwkv_w_up[l, 0], zpad], 0), jnp.concatenate([zpad, rwkv_w_up[l, 1]], 0)])
        aup_pad = jnp.stack([jnp.concatenate([rwkv_a_up[l, 0], zpad], 0), jnp.concatenate([zpad, rwkv_a_up[l, 1]], 0)])
        kk, dec, kd, bd = _rwkv_prep(p, col["k_c"], col["xw"], col["xa"], rwkv_w0[l], wup_pad, rwkv_a0[l], aup_pad,
                                     rwkv_k_k[l][None, :], rwkv_k_a[l][None, :], e2, tm_rw)
        r_nat = p[:, col["r_c"]:col["r_c"] + c_c]
        v_nat = p[:, col["v_c"]:col["v_c"] + c_c]

        def chains(arr_f, arr_b, lo, hi, nb, ll):
            return _both_dirs(_to_chain(arr_f[lo:hi], nb, ll, n_hc), _to_chain(arr_b[lo:hi], nb, ll, n_hc))

        args_p = [chains(a0_, a1_, 0, tp, bp, lp_) for a0_, a1_ in
                  ((dec[0], dec[1]), (bd[0], bd[1]), (kd[0], kd[1]), (kk, kk), (r_nat, r_nat), (v_nat, v_nat))]
        nl_p = 2 * bp * n_hc
        y_p, sf_p = _scan(*args_p, jnp.zeros((N_C, N_C, nl_p), F32), _largest_tile(lp_, 32))
        yc_p = _from_chain(y_p, bp, lp_, n_hc)
        ss.append(sf_p.reshape(N_C, N_C, 2, bp, n_hc).transpose(3, 2, 4, 0, 1))

        nc_s = 2 * bs * n_hc
        half = N_C // 2
        args_s = [chains(a0_, a1_, tp, t_all, bs, ls) for a0_, a1_ in
                  ((dec[0], dec[1]), (bd[0], bd[1]), (kd[0], kd[1]), (kk, kk), (r_nat, r_nat))]
        args_s = [jnp.concatenate([a, a], axis=-1) for a in args_s]
        v_s = chains(v_nat, v_nat, tp, t_all, bs, ls)
        v_s = v_s.reshape(ls, 2, half, nc_s).transpose(0, 2, 1, 3).reshape(ls, half, 2 * nc_s)
        s0 = state_rwkv[:, l].transpose(3, 4, 1, 0, 2).reshape(2, half, N_C, nc_s)
        s0 = s0.transpose(1, 2, 0, 3).reshape(half, N_C, 2 * nc_s)
        y_s, _ = _scan(*args_s, v_s, s0, _largest_tile(ls, 32))
        y_s = y_s.reshape(ls, half, 2, nc_s).transpose(0, 2, 1, 3).reshape(ls, N_C, nc_s)
        yc_s = _from_chain(y_s, bs, ls, n_hc)

        yc = jnp.concatenate([yc_p, yc_s], axis=0)
        bc = _rwkv_post(yc, p, kd, col["r_c"], col["v_c"], col["g_c"], rwkv_r_k[l].reshape(1, c_c),
                        rwkv_gn_w[l][None, :], rwkv_gn_b[l][None, :], e2, tm_rw)

        x = _out_proj(ba, bb, bc, p, w_br_a[l].astype(BF16), w_br_b[l].astype(BF16), w_br_c[l].astype(BF16),
                      w_out[l].astype(BF16), x, mod, final_norm_w[None, :], cond_of(tm_out), tm_out, tn_out,
                      final=(l == depth - 1))

        ks.append(p[:tp, col["k"]:col["k"] + w_a].reshape(bp, lp_, n_ha, 2, DH_A))
        vs.append(p[:tp, col["v"]:col["v"] + w_a].reshape(bp, lp_, n_ha, HEAD_A))

    y_prompt = x[:tp].reshape(bp, lp_, d)
    y_sample = x[tp:].reshape(bs, ls, d)
    return (y_prompt, y_sample, jnp.stack(ks, axis=1), jnp.stack(vs, axis=1), jnp.stack(ss, axis=1))
```

```python
import functools
import math

import jax
import jax.numpy as jnp
from jax import lax
from jax.experimental import pallas as pl
from jax.experimental.pallas import tpu as pltpu

F32 = jnp.float32
BF16 = jnp.bfloat16

HEAD_A = 128
DH_A = 64
CONV_W = 31
CONV_HALO = 16
N_C = 64
LORA = 64
GRID_W = 64
ROPE_BASE = 10000.0
EPS = 1e-6
LN_EPS = 1e-5
GN_EPS = 64e-5

LANES = 128
SUBLANES = 8
MIB = 1024 * 1024


def _cparams(semantics, vmem_mib=48):
    return pltpu.CompilerParams(dimension_semantics=semantics, vmem_limit_bytes=vmem_mib * MIB)


def _split_bf16(x):
    hi = x.astype(BF16)
    lo = (x - hi.astype(F32)).astype(BF16)
    return hi, lo


def _mm(a, b):
    return jnp.dot(a, b, preferred_element_type=F32)


def _dot3(a, b):
    ah, al = _split_bf16(a)
    bh, bl = _split_bf16(b)
    return _mm(ah, bh) + _mm(ah, bl) + _mm(al, bh)


def _dot2(a, b_bf16):
    ah, al = _split_bf16(a)
    return _mm(ah, b_bf16) + _mm(al, b_bf16)


def _silu(x):
    return x * jax.nn.sigmoid(x)


def _head_sums(x, e2):
    parts = [_dot2(x[:, c * LANES:(c + 1) * LANES], e2) for c in range(x.shape[1] // LANES)]
    return jnp.concatenate(parts, axis=1)


def _ada_kernel(c_ref, w_ref, b_ref, o_ref):
    o_ref[...] = _dot3(_silu(c_ref[...]), w_ref[...]) + b_ref[...]


def _ada(cond8, w, b):
    d, n = w.shape
    tn = 512
    return pl.pallas_call(
        _ada_kernel,
        grid=(n // tn,),
        in_specs=[pl.BlockSpec((SUBLANES, d), lambda j: (0, 0)),
                  pl.BlockSpec((d, tn), lambda j: (0, j)),
                  pl.BlockSpec((1, tn), lambda j: (0, j))],
        out_specs=pl.BlockSpec((SUBLANES, tn), lambda j: (0, j)),
        out_shape=jax.ShapeDtypeStruct((SUBLANES, n), F32),
        compiler_params=_cparams(("arbitrary",)),
        name="ada_mod",
    )(cond8, w, b)


def _inproj_kernel(x_ref, mod_ref, nw_ref, w_ref, o_ref, h_scr):
    @pl.when(pl.program_id(1) == 0)
    def _():
        x = x_ref[...]
        y = x * lax.rsqrt(jnp.mean(x * x, axis=-1, keepdims=True) + EPS) * nw_ref[...]
        h = y * (1.0 + mod_ref[0, 1:2, :]) + mod_ref[0, 0:1, :]
        h_scr[...] = h.astype(BF16)

    o_ref[...] = _mm(h_scr[...], w_ref[...])


def _inproj(x, mod, norm_w, w_bf16, cond_of_tile, tm, tn):
    t, d = x.shape
    n = w_bf16.shape[1]
    return pl.pallas_call(
        _inproj_kernel,
        grid=(t // tm, n // tn),
        in_specs=[pl.BlockSpec((tm, d), lambda i, j: (i, 0)),
                  pl.BlockSpec((1, 3, d), lambda i, j: (cond_of_tile(i), 0, 0)),
                  pl.BlockSpec((1, d), lambda i, j: (0, 0)),
                  pl.BlockSpec((d, tn), lambda i, j: (0, j))],
        out_specs=pl.BlockSpec((tm, tn), lambda i, j: (i, j)),
        out_shape=jax.ShapeDtypeStruct((t, n), F32),
        scratch_shapes=[pltpu.VMEM((tm, d), BF16)],
        compiler_params=_cparams(("parallel", "arbitrary")),
        name="inproj",
    )(x, mod, norm_w, w_bf16)


def _rope_tab_kernel(inv_ref, c_ref, s1_ref, s2_ref, *, tr):
    t = lax.broadcasted_iota(jnp.int32, (tr, LANES), 0) + pl.program_id(0) * tr
    lane = lax.broadcasted_iota(jnp.int32, (tr, LANES), 1)
    row = (t // GRID_W).astype(F32)
    col = (t % GRID_W).astype(F32)
    pos = jnp.where((lane % DH_A) >= DH_A // 2, col, row)
    ang = pos * inv_ref[...]
    c = jnp.cos(ang)
    s = jnp.sin(ang)
    first = (lane % (DH_A // 2)) < DH_A // 4
    c_ref[...] = c
    s1_ref[...] = jnp.where(first, -s, 0.0)
    s2_ref[...] = jnp.where(first, 0.0, s)


def _rope_tables(ls):
    tr = min(ls, 512)
    quarter = DH_A // 4
    inv = ROPE_BASE ** (-jnp.arange(quarter, dtype=F32) / quarter)
    inv = jnp.tile(inv, LANES // quarter)[None, :]
    shp = jax.ShapeDtypeStruct((ls, LANES), F32)
    spec = pl.BlockSpec((tr, LANES), lambda i: (i, 0))
    return pl.pallas_call(
        functools.partial(_rope_tab_kernel, tr=tr),
        grid=(ls // tr,),
        in_specs=[pl.BlockSpec((1, LANES), lambda i: (0, 0))],
        out_specs=[spec, spec, spec],
        out_shape=[shp, shp, shp],
        compiler_params=_cparams(("arbitrary",)),
        name="rope_tables",
    )(inv)


def _rope_kernel(x_ref, c_ref, s1_ref, s2_ref, o_ref, *, n_q):
    x = x_ref[...]
    quarter = DH_A // 4
    y = (x * c_ref[...] + pltpu.roll(x, LANES - quarter, 1) * s1_ref[...]
         + pltpu.roll(x, quarter, 1) * s2_ref[...])
    scale = jnp.where(pl.program_id(1) < n_q, DH_A ** -0.5, 1.0)
    o_ref[...] = (y * scale).astype(BF16)


def _rope_qk(p, tabs, row0, ts, ls, col0, n_heads):
    tr = min(ls, 512)
    tab_spec = pl.BlockSpec((tr, LANES), lambda i, j: (i % (ls // tr), 0))
    return pl.pallas_call(
        functools.partial(_rope_kernel, n_q=n_heads),
        grid=(ts // tr, 2 * n_heads),
        in_specs=[pl.BlockSpec((tr, LANES), lambda i, j: (row0 // tr + i, col0 // LANES + j)),
                  tab_spec, tab_spec, tab_spec],
        out_specs=pl.BlockSpec((tr, LANES), lambda i, j: (i, j)),
        out_shape=jax.ShapeDtypeStruct((ts, 2 * n_heads * LANES), BF16),
        compiler_params=_cparams(("parallel", "arbitrary")),
        name="rope_qk",
    )(p, *tabs)


def _attn_kernel(lam_ref, sw_ref, q_ref, k_ref, v_ref, g_ref, o_ref, *, lam_init, q_scale):
    lp = lam_ref[...]
    lam = (jnp.exp(jnp.sum(lp[0:1, :] * lp[1:2, :], axis=1, keepdims=True))
           - jnp.exp(jnp.sum(lp[2:3, :] * lp[3:4, :], axis=1, keepdims=True)) + lam_init)
    q = q_ref[...]
    if q_scale != 1.0:
        q = q * q_scale
    q = q.astype(BF16)
    k = k_ref[...].astype(BF16)
    v = v_ref[...].astype(BF16)
    lane = lax.broadcasted_iota(jnp.int32, q.shape, 1)
    zero = jnp.zeros_like(q)
    nt = (((1,), (1,)), ((), ()))

    def softmax_map(qm):
        s = lax.dot_general(qm, k, nt, preferred_element_type=F32)
        e = jnp.exp(s - jnp.max(s, axis=-1, keepdims=True))
        return e, 1.0 / jnp.sum(e, axis=-1, keepdims=True)

    e1, r1 = softmax_map(jnp.where(lane < DH_A, q, zero))
    e2, r2 = softmax_map(jnp.where(lane < DH_A, zero, q))
    p = e1 * r1 - e2 * (lam * r2)
    o = _mm(p.astype(BF16), v)
    o = o * lax.rsqrt(jnp.mean(o * o, axis=-1, keepdims=True) + EPS) * sw_ref[...] * (1.0 - lam_init)
    o_ref[...] = (o * _silu(g_ref[...])).astype(BF16)


def _attention(lam_p, subln_w, q_arr, q_map, k_arr, k_map, v_arr, v_map, g_arr, g_map,
               n_b, n_heads, lq, lk, tq, lam_init, q_scale):
    nq = lq // tq
    return pl.pallas_call(
        functools.partial(_attn_kernel, lam_init=lam_init, q_scale=q_scale),
        grid=(n_b, n_heads, nq),
        in_specs=[pl.BlockSpec(lam_p.shape, lambda b, h, i: (0, 0)),
                  pl.BlockSpec((1, HEAD_A), lambda b, h, i: (0, 0)),
                  pl.BlockSpec((tq, HEAD_A), q_map),
                  pl.BlockSpec((lk, HEAD_A), k_map),
                  pl.BlockSpec((lk, HEAD_A), v_map),
                  pl.BlockSpec((tq, HEAD_A), g_map)],
        out_specs=pl.BlockSpec((tq, HEAD_A), lambda b, h, i: (b * nq + i, h)),
        out_shape=jax.ShapeDtypeStruct((n_b * lq, n_heads * HEAD_A), BF16),
        compiler_params=_cparams(("parallel", "parallel", "arbitrary")),
        name="diff_attn",
    )(lam_p, subln_w, q_arr, k_arr, v_arr, g_arr)


def _conv_kernel(u_ref, up_ref, un_ref, g_ref, cw_ref, cb_ref, lw_ref, lb_ref, o_ref, zp_scr, acc_scr,
                 *, tt, n_prompt_blocks, blocks_per_prompt_seq, blocks_per_sample_seq):
    cb_ = u_ref.shape[1] // 2
    blk = pl.program_id(0)
    in_prompt = blk < n_prompt_blocks
    pos = jnp.where(in_prompt, blk % blocks_per_prompt_seq, (blk - n_prompt_blocks) % blocks_per_sample_seq)
    per_seq = jnp.where(in_prompt, blocks_per_prompt_seq, blocks_per_sample_seq)
    keep_prev = (pos > 0).astype(F32)
    keep_next = (pos < per_seq - 1).astype(F32)

    def glu(ref):
        u = ref[...]
        return u[:, :cb_] * jax.nn.sigmoid(u[:, cb_:])

    h = CONV_HALO
    zp_scr[pl.ds(0, h), :] = glu(up_ref) * keep_prev
    zp_scr[pl.ds(h, tt), :] = glu(u_ref)
    zp_scr[pl.ds(h + tt, h), :] = glu(un_ref) * keep_next

    rc = 64
    off = h - CONV_W // 2

    def lane_chunk(c, carry):
        ls = pl.ds(pl.multiple_of(c * LANES, LANES), LANES)
        w = cw_ref[:, ls]
        for r0 in range(0, tt, rc):
            acc = jnp.zeros((rc, LANES), F32)
            for tau in range(CONV_W):
                acc = acc + zp_scr[pl.ds(r0 + tau + off, rc), ls] * w[tau:tau + 1, :]
            acc_scr[pl.ds(r0, rc), ls] = acc
        return carry

    lax.fori_loop(0, cb_ // LANES, lane_chunk, 0)

    z = acc_scr[...] + cb_ref[...]
    mu = jnp.mean(z, axis=-1, keepdims=True)
    d = z - mu
    var = jnp.mean(d * d, axis=-1, keepdims=True)
    zn = d * lax.rsqrt(var + LN_EPS) * lw_ref[...] + lb_ref[...]
    o_ref[...] = (_silu(zn) * _silu(g_ref[...])).astype(BF16)


def _conv(p, conv_w, conv_b, ln_w, ln_b, glu_col0, gb_col0, tt, n_prompt_blocks,
          blocks_per_prompt_seq, blocks_per_sample_seq):
    t = p.shape[0]
    cb_ = conv_w.shape[1]
    nblk = t // tt
    hb = tt // CONV_HALO
    n_hblk = t // CONV_HALO
    vec = pl.BlockSpec((1, cb_), lambda i: (0, 0))
    return pl.pallas_call(
        functools.partial(_conv_kernel, tt=tt, n_prompt_blocks=n_prompt_blocks,
                          blocks_per_prompt_seq=blocks_per_prompt_seq,
                          blocks_per_sample_seq=blocks_per_sample_seq),
        grid=(nblk,),
        in_specs=[pl.BlockSpec((tt, 2 * cb_), lambda i: (i, glu_col0 // (2 * cb_))),
                  pl.BlockSpec((CONV_HALO, 2 * cb_), lambda i: (jnp.maximum(i * hb - 1, 0), glu_col0 // (2 * cb_))),
                  pl.BlockSpec((CONV_HALO, 2 * cb_),
                               lambda i: (jnp.minimum((i + 1) * hb, n_hblk - 1), glu_col0 // (2 * cb_))),
                  pl.BlockSpec((tt, cb_), lambda i: (i, gb_col0 // cb_)),
                  pl.BlockSpec(conv_w.shape, lambda i: (0, 0)),
                  vec, vec, vec],
        out_specs=pl.BlockSpec((tt, cb_), lambda i: (i, 0)),
        out_shape=jax.ShapeDtypeStruct((t, cb_), BF16),
        scratch_shapes=[pltpu.VMEM((tt + 2 * CONV_HALO, cb_), F32), pltpu.VMEM((tt, cb_), F32)],
        compiler_params=_cparams(("parallel",)),
        name="conformer_conv",
    )(p, p, p, p, conv_w, conv_b, ln_w, ln_b)


def _rwkv_prep_kernel(k_ref, xw_ref, xa_ref, w0_ref, wup_ref, a0_ref, aup_ref, kk_ref, ka_ref, e2_ref,
                      okk_ref, olw_ref, okd_ref, obd_ref):
    k = k_ref[...]
    e2 = e2_ref[...]
    kkr = k * kk_ref[...]
    kk = kkr / jnp.maximum(jnp.sqrt(_head_sums(kkr * kkr, e2)), 1e-12)
    okk_ref[...] = kk
    txw = jnp.tanh(xw_ref[...])
    xa = xa_ref[...]
    for d in range(2):
        w_log = -jax.nn.softplus(-(w0_ref[d:d + 1, :] + _dot3(txw, wup_ref[d]))) - 0.5
        olw_ref[d] = -jnp.exp(w_log)
        a = jax.nn.sigmoid(a0_ref[d:d + 1, :] + _dot3(xa, aup_ref[d]))
        okd_ref[d] = k * (1.0 + (a - 1.0) * ka_ref[...])
        obd_ref[d] = kk * a


def _rwkv_prep(p, kc_col0, xw_col0, xa_col0, w0, wup_pad, a0, aup_pad, k_k, k_a, e2, tm):
    t = p.shape[0]
    cc = w0.shape[1]
    vec = pl.BlockSpec((1, cc), lambda i: (0, 0))
    vec2 = pl.BlockSpec((2, cc), lambda i: (0, 0))
    mat = pl.BlockSpec((2, LANES, cc), lambda i: (0, 0, 0))
    one = jax.ShapeDtypeStruct((t, cc), F32)
    two = jax.ShapeDtypeStruct((2, t, cc), F32)
    two_spec = pl.BlockSpec((2, tm, cc), lambda i: (0, i, 0))
    return pl.pallas_call(
        _rwkv_prep_kernel,
        grid=(t // tm,),
        in_specs=[pl.BlockSpec((tm, cc), lambda i: (i, kc_col0 // cc)),
                  pl.BlockSpec((tm, LANES), lambda i: (i, xw_col0 // LANES)),
                  pl.BlockSpec((tm, LANES), lambda i: (i, xa_col0 // LANES)),
                  vec2, mat, vec2, mat, vec, vec,
                  pl.BlockSpec((LANES, LANES), lambda i: (0, 0))],
        out_specs=[pl.BlockSpec((tm, cc), lambda i: (i, 0)), two_spec, two_spec, two_spec],
        out_shape=[one, two, two, two],
        compiler_params=_cparams(("parallel",)),
        name="rwkv_prep",
    )(p, p, p, w0, wup_pad, a0, aup_pad, k_k, k_a, e2)


CHUNK = 64
PAIRS_PER_BODY = 4
_NT = (((1,), (1,)), ((), ()))


def _mmb(a, b):
    return _mm(a.astype(BF16), b.astype(BF16))


def _chunk_unit(r, v, kk, lw, c, kd, bd, h, reverse, lane_lo, eye, strict, incl):
    ctot = c[0:1, :] if reverse else c[CHUNK - 1:CHUNK, :]
    g_inv = jnp.exp(-c)
    g_rem = jnp.exp(ctot - c)
    rt = r * jnp.exp(c)

    def stack(x):
        z = jnp.zeros_like(x)
        return jnp.concatenate([jnp.where(lane_lo, x, z), jnp.where(lane_lo, z, x)], axis=0)

    bt_s = stack(-(kk * jnp.exp(c - lw))).astype(BF16)
    v_s = stack(v).astype(BF16)
    lhs = jnp.concatenate([bt_s, stack(rt).astype(BF16)], axis=0)
    rhs = jnp.concatenate([stack(bd * g_inv), stack(kd * g_inv)], axis=0).astype(BF16)
    lhs_t = jnp.concatenate([stack(bd * g_rem).T, stack(kd * g_rem).T], axis=1).astype(BF16)
    nn = lax.dot_general(lhs, rhs, _NT, preferred_element_type=F32)
    yield None
    w2 = 2 * CHUNK
    a_ab = jnp.where(strict, nn[:w2, :w2], 0.0)
    a_bk = jnp.where(strict, nn[:w2, w2:], 0.0)
    a_r = jnp.concatenate([jnp.where(incl, nn[w2:, :w2], 0.0), jnp.where(incl, nn[w2:, w2:], 0.0)], axis=1)
    p1 = _mmb(a_bk, v_s)
    n = a_ab
    t = eye + n
    for _ in range(int(math.log2(CHUNK)) - 1):
        n = _mmb(n, n)
        yield None
        t = t + _mmb(t, n)
    yield None
    wx = _mmb(t, jnp.concatenate([bt_s, p1.astype(BF16)], axis=1))
    yield None
    rhs_big = jnp.concatenate(
        [wx.astype(BF16), jnp.concatenate([jnp.zeros((w2, w2), BF16), v_s], axis=1)], axis=0)
    qy = _mmb(a_r, rhs_big)
    mz = _mm(lhs_t, rhs_big)
    yield None
    qhat = rt + qy[:CHUNK, :w2] + qy[CHUNK:, :w2]
    yhat = qy[:CHUNK, w2:] + qy[CHUNK:, w2:]
    m = jnp.where(eye > 0.0, jnp.exp(ctot), 0.0) + mz[:, :w2]
    res = _dot3(jnp.concatenate([m, qhat], axis=0), h)
    yield res[:w2] + mz[:, w2:], res[w2:] + yhat


def _run_lockstep(gens):
    last = [None] * len(gens)
    live = list(range(len(gens)))
    while live:
        nxt = []
        for i in live:
            try:
                last[i] = next(gens[i])
                nxt.append(i)
            except StopIteration:
                pass
        live = nxt
    return last


def _rwkv_chunk_kernel(rf_ref, vf_ref, kkf_ref, lwf_ref, kdf_ref, bdf_ref,
                       rb_ref, vb_ref, kkb_ref, lwb_ref, kdb_ref, bdb_ref, h0_ref,
                       yf_ref, yb_ref, hfin_ref, h_scr, c_scr):
    @pl.when(pl.program_id(1) == 0)
    def _():
        h_scr[...] = h0_ref[0]

    ti = lax.broadcasted_iota(jnp.int32, (CHUNK, CHUNK), 0)
    si = lax.broadcasted_iota(jnp.int32, (CHUNK, CHUNK), 1)

    def cumsum(tri, x):
        hi = x.astype(BF16)
        r1 = x - hi.astype(F32)
        mid = r1.astype(BF16)
        lo = (r1 - mid.astype(F32)).astype(BF16)
        return _mm(tri, hi) + _mm(tri, mid) + _mm(tri, lo)

    c_scr[0] = cumsum((si <= ti).astype(BF16), lwf_ref[...])
    c_scr[1] = cumsum((si >= ti).astype(BF16), lwb_ref[...])

    w2 = 2 * CHUNK
    row = lax.broadcasted_iota(jnp.int32, (w2, w2), 0)
    col = lax.broadcasted_iota(jnp.int32, (w2, w2), 1)
    tpos, spos = row % CHUNK, col % CHUNK
    eye = (row == col).astype(F32)
    lane_lo = lax.broadcasted_iota(jnp.int32, (CHUNK, w2), 1) < N_C
    dirs = ((rf_ref, vf_ref, kkf_ref, lwf_ref, kdf_ref, bdf_ref, yf_ref, spos < tpos, spos <= tpos),
            (rb_ref, vb_ref, kkb_ref, lwb_ref, kdb_ref, bdb_ref, yb_ref, spos > tpos, spos >= tpos))

    def pair_group(pg, carry):
        units = []
        for pi in range(PAIRS_PER_BODY):
            p = pg * PAIRS_PER_BODY + pi
            ls = pl.ds(pl.multiple_of(p * w2, w2), w2)
            for d, (r_ref, v_ref, kk_ref, lw_ref, kd_ref, bd_ref, y_ref, strict, incl) in enumerate(dirs):
                args = (r_ref[:, ls], v_ref[:, ls], kk_ref[:, ls], lw_ref[:, ls], c_scr[d, :, ls],
                        kd_ref[:, ls], bd_ref[:, ls], h_scr[d, p])
                units.append((d, p, ls, y_ref, args, strict, incl))
        results = _run_lockstep([_chunk_unit(*args, d == 1, lane_lo, eye, strict, incl)
                                 for d, p, ls, y_ref, args, strict, incl in units])
        for (d, p, ls, y_ref, _, _, _), (h_new, y) in zip(units, results):
            h_scr[d, p] = h_new
            y_ref[:, ls] = y
        return carry

    lax.fori_loop(0, rf_ref.shape[1] // (w2 * PAIRS_PER_BODY), pair_group, 0)

    @pl.when(pl.program_id(1) == pl.num_programs(1) - 1)
    def _():
        hfin_ref[0] = h_scr[...]


def _rwkv_chunks(p, kk, lw, kd, bd, h0, row0, nseq, l, rc_col0, vc_col0):
    cc = kk.shape[1]
    nc = l // CHUNK
    rb0 = row0 // CHUNK
    npair = cc // (2 * N_C)

    def fwd(s, g):
        return rb0 + s * nc + g

    def bwd(s, g):
        return rb0 + s * nc + nc - 1 - g

    def specs(rowf, d):
        return [pl.BlockSpec((CHUNK, cc), lambda s, g: (rowf(s, g), rc_col0 // cc)),
                pl.BlockSpec((CHUNK, cc), lambda s, g: (rowf(s, g), vc_col0 // cc)),
                pl.BlockSpec((CHUNK, cc), lambda s, g: (rowf(s, g), 0)),
                pl.BlockSpec((None, CHUNK, cc), lambda s, g: (d, rowf(s, g), 0)),
                pl.BlockSpec((None, CHUNK, cc), lambda s, g: (d, rowf(s, g), 0)),
                pl.BlockSpec((None, CHUNK, cc), lambda s, g: (d, rowf(s, g), 0))]

    hspec = pl.BlockSpec((1, 2, npair, 2 * N_C, 2 * N_C), lambda s, g: (s, 0, 0, 0, 0))
    yshape = jax.ShapeDtypeStruct((nseq * l, cc), F32)
    return pl.pallas_call(
        _rwkv_chunk_kernel,
        grid=(nseq, nc),
        in_specs=specs(fwd, 0) + specs(bwd, 1) + [hspec],
        out_specs=[pl.BlockSpec((CHUNK, cc), lambda s, g: (s * nc + g, 0)),
                   pl.BlockSpec((CHUNK, cc), lambda s, g: (s * nc + nc - 1 - g, 0)),
                   hspec],
        out_shape=[yshape, yshape, jax.ShapeDtypeStruct(h0.shape, F32)],
        scratch_shapes=[pltpu.VMEM((2, npair, 2 * N_C, 2 * N_C), F32), pltpu.VMEM((2, CHUNK, cc), F32)],
        compiler_params=_cparams(("parallel", "arbitrary")),
        name="rwkv_chunks",
    )(p, p, kk, lw, kd, bd, p, p, kk, lw, kd, bd, h0)


def _state_to_wide(s):
    nseq, _, nh, n, _ = s.shape
    ht = s.transpose(0, 1, 2, 4, 3).reshape(nseq, 2, nh // 2, 2, n, n)
    z = jnp.zeros_like(ht[:, :, :, 0])
    top = jnp.concatenate([ht[:, :, :, 0], z], axis=-1)
    bot = jnp.concatenate([z, ht[:, :, :, 1]], axis=-1)
    return jnp.concatenate([top, bot], axis=-2)


def _wide_to_state(h):
    nseq, _, npair, w2, _ = h.shape
    n = w2 // 2
    blocks = jnp.stack([h[..., :n, :n], h[..., n:, n:]], axis=3)
    return blocks.reshape(nseq, 2, 2 * npair, n, n).transpose(0, 1, 2, 4, 3)


def _rwkv_post_kernel(yf_ref, yb_ref, r_ref, v_ref, kd_ref, g_ref, rk_ref, gw_ref, gb_ref, e2_ref, o_ref):
    e2 = e2_ref[...]
    y = yf_ref[...] + yb_ref[...]
    inv_n = 1.0 / N_C
    d = y - _head_sums(y, e2) * inv_n
    var = _head_sums(d * d, e2) * inv_n
    yn = d * lax.rsqrt(var + GN_EPS) * gw_ref[...] + gb_ref[...]
    bonus = _head_sums(r_ref[...] * (kd_ref[0] + kd_ref[1]) * rk_ref[...], e2) * v_ref[...]
    o_ref[...] = ((yn + bonus) * _silu(g_ref[...])).astype(BF16)


def _rwkv_post(yf, yb, p, kd, row0, rc_col0, vc_col0, gc_col0, r_k, gn_w, gn_b, e2, tm):
    t, cc = yf.shape
    rb0 = row0 // tm
    vec = pl.BlockSpec((1, cc), lambda i: (0, 0))
    own = pl.BlockSpec((tm, cc), lambda i: (i, 0))
    return pl.pallas_call(
        _rwkv_post_kernel,
        grid=(t // tm,),
        in_specs=[own, own,
                  pl.BlockSpec((tm, cc), lambda i: (rb0 + i, rc_col0 // cc)),
                  pl.BlockSpec((tm, cc), lambda i: (rb0 + i, vc_col0 // cc)),
                  pl.BlockSpec((2, tm, cc), lambda i: (0, rb0 + i, 0)),
                  pl.BlockSpec((tm, cc), lambda i: (rb0 + i, gc_col0 // cc)),
                  vec, vec, vec,
                  pl.BlockSpec((LANES, LANES), lambda i: (0, 0))],
        out_specs=own,
        out_shape=jax.ShapeDtypeStruct((t, cc), BF16),
        compiler_params=_cparams(("parallel",)),
        name="rwkv_post",
    )(yf, yb, p, p, kd, p, r_k, gn_w, gn_b, e2)


def _out_kernel(ba_ref, bb_ref, bc_ref, m0_ref, m1_ref, m2_ref, wa_ref, wb_ref, wc_ref, wo_ref,
                x_ref, mod_ref, fw_ref, o_ref, acc_scr, *, final):
    j = pl.program_id(1)
    merged = (jax.nn.sigmoid(m0_ref[...]) * _mm(ba_ref[...], wa_ref[...])
              + jax.nn.sigmoid(m1_ref[...]) * _mm(bb_ref[...], wb_ref[...])
              + jax.nn.sigmoid(m2_ref[...]) * _mm(bc_ref[...], wc_ref[...]))
    part = _mm(merged.astype(BF16), wo_ref[...])

    @pl.when(j == 0)
    def _():
        acc_scr[...] = part

    @pl.when(j > 0)
    def _():
        acc_scr[...] += part

    @pl.when(j == pl.num_programs(1) - 1)
    def _():
        x = x_ref[...] + mod_ref[0, 2:3, :] * acc_scr[...]
        if final:
            x = x * lax.rsqrt(jnp.mean(x * x, axis=-1, keepdims=True) + EPS) * fw_ref[...]
        o_ref[...] = x


def _out_proj(ba, bb, bc, p, wa, wb, wc, wo, x, mod, final_w, cond_of_tile, tm, tn, final):
    t, d = x.shape
    wbr = ba.shape[1]
    nj = d // tn
    br = pl.BlockSpec((tm, wbr), lambda i, j: (i, 0))
    wspec = pl.BlockSpec((wbr, tn), lambda i, j: (0, j))
    return pl.pallas_call(
        functools.partial(_out_kernel, final=final),
        grid=(t // tm, nj),
        in_specs=[br, br, br,
                  pl.BlockSpec((tm, tn), lambda i, j: (i, j)),
                  pl.BlockSpec((tm, tn), lambda i, j: (i, nj + j)),
                  pl.BlockSpec((tm, tn), lambda i, j: (i, 2 * nj + j)),
                  wspec, wspec, wspec,
                  pl.BlockSpec((tn, d), lambda i, j: (j, 0)),
                  pl.BlockSpec((tm, d), lambda i, j: (i, 0)),
                  pl.BlockSpec((1, 3, d), lambda i, j: (cond_of_tile(i), 0, 0)),
                  pl.BlockSpec((1, d), lambda i, j: (0, 0))],
        out_specs=pl.BlockSpec((tm, d), lambda i, j: (i, 0)),
        out_shape=jax.ShapeDtypeStruct((t, d), F32),
        scratch_shapes=[pltpu.VMEM((tm, d), F32)],
        compiler_params=_cparams(("parallel", "arbitrary")),
        name="out_proj",
    )(ba, bb, bc, p, p, p, wa, wb, wc, wo, x, mod, final_w)


def _largest_tile(n, cap):
    t = cap
    while n % t:
        t //= 2
    return t


def kernel(x_prompt, x_sample, cache_k, cache_v, state_rwkv, c, c_ctx, w_ada, b_ada, norm_w, w_in,
           lambda_q1, lambda_k1, lambda_q2, lambda_k2, subln_w, conv_w, conv_b, conv_ln_w, conv_ln_b,
           rwkv_w0, rwkv_w_up, rwkv_a0, rwkv_a_up, rwkv_k_k, rwkv_k_a, rwkv_r_k, rwkv_gn_w, rwkv_gn_b,
           w_br_a, w_br_b, w_br_c, w_out, final_norm_w):
    bp, lp_, d = x_prompt.shape
    bs, ls, _ = x_sample.shape
    depth = w_ada.shape[0]
    past = cache_k.shape[2]
    tp, ts = bp * lp_, bs * ls
    n_ha = d // 256
    w_a = n_ha * HEAD_A
    c_b = d // 2
    n_hc = d // (2 * N_C)
    c_c = n_hc * N_C
    n_mg = 3 * d
    assert n_ha * 2 * DH_A == w_a and w_a == c_b == c_c
    assert bs + 1 <= SUBLANES

    col = {}
    off = 0
    for name, width in (("mg", n_mg), ("q", w_a), ("k", w_a), ("v", w_a), ("g_a", w_a), ("glu", 2 * c_b),
                        ("g_b", c_b), ("r_c", c_c), ("k_c", c_c), ("v_c", c_c), ("g_c", c_c),
                        ("xw", 2 * LORA), ("xa", 2 * LORA)):
        col[name] = off
        off += width
    in_cols = off
    n_lead = in_cols - n_mg

    tm_in = _largest_tile(math.gcd(tp, ls), 1024)
    tm_out = _largest_tile(math.gcd(tp, ls), 256)
    tn_in = 768 if in_cols % 768 == 0 else 256
    tn_out = 512
    tt = _largest_tile(math.gcd(lp_, ls), 256)
    tm_rw = _largest_tile(math.gcd(tp, ts), 256)

    def cond_of(tm):
        npt = tp // tm
        return lambda i: jnp.where(i < npt, 0, 1 + ((i - npt) * tm) // ls)

    x = jnp.concatenate([x_prompt.reshape(tp, d), x_sample.reshape(ts, d)], axis=0)
    cond8 = jnp.zeros((SUBLANES, d), F32).at[0].set(c_ctx).at[1:1 + bs].set(c)
    rope_tabs = _rope_tables(ls)
    blk = jnp.arange(LANES) // N_C
    e2 = (blk[:, None] == blk[None, :]).astype(BF16)
    zpad = jnp.zeros((LORA, c_c), F32)

    ks, vs, ss = [], [], []
    for l in range(depth):
        lam_init = 0.8 - 0.6 * math.exp(-0.3 * l)
        mod = _ada(cond8, w_ada[l], b_ada[l][None, :]).reshape(SUBLANES, 3, d)
        w_perm = jnp.concatenate([w_in[l][:, n_lead:], w_in[l][:, :n_lead]], axis=1).astype(BF16)
        p = _inproj(x, mod, norm_w[l][None, :], w_perm, cond_of(tm_in), tm_in, tn_in)

        lam_p = jnp.stack([lambda_q1[l], lambda_k1[l], lambda_q2[l], lambda_k2[l]], axis=0)
        sw = subln_w[l][None, :]
        cq, ck, cv, cg = (col[n] // HEAD_A for n in ("q", "k", "v", "g_a"))
        tq_p = _largest_tile(lp_, 256)
        npq = lp_ // tq_p
        ba_p = _attention(
            lam_p, sw,
            p, lambda b, h, i: (b * npq + i, cq + h),
            p, lambda b, h, i: (b, ck + h),
            p, lambda b, h, i: (b, cv + h),
            p, lambda b, h, i: (b * npq + i, cg + h),
            bp, n_ha, lp_, lp_, tq_p, lam_init, DH_A ** -0.5)
        qk_rot = _rope_qk(p, rope_tabs, tp, ts, ls, col["q"], n_ha)
        k_all = jnp.concatenate([cache_k[:, l].reshape(bs, past, w_a).astype(BF16),
                                 qk_rot[:, w_a:].reshape(bs, ls, w_a)], axis=1).reshape(bs * (past + ls), w_a)
        v_all = jnp.concatenate([cache_v[:, l].reshape(bs, past, w_a).astype(BF16),
                                 p[tp:, col["v"]:col["v"] + w_a].astype(BF16).reshape(bs, ls, w_a)],
                                axis=1).reshape(bs * (past + ls), w_a)
        tq_s = _largest_tile(ls, 256)
        nsq = ls // tq_s
        ba_s = _attention(
            lam_p, sw,
            qk_rot, lambda b, h, i: (b * nsq + i, h),
            k_all, lambda b, h, i: (b, h),
            v_all, lambda b, h, i: (b, h),
            p, lambda b, h, i: (tp // tq_s + b * nsq + i, cg + h),
            bs, n_ha, ls, past + ls, tq_s, lam_init, 1.0)
        ba = jnp.concatenate([ba_p, ba_s], axis=0)

        bb = _conv(p, conv_w[l], conv_b[l][None, :], conv_ln_w[l][None, :], conv_ln_b[l][None, :],
                   col["glu"], col["g_b"], tt, tp // tt, lp_ // tt, ls // tt)

        wup_pad = jnp.stack([jnp.concatenate([rwkv_w_up[l, 0], zpad], 0), jnp.concatenate([zpad, rwkv_w_up[l, 1]], 0)])
        aup_pad = jnp.stack([jnp.concatenate([rwkv_a_up[l, 0], zpad], 0), jnp.concatenate([zpad, rwkv_a_up[l, 1]], 0)])
        kk, lw, kd, bd = _rwkv_prep(p, col["k_c"], col["xw"], col["xa"], rwkv_w0[l], wup_pad, rwkv_a0[l], aup_pad,
                                    rwkv_k_k[l][None, :], rwkv_k_a[l][None, :], e2, tm_rw)
        h0_p = jnp.zeros((bp, 2, n_hc // 2, 2 * N_C, 2 * N_C), F32)
        yf_p, yb_p, hfin_p = _rwkv_chunks(p, kk, lw, kd, bd, h0_p, 0, bp, lp_, col["r_c"], col["v_c"])
        ss.append(_wide_to_state(hfin_p))
        yf_s, yb_s, _ = _rwkv_chunks(p, kk, lw, kd, bd, _state_to_wide(state_rwkv[:, l]), tp, bs, ls,
                                     col["r_c"], col["v_c"])
        post_args = (col["r_c"], col["v_c"], col["g_c"], rwkv_r_k[l].reshape(1, c_c),
                     rwkv_gn_w[l][None, :], rwkv_gn_b[l][None, :], e2, tm_rw)
        bc = jnp.concatenate([_rwkv_post(yf_p, yb_p, p, kd, 0, *post_args),
                              _rwkv_post(yf_s, yb_s, p, kd, tp, *post_args)], axis=0)

        x = _out_proj(ba, bb, bc, p, w_br_a[l].astype(BF16), w_br_b[l].astype(BF16), w_br_c[l].astype(BF16),
                      w_out[l].astype(BF16), x, mod, final_norm_w[None, :], cond_of(tm_out), tm_out, tn_out,
                      final=(l == depth - 1))

        ks.append(p[:tp, col["k"]:col["k"] + w_a].reshape(bp, lp_, n_ha, 2, DH_A))
        vs.append(p[:tp, col["v"]:col["v"] + w_a].reshape(bp, lp_, n_ha, HEAD_A))

    y_prompt = x[:tp].reshape(bp, lp_, d)
    y_sample = x[tp:].reshape(bs, ls, d)
    return (y_prompt, y_sample, jnp.stack(ks, axis=1), jnp.stack(vs, axis=1), jnp.stack(ss, axis=1))
```

```python
import functools
import math

import jax
import jax.numpy as jnp
from jax import lax
from jax.experimental import pallas as pl
from jax.experimental.pallas import tpu as pltpu

F32 = jnp.float32
BF16 = jnp.bfloat16

HEAD_A = 128
DH_A = 64
CONV_W = 31
CONV_HALO = 16
N_C = 64
LORA = 64
GRID_W = 64
ROPE_BASE = 10000.0
EPS = 1e-6
LN_EPS = 1e-5
GN_EPS = 64e-5
Q_SCALE = DH_A ** -0.5 * math.log2(math.e)

LANES = 128
SUBLANES = 8
MIB = 1024 * 1024


def _cparams(semantics, vmem_mib=48):
    return pltpu.CompilerParams(dimension_semantics=semantics, vmem_limit_bytes=vmem_mib * MIB)


def _split_bf16(x):
    hi = x.astype(BF16)
    lo = (x - hi.astype(F32)).astype(BF16)
    return hi, lo


_NT = (((1,), (1,)), ((), ()))


def _mm(a, b):
    return jnp.dot(a, b, preferred_element_type=F32)


def _dot3(a, b):
    ah, al = _split_bf16(a)
    bh, bl = _split_bf16(b)
    return _mm(ah, bh) + _mm(ah, bl) + _mm(al, bh)


def _dot2(a, b_bf16):
    ah, al = _split_bf16(a)
    return _mm(ah, b_bf16) + _mm(al, b_bf16)


def _silu(x):
    return x * jax.nn.sigmoid(x)


def _head_sums(x, e2):
    parts = [_dot2(x[:, c * LANES:(c + 1) * LANES], e2) for c in range(x.shape[1] // LANES)]
    return jnp.concatenate(parts, axis=1)


def _ada_kernel(c_ref, w_ref, b_ref, o_ref):
    o_ref[...] = _dot3(_silu(c_ref[...]), w_ref[...]) + b_ref[...]


def _ada(cond8, w, b):
    d, n = w.shape
    tn = 512
    return pl.pallas_call(
        _ada_kernel,
        grid=(n // tn,),
        in_specs=[pl.BlockSpec((SUBLANES, d), lambda j: (0, 0)),
                  pl.BlockSpec((d, tn), lambda j: (0, j)),
                  pl.BlockSpec((1, tn), lambda j: (0, j))],
        out_specs=pl.BlockSpec((SUBLANES, tn), lambda j: (0, j)),
        out_shape=jax.ShapeDtypeStruct((SUBLANES, n), F32),
        compiler_params=_cparams(("arbitrary",)),
        name="ada_mod",
    )(cond8, w, b)


def _inproj_kernel(x_ref, mod_ref, nw_ref, w_ref, o_ref, h_scr):
    @pl.when(pl.program_id(1) == 0)
    def _():
        x = x_ref[...]
        y = x * lax.rsqrt(jnp.mean(x * x, axis=-1, keepdims=True) + EPS) * nw_ref[...]
        h = y * (1.0 + mod_ref[0, 1:2, :]) + mod_ref[0, 0:1, :]
        h_scr[...] = h.astype(BF16)

    o_ref[...] = _mm(h_scr[...], w_ref[...]).astype(o_ref.dtype)


def _inproj(x, mod, norm_w, w_bf16, cond_of_tile, tm, tn):
    t, d = x.shape
    n = w_bf16.shape[1]
    return pl.pallas_call(
        _inproj_kernel,
        grid=(t // tm, n // tn),
        in_specs=[pl.BlockSpec((tm, d), lambda i, j: (i, 0)),
                  pl.BlockSpec((1, 3, d), lambda i, j: (cond_of_tile(i), 0, 0)),
                  pl.BlockSpec((1, d), lambda i, j: (0, 0)),
                  pl.BlockSpec((d, tn), lambda i, j: (0, j))],
        out_specs=pl.BlockSpec((tm, tn), lambda i, j: (i, j)),
        out_shape=jax.ShapeDtypeStruct((t, n), BF16),
        scratch_shapes=[pltpu.VMEM((tm, d), BF16)],
        compiler_params=_cparams(("parallel", "arbitrary")),
        name="inproj",
    )(x, mod, norm_w, w_bf16)


def _rope_tab_kernel(inv_ref, c_ref, s1_ref, s2_ref, *, tr):
    t = lax.broadcasted_iota(jnp.int32, (tr, LANES), 0) + pl.program_id(0) * tr
    lane = lax.broadcasted_iota(jnp.int32, (tr, LANES), 1)
    row = (t // GRID_W).astype(F32)
    col = (t % GRID_W).astype(F32)
    pos = jnp.where((lane % DH_A) >= DH_A // 2, col, row)
    ang = pos * inv_ref[...]
    c = jnp.cos(ang)
    s = jnp.sin(ang)
    first = (lane % (DH_A // 2)) < DH_A // 4
    c_ref[...] = c
    s1_ref[...] = jnp.where(first, -s, 0.0)
    s2_ref[...] = jnp.where(first, 0.0, s)


def _rope_tables(ls):
    tr = min(ls, 512)
    quarter = DH_A // 4
    inv = ROPE_BASE ** (-jnp.arange(quarter, dtype=F32) / quarter)
    inv = jnp.tile(inv, LANES // quarter)[None, :]
    shp = jax.ShapeDtypeStruct((ls, LANES), F32)
    spec = pl.BlockSpec((tr, LANES), lambda i: (i, 0))
    return pl.pallas_call(
        functools.partial(_rope_tab_kernel, tr=tr),
        grid=(ls // tr,),
        in_specs=[pl.BlockSpec((1, LANES), lambda i: (0, 0))],
        out_specs=[spec, spec, spec],
        out_shape=[shp, shp, shp],
        compiler_params=_cparams(("arbitrary",)),
        name="rope_tables",
    )(inv)


def _rope_kernel(x_ref, c_ref, s1_ref, s2_ref, o_ref, *, n_q):
    x = x_ref[...].astype(F32)
    quarter = DH_A // 4
    y = (x * c_ref[...] + pltpu.roll(x, LANES - quarter, 1) * s1_ref[...]
         + pltpu.roll(x, quarter, 1) * s2_ref[...])
    scale = jnp.where(pl.program_id(1) < n_q, Q_SCALE, 1.0)
    o_ref[...] = (y * scale).astype(BF16)


def _rope_qk(p, tabs, row0, ts, ls, col0, n_heads):
    tr = min(ls, 512)
    tab_spec = pl.BlockSpec((tr, LANES), lambda i, j: (i % (ls // tr), 0))
    return pl.pallas_call(
        functools.partial(_rope_kernel, n_q=n_heads),
        grid=(ts // tr, 2 * n_heads),
        in_specs=[pl.BlockSpec((tr, LANES), lambda i, j: (row0 // tr + i, col0 // LANES + j)),
                  tab_spec, tab_spec, tab_spec],
        out_specs=pl.BlockSpec((tr, LANES), lambda i, j: (i, j)),
        out_shape=jax.ShapeDtypeStruct((ts, 2 * n_heads * LANES), BF16),
        compiler_params=_cparams(("parallel", "arbitrary")),
        name="rope_qk",
    )(p, *tabs)


ATTN_SUB = 128


def _attn_kernel(lam_ref, sw_ref, q_ref, k_ref, v_ref, g_ref, o_ref, *, lam_init, q_scale):
    lp = lam_ref[...]
    lam = (jnp.exp(jnp.sum(lp[0:1, :] * lp[1:2, :], axis=1, keepdims=True))
           - jnp.exp(jnp.sum(lp[2:3, :] * lp[3:4, :], axis=1, keepdims=True)) + lam_init)
    q = q_ref[...]
    if q_scale != 1.0:
        q = q.astype(F32) * q_scale
    q = q.astype(BF16)
    k = k_ref[...].astype(BF16)
    v = v_ref[...].astype(BF16)
    v_aug = jnp.concatenate([v, jnp.ones_like(v)], axis=1)
    lane = lax.broadcasted_iota(jnp.int32, q.shape, 1)
    zero = jnp.zeros_like(q)
    q1 = jnp.where(lane < DH_A, q, zero)
    q2 = jnp.where(lane < DH_A, zero, q)
    rs = min(ATTN_SUB, q.shape[0])
    n_sub = q.shape[0] // rs

    def logits(i):
        lhs = jnp.concatenate([q1[i * rs:(i + 1) * rs], q2[i * rs:(i + 1) * rs]], axis=0)
        return lax.dot_general(lhs, k, _NT, preferred_element_type=F32)

    outs = []
    s = logits(0)
    for i in range(n_sub):
        s_next = logits(i + 1) if i + 1 < n_sub else None
        e = jnp.exp2(s - jnp.max(s, axis=-1, keepdims=True)).astype(BF16)
        oa = _mm(e, v_aug)
        on = oa[:, :HEAD_A] * (1.0 / oa[:, HEAD_A:HEAD_A + 1])
        outs.append(on[:rs] - lam * on[rs:])
        s = s_next
    o = jnp.concatenate(outs, axis=0)
    o = o * lax.rsqrt(jnp.mean(o * o, axis=-1, keepdims=True) + EPS) * sw_ref[...] * (1.0 - lam_init)
    o_ref[...] = (o * _silu(g_ref[...].astype(F32))).astype(BF16)


def _attention(lam_p, subln_w, q_arr, q_map, k_arr, k_map, v_arr, v_map, g_arr, g_map,
               n_b, n_heads, lq, lk, tq, lam_init, q_scale):
    nq = lq // tq
    return pl.pallas_call(
        functools.partial(_attn_kernel, lam_init=lam_init, q_scale=q_scale),
        grid=(n_b, n_heads, nq),
        in_specs=[pl.BlockSpec(lam_p.shape, lambda b, h, i: (0, 0)),
                  pl.BlockSpec((1, HEAD_A), lambda b, h, i: (0, 0)),
                  pl.BlockSpec((tq, HEAD_A), q_map),
                  pl.BlockSpec((lk, HEAD_A), k_map),
                  pl.BlockSpec((lk, HEAD_A), v_map),
                  pl.BlockSpec((tq, HEAD_A), g_map)],
        out_specs=pl.BlockSpec((tq, HEAD_A), lambda b, h, i: (b * nq + i, h)),
        out_shape=jax.ShapeDtypeStruct((n_b * lq, n_heads * HEAD_A), BF16),
        compiler_params=_cparams(("parallel", "parallel", "arbitrary")),
        name="diff_attn",
    )(lam_p, subln_w, q_arr, k_arr, v_arr, g_arr)


def _conv_kernel(u_ref, up_ref, un_ref, g_ref, cw_ref, cb_ref, lw_ref, lb_ref, o_ref, zp_scr, acc_scr, zs_scr,
                 *, tt, n_prompt_blocks, blocks_per_prompt_seq, blocks_per_sample_seq):
    cb_ = u_ref.shape[1] // 2
    blk = pl.program_id(0)
    in_prompt = blk < n_prompt_blocks
    pos = jnp.where(in_prompt, blk % blocks_per_prompt_seq, (blk - n_prompt_blocks) % blocks_per_sample_seq)
    per_seq = jnp.where(in_prompt, blocks_per_prompt_seq, blocks_per_sample_seq)
    keep_prev = (pos > 0).astype(F32)
    keep_next = (pos < per_seq - 1).astype(F32)

    def glu(ref):
        u = ref[...].astype(F32)
        return u[:, :cb_] * jax.nn.sigmoid(u[:, cb_:])

    h = CONV_HALO
    zp_scr[pl.ds(0, h), :] = glu(up_ref) * keep_prev
    zp_scr[pl.ds(h, tt), :] = glu(u_ref)
    zp_scr[pl.ds(h + tt, h), :] = glu(un_ref) * keep_next

    rc = 64
    off = h - CONV_W // 2
    n_sh = zs_scr.shape[1]

    def lane_chunk(c, carry):
        ls = pl.ds(pl.multiple_of(c * LANES, LANES), LANES)
        w = cw_ref[:, ls]
        for s in range(SUBLANES):
            zs_scr[s] = zp_scr[pl.ds(s, n_sh), ls]
        for r0 in range(0, tt, rc):
            acc = jnp.zeros((rc, LANES), F32)
            for tau in range(CONV_W):
                o = tau + off
                acc = acc + zs_scr[o % SUBLANES, pl.ds(r0 + o - o % SUBLANES, rc), :] * w[tau:tau + 1, :]
            acc_scr[pl.ds(r0, rc), ls] = acc
        return carry

    lax.fori_loop(0, cb_ // LANES, lane_chunk, 0)

    z = acc_scr[...] + cb_ref[...]
    mu = jnp.mean(z, axis=-1, keepdims=True)
    d = z - mu
    var = jnp.mean(d * d, axis=-1, keepdims=True)
    zn = d * lax.rsqrt(var + LN_EPS) * lw_ref[...] + lb_ref[...]
    o_ref[...] = (_silu(zn) * _silu(g_ref[...].astype(F32))).astype(BF16)


def _conv(p, conv_w, conv_b, ln_w, ln_b, glu_col0, gb_col0, tt, n_prompt_blocks,
          blocks_per_prompt_seq, blocks_per_sample_seq):
    t = p.shape[0]
    cb_ = conv_w.shape[1]
    nblk = t // tt
    hb = tt // CONV_HALO
    n_hblk = t // CONV_HALO
    vec = pl.BlockSpec((1, cb_), lambda i: (0, 0))
    return pl.pallas_call(
        functools.partial(_conv_kernel, tt=tt, n_prompt_blocks=n_prompt_blocks,
                          blocks_per_prompt_seq=blocks_per_prompt_seq,
                          blocks_per_sample_seq=blocks_per_sample_seq),
        grid=(nblk,),
        in_specs=[pl.BlockSpec((tt, 2 * cb_), lambda i: (i, glu_col0 // (2 * cb_))),
                  pl.BlockSpec((CONV_HALO, 2 * cb_), lambda i: (jnp.maximum(i * hb - 1, 0), glu_col0 // (2 * cb_))),
                  pl.BlockSpec((CONV_HALO, 2 * cb_),
                               lambda i: (jnp.minimum((i + 1) * hb, n_hblk - 1), glu_col0 // (2 * cb_))),
                  pl.BlockSpec((tt, cb_), lambda i: (i, gb_col0 // cb_)),
                  pl.BlockSpec(conv_w.shape, lambda i: (0, 0)),
                  vec, vec, vec],
        out_specs=pl.BlockSpec((tt, cb_), lambda i: (i, 0)),
        out_shape=jax.ShapeDtypeStruct((t, cb_), BF16),
        scratch_shapes=[pltpu.VMEM((tt + 2 * CONV_HALO, cb_), F32), pltpu.VMEM((tt, cb_), F32),
                        pltpu.VMEM((SUBLANES, tt + 2 * CONV_HALO - SUBLANES, LANES), F32)],
        compiler_params=_cparams(("parallel",)),
        name="conformer_conv",
    )(p, p, p, p, conv_w, conv_b, ln_w, ln_b)


def _rwkv_prep_kernel(k_ref, xw_ref, xa_ref, w0_ref, wup_ref, a0_ref, aup_ref, kk_ref, ka_ref, e2_ref,
                      okk_ref, olw_ref, okd_ref, obd_ref):
    k = k_ref[...].astype(F32)
    e2 = e2_ref[...]
    kkr = k * kk_ref[...]
    kk = kkr / jnp.maximum(jnp.sqrt(_head_sums(kkr * kkr, e2)), 1e-12)
    okk_ref[...] = kk
    txw = jnp.tanh(xw_ref[...].astype(F32))
    xa = xa_ref[...].astype(F32)
    for d in range(2):
        w_log = -jax.nn.softplus(-(w0_ref[d:d + 1, :] + _dot3(txw, wup_ref[d]))) - 0.5
        olw_ref[d] = -jnp.exp(w_log)
        a = jax.nn.sigmoid(a0_ref[d:d + 1, :] + _dot3(xa, aup_ref[d]))
        okd_ref[d] = k * (1.0 + (a - 1.0) * ka_ref[...])
        obd_ref[d] = kk * a


def _rwkv_prep(p, kc_col0, xw_col0, xa_col0, w0, wup_pad, a0, aup_pad, k_k, k_a, e2, tm):
    t = p.shape[0]
    cc = w0.shape[1]
    vec = pl.BlockSpec((1, cc), lambda i: (0, 0))
    vec2 = pl.BlockSpec((2, cc), lambda i: (0, 0))
    mat = pl.BlockSpec((2, LANES, cc), lambda i: (0, 0, 0))
    one = jax.ShapeDtypeStruct((t, cc), F32)
    two = jax.ShapeDtypeStruct((2, t, cc), F32)
    two_spec = pl.BlockSpec((2, tm, cc), lambda i: (0, i, 0))
    return pl.pallas_call(
        _rwkv_prep_kernel,
        grid=(t // tm,),
        in_specs=[pl.BlockSpec((tm, cc), lambda i: (i, kc_col0 // cc)),
                  pl.BlockSpec((tm, LANES), lambda i: (i, xw_col0 // LANES)),
                  pl.BlockSpec((tm, LANES), lambda i: (i, xa_col0 // LANES)),
                  vec2, mat, vec2, mat, vec, vec,
                  pl.BlockSpec((LANES, LANES), lambda i: (0, 0))],
        out_specs=[pl.BlockSpec((tm, cc), lambda i: (i, 0)), two_spec, two_spec, two_spec],
        out_shape=[one, two, two, two],
        compiler_params=_cparams(("parallel",)),
        name="rwkv_prep",
    )(p, p, p, w0, wup_pad, a0, aup_pad, k_k, k_a, e2)


CHUNK = 64
PAIRS_PER_BODY = 4


def _mmb(a, b):
    return _mm(a.astype(BF16), b.astype(BF16))


def _chunk_unit(r, v, kk, lw, c, kd, bd, h, reverse, lane_lo, eye, strict, incl):
    ctot = c[0:1, :] if reverse else c[CHUNK - 1:CHUNK, :]
    g_inv = jnp.exp(-c)
    g_rem = jnp.exp(ctot - c)
    rt = r * jnp.exp(c)

    def stack(x):
        z = jnp.zeros_like(x)
        return jnp.concatenate([jnp.where(lane_lo, x, z), jnp.where(lane_lo, z, x)], axis=0)

    bt_s = stack(-(kk * jnp.exp(c - lw))).astype(BF16)
    v_s = stack(v).astype(BF16)
    lhs = jnp.concatenate([bt_s, stack(rt).astype(BF16)], axis=0)
    rhs = jnp.concatenate([stack(bd * g_inv), stack(kd * g_inv)], axis=0).astype(BF16)
    lhs_t = jnp.concatenate([stack(bd * g_rem).T, stack(kd * g_rem).T], axis=1).astype(BF16)
    nn = lax.dot_general(lhs, rhs, _NT, preferred_element_type=F32)
    yield None
    w2 = 2 * CHUNK
    a_ab = jnp.where(strict, nn[:w2, :w2], 0.0)
    a_bk = jnp.where(strict, nn[:w2, w2:], 0.0)
    a_r = jnp.concatenate([jnp.where(incl, nn[w2:, :w2], 0.0), jnp.where(incl, nn[w2:, w2:], 0.0)], axis=1)
    p1 = _mmb(a_bk, v_s)
    n = a_ab
    t = eye + n
    for _ in range(int(math.log2(CHUNK)) - 1):
        n = _mmb(n, n)
        yield None
        t = t + _mmb(t, n)
    yield None
    wx = _mmb(t, jnp.concatenate([bt_s, p1.astype(BF16)], axis=1))
    yield None
    rhs_big = jnp.concatenate(
        [wx.astype(BF16), jnp.concatenate([jnp.zeros((w2, w2), BF16), v_s], axis=1)], axis=0)
    qy = _mmb(a_r, rhs_big)
    mz = _mm(lhs_t, rhs_big)
    yield None
    qhat = rt + qy[:CHUNK, :w2] + qy[CHUNK:, :w2]
    yhat = qy[:CHUNK, w2:] + qy[CHUNK:, w2:]
    m = jnp.where(eye > 0.0, jnp.exp(ctot), 0.0) + mz[:, :w2]
    res = _dot3(jnp.concatenate([m, qhat], axis=0), h)
    yield res[:w2] + mz[:, w2:], res[w2:] + yhat


def _run_lockstep(gens):
    last = [None] * len(gens)
    live = list(range(len(gens)))
    while live:
        nxt = []
        for i in live:
            try:
                last[i] = next(gens[i])
                nxt.append(i)
            except StopIteration:
                pass
        live = nxt
    return last


def _rwkv_chunk_kernel(rf_ref, vf_ref, kkf_ref, lwf_ref, kdf_ref, bdf_ref,
                       rb_ref, vb_ref, kkb_ref, lwb_ref, kdb_ref, bdb_ref, h0_ref,
                       yf_ref, yb_ref, hfin_ref, h_scr, c_scr):
    @pl.when(pl.program_id(1) == 0)
    def _():
        h_scr[...] = h0_ref[0]

    ti = lax.broadcasted_iota(jnp.int32, (CHUNK, CHUNK), 0)
    si = lax.broadcasted_iota(jnp.int32, (CHUNK, CHUNK), 1)

    def cumsum(tri, x):
        hi = x.astype(BF16)
        r1 = x - hi.astype(F32)
        mid = r1.astype(BF16)
        lo = (r1 - mid.astype(F32)).astype(BF16)
        return _mm(tri, hi) + _mm(tri, mid) + _mm(tri, lo)

    c_scr[0] = cumsum((si <= ti).astype(BF16), lwf_ref[...])
    c_scr[1] = cumsum((si >= ti).astype(BF16), lwb_ref[...])

    w2 = 2 * CHUNK
    row = lax.broadcasted_iota(jnp.int32, (w2, w2), 0)
    col = lax.broadcasted_iota(jnp.int32, (w2, w2), 1)
    tpos, spos = row % CHUNK, col % CHUNK
    eye = (row == col).astype(F32)
    lane_lo = lax.broadcasted_iota(jnp.int32, (CHUNK, w2), 1) < N_C
    dirs = ((rf_ref, vf_ref, kkf_ref, lwf_ref, kdf_ref, bdf_ref, yf_ref, spos < tpos, spos <= tpos),
            (rb_ref, vb_ref, kkb_ref, lwb_ref, kdb_ref, bdb_ref, yb_ref, spos > tpos, spos >= tpos))

    def pair_group(pg, carry):
        units = []
        for pi in range(PAIRS_PER_BODY):
            p = pg * PAIRS_PER_BODY + pi
            ls = pl.ds(pl.multiple_of(p * w2, w2), w2)
            for d, (r_ref, v_ref, kk_ref, lw_ref, kd_ref, bd_ref, y_ref, strict, incl) in enumerate(dirs):
                args = (r_ref[:, ls].astype(F32), v_ref[:, ls].astype(F32), kk_ref[:, ls], lw_ref[:, ls], c_scr[d, :, ls],
                        kd_ref[:, ls], bd_ref[:, ls], h_scr[d, p])
                units.append((d, p, ls, y_ref, args, strict, incl))
        results = _run_lockstep([_chunk_unit(*args, d == 1, lane_lo, eye, strict, incl)
                                 for d, p, ls, y_ref, args, strict, incl in units])
        for (d, p, ls, y_ref, _, _, _), (h_new, y) in zip(units, results):
            h_scr[d, p] = h_new
            y_ref[:, ls] = y
        return carry

    lax.fori_loop(0, rf_ref.shape[1] // (w2 * PAIRS_PER_BODY), pair_group, 0)

    @pl.when(pl.program_id(1) == pl.num_programs(1) - 1)
    def _():
        hfin_ref[0] = h_scr[...]


def _rwkv_chunks(p, kk, lw, kd, bd, h0, row0, nseq, l, rc_col0, vc_col0):
    cc = kk.shape[1]
    nc = l // CHUNK
    rb0 = row0 // CHUNK
    npair = cc // (2 * N_C)

    def fwd(s, g):
        return rb0 + s * nc + g

    def bwd(s, g):
        return rb0 + s * nc + nc - 1 - g

    def specs(rowf, d):
        return [pl.BlockSpec((CHUNK, cc), lambda s, g: (rowf(s, g), rc_col0 // cc)),
                pl.BlockSpec((CHUNK, cc), lambda s, g: (rowf(s, g), vc_col0 // cc)),
                pl.BlockSpec((CHUNK, cc), lambda s, g: (rowf(s, g), 0)),
                pl.BlockSpec((None, CHUNK, cc), lambda s, g: (d, rowf(s, g), 0)),
                pl.BlockSpec((None, CHUNK, cc), lambda s, g: (d, rowf(s, g), 0)),
                pl.BlockSpec((None, CHUNK, cc), lambda s, g: (d, rowf(s, g), 0))]

    hspec = pl.BlockSpec((1, 2, npair, 2 * N_C, 2 * N_C), lambda s, g: (s, 0, 0, 0, 0))
    yshape = jax.ShapeDtypeStruct((nseq * l, cc), F32)
    return pl.pallas_call(
        _rwkv_chunk_kernel,
        grid=(nseq, nc),
        in_specs=specs(fwd, 0) + specs(bwd, 1) + [hspec],
        out_specs=[pl.BlockSpec((CHUNK, cc), lambda s, g: (s * nc + g, 0)),
                   pl.BlockSpec((CHUNK, cc), lambda s, g: (s * nc + nc - 1 - g, 0)),
                   hspec],
        out_shape=[yshape, yshape, jax.ShapeDtypeStruct(h0.shape, F32)],
        scratch_shapes=[pltpu.VMEM((2, npair, 2 * N_C, 2 * N_C), F32), pltpu.VMEM((2, CHUNK, cc), F32)],
        compiler_params=_cparams(("parallel", "arbitrary")),
        name="rwkv_chunks",
    )(p, p, kk, lw, kd, bd, p, p, kk, lw, kd, bd, h0)


def _state_to_wide(s):
    nseq, _, nh, n, _ = s.shape
    ht = s.transpose(0, 1, 2, 4, 3).reshape(nseq, 2, nh // 2, 2, n, n)
    z = jnp.zeros_like(ht[:, :, :, 0])
    top = jnp.concatenate([ht[:, :, :, 0], z], axis=-1)
    bot = jnp.concatenate([z, ht[:, :, :, 1]], axis=-1)
    return jnp.concatenate([top, bot], axis=-2)


def _wide_to_state(h):
    nseq, _, npair, w2, _ = h.shape
    n = w2 // 2
    blocks = jnp.stack([h[..., :n, :n], h[..., n:, n:]], axis=3)
    return blocks.reshape(nseq, 2, 2 * npair, n, n).transpose(0, 1, 2, 4, 3)


def _rwkv_post_kernel(yf_ref, yb_ref, r_ref, v_ref, kd_ref, g_ref, rk_ref, gw_ref, gb_ref, e2_ref, o_ref):
    e2 = e2_ref[...]
    y = yf_ref[...] + yb_ref[...]
    inv_n = 1.0 / N_C
    d = y - _head_sums(y, e2) * inv_n
    var = _head_sums(d * d, e2) * inv_n
    yn = d * lax.rsqrt(var + GN_EPS) * gw_ref[...] + gb_ref[...]
    bonus = (_head_sums(r_ref[...].astype(F32) * (kd_ref[0] + kd_ref[1]) * rk_ref[...], e2)
             * v_ref[...].astype(F32))
    o_ref[...] = ((yn + bonus) * _silu(g_ref[...].astype(F32))).astype(BF16)


def _rwkv_post(yf, yb, p, kd, row0, rc_col0, vc_col0, gc_col0, r_k, gn_w, gn_b, e2, tm):
    t, cc = yf.shape
    rb0 = row0 // tm
    vec = pl.BlockSpec((1, cc), lambda i: (0, 0))
    own = pl.BlockSpec((tm, cc), lambda i: (i, 0))
    return pl.pallas_call(
        _rwkv_post_kernel,
        grid=(t // tm,),
        in_specs=[own, own,
                  pl.BlockSpec((tm, cc), lambda i: (rb0 + i, rc_col0 // cc)),
                  pl.BlockSpec((tm, cc), lambda i: (rb0 + i, vc_col0 // cc)),
                  pl.BlockSpec((2, tm, cc), lambda i: (0, rb0 + i, 0)),
                  pl.BlockSpec((tm, cc), lambda i: (rb0 + i, gc_col0 // cc)),
                  vec, vec, vec,
                  pl.BlockSpec((LANES, LANES), lambda i: (0, 0))],
        out_specs=own,
        out_shape=jax.ShapeDtypeStruct((t, cc), BF16),
        compiler_params=_cparams(("parallel",)),
        name="rwkv_post",
    )(yf, yb, p, p, kd, p, r_k, gn_w, gn_b, e2)


def _merge_kernel(ba_ref, bb_ref, bc_ref, m0_ref, m1_ref, m2_ref, wa_ref, wb_ref, wc_ref, o_ref):
    o_ref[...] = (jax.nn.sigmoid(m0_ref[...].astype(F32)) * _mm(ba_ref[...], wa_ref[...])
                  + jax.nn.sigmoid(m1_ref[...].astype(F32)) * _mm(bb_ref[...], wb_ref[...])
                  + jax.nn.sigmoid(m2_ref[...].astype(F32)) * _mm(bc_ref[...], wc_ref[...])).astype(BF16)


def _merge(ba, bb, bc, p, wa, wb, wc, tm, tn):
    t, wbr = ba.shape
    d = wa.shape[1]
    nj = d // tn
    br = pl.BlockSpec((tm, wbr), lambda i, j: (i, 0))
    wspec = pl.BlockSpec((wbr, tn), lambda i, j: (0, j))
    return pl.pallas_call(
        _merge_kernel,
        grid=(t // tm, nj),
        in_specs=[br, br, br,
                  pl.BlockSpec((tm, tn), lambda i, j: (i, j)),
                  pl.BlockSpec((tm, tn), lambda i, j: (i, nj + j)),
                  pl.BlockSpec((tm, tn), lambda i, j: (i, 2 * nj + j)),
                  wspec, wspec, wspec],
        out_specs=pl.BlockSpec((tm, tn), lambda i, j: (i, j)),
        out_shape=jax.ShapeDtypeStruct((t, d), BF16),
        compiler_params=_cparams(("parallel", "arbitrary")),
        name="branch_merge",
    )(ba, bb, bc, p, p, p, wa, wb, wc)


def _resid_kernel(m_ref, wo_ref, x_ref, mod_ref, fw_ref, o_ref, *, final):
    x = x_ref[...] + mod_ref[0, 2:3, :] * _mm(m_ref[...], wo_ref[...])
    if final:
        x = x * lax.rsqrt(jnp.mean(x * x, axis=-1, keepdims=True) + EPS) * fw_ref[...]
    o_ref[...] = x


def _out_proj(merged, wo, x, mod, final_w, cond_of_tile, tm, final):
    t, d = x.shape
    row = pl.BlockSpec((tm, d), lambda i: (i, 0))
    return pl.pallas_call(
        functools.partial(_resid_kernel, final=final),
        grid=(t // tm,),
        in_specs=[row,
                  pl.BlockSpec((d, d), lambda i: (0, 0)),
                  row,
                  pl.BlockSpec((1, 3, d), lambda i: (cond_of_tile(i), 0, 0)),
                  pl.BlockSpec((1, d), lambda i: (0, 0))],
        out_specs=row,
        out_shape=jax.ShapeDtypeStruct((t, d), F32),
        compiler_params=_cparams(("parallel",)),
        name="out_proj",
    )(merged, wo, x, mod, final_w)


def _largest_tile(n, cap):
    t = cap
    while n % t:
        t //= 2
    return t


def kernel(x_prompt, x_sample, cache_k, cache_v, state_rwkv, c, c_ctx, w_ada, b_ada, norm_w, w_in,
           lambda_q1, lambda_k1, lambda_q2, lambda_k2, subln_w, conv_w, conv_b, conv_ln_w, conv_ln_b,
           rwkv_w0, rwkv_w_up, rwkv_a0, rwkv_a_up, rwkv_k_k, rwkv_k_a, rwkv_r_k, rwkv_gn_w, rwkv_gn_b,
           w_br_a, w_br_b, w_br_c, w_out, final_norm_w):
    bp, lp_, d = x_prompt.shape
    bs, ls, _ = x_sample.shape
    depth = w_ada.shape[0]
    past = cache_k.shape[2]
    tp, ts = bp * lp_, bs * ls
    n_ha = d // 256
    w_a = n_ha * HEAD_A
    c_b = d // 2
    n_hc = d // (2 * N_C)
    c_c = n_hc * N_C
    n_mg = 3 * d
    assert n_ha * 2 * DH_A == w_a and w_a == c_b == c_c
    assert bs + 1 <= SUBLANES

    col = {}
    off = 0
    for name, width in (("mg", n_mg), ("q", w_a), ("k", w_a), ("v", w_a), ("g_a", w_a), ("glu", 2 * c_b),
                        ("g_b", c_b), ("r_c", c_c), ("k_c", c_c), ("v_c", c_c), ("g_c", c_c),
                        ("xw", 2 * LORA), ("xa", 2 * LORA)):
        col[name] = off
        off += width
    in_cols = off
    n_lead = in_cols - n_mg

    tm_in = _largest_tile(math.gcd(tp, ls), 1024)
    tm_out = _largest_tile(math.gcd(tp, ls), 512)
    tn_in = 768 if in_cols % 768 == 0 else 256
    tm_mg = _largest_tile(tp + ts, 1024)
    tn_mg = 512
    tt = _largest_tile(math.gcd(lp_, ls), 256)
    tm_rw = _largest_tile(math.gcd(tp, ts), 256)

    def cond_of(tm):
        npt = tp // tm
        return lambda i: jnp.where(i < npt, 0, 1 + ((i - npt) * tm) // ls)

    x = jnp.concatenate([x_prompt.reshape(tp, d), x_sample.reshape(ts, d)], axis=0)
    cond8 = jnp.zeros((SUBLANES, d), F32).at[0].set(c_ctx).at[1:1 + bs].set(c)
    rope_tabs = _rope_tables(ls)
    blk = jnp.arange(LANES) // N_C
    e2 = (blk[:, None] == blk[None, :]).astype(BF16)
    zpad = jnp.zeros((LORA, c_c), F32)

    ks, vs, ss = [], [], []
    for l in range(depth):
        lam_init = 0.8 - 0.6 * math.exp(-0.3 * l)
        mod = _ada(cond8, w_ada[l], b_ada[l][None, :]).reshape(SUBLANES, 3, d)
        w_perm = jnp.concatenate([w_in[l][:, n_lead:], w_in[l][:, :n_lead]], axis=1).astype(BF16)
        p = _inproj(x, mod, norm_w[l][None, :], w_perm, cond_of(tm_in), tm_in, tn_in)

        lam_p = jnp.stack([lambda_q1[l], lambda_k1[l], lambda_q2[l], lambda_k2[l]], axis=0)
        sw = subln_w[l][None, :]
        cq, ck, cv, cg = (col[n] // HEAD_A for n in ("q", "k", "v", "g_a"))
        tq_p = _largest_tile(lp_, 256)
        npq = lp_ // tq_p
        ba_p = _attention(
            lam_p, sw,
            p, lambda b, h, i: (b * npq + i, cq + h),
            p, lambda b, h, i: (b, ck + h),
            p, lambda b, h, i: (b, cv + h),
            p, lambda b, h, i: (b * npq + i, cg + h),
            bp, n_ha, lp_, lp_, tq_p, lam_init, Q_SCALE)
        qk_rot = _rope_qk(p, rope_tabs, tp, ts, ls, col["q"], n_ha)
        k_all = jnp.concatenate([cache_k[:, l].reshape(bs, past, w_a).astype(BF16),
                                 qk_rot[:, w_a:].reshape(bs, ls, w_a)], axis=1).reshape(bs * (past + ls), w_a)
        v_all = jnp.concatenate([cache_v[:, l].reshape(bs, past, w_a).astype(BF16),
                                 p[tp:, col["v"]:col["v"] + w_a].reshape(bs, ls, w_a)],
                                axis=1).reshape(bs * (past + ls), w_a)
        tq_s = _largest_tile(ls, 512)
        nsq = ls // tq_s
        ba_s = _attention(
            lam_p, sw,
            qk_rot, lambda b, h, i: (b * nsq + i, h),
            k_all, lambda b, h, i: (b, h),
            v_all, lambda b, h, i: (b, h),
            p, lambda b, h, i: (tp // tq_s + b * nsq + i, cg + h),
            bs, n_ha, ls, past + ls, tq_s, lam_init, 1.0)
        ba = jnp.concatenate([ba_p, ba_s], axis=0)

        bb = _conv(p, conv_w[l], conv_b[l][None, :], conv_ln_w[l][None, :], conv_ln_b[l][None, :],
                   col["glu"], col["g_b"], tt, tp // tt, lp_ // tt, ls // tt)

        wup_pad = jnp.stack([jnp.concatenate([rwkv_w_up[l, 0], zpad], 0), jnp.concatenate([zpad, rwkv_w_up[l, 1]], 0)])
        aup_pad = jnp.stack([jnp.concatenate([rwkv_a_up[l, 0], zpad], 0), jnp.concatenate([zpad, rwkv_a_up[l, 1]], 0)])
        kk, lw, kd, bd = _rwkv_prep(p, col["k_c"], col["xw"], col["xa"], rwkv_w0[l], wup_pad, rwkv_a0[l], aup_pad,
                                    rwkv_k_k[l][None, :], rwkv_k_a[l][None, :], e2, tm_rw)
        h0_p = jnp.zeros((bp, 2, n_hc // 2, 2 * N_C, 2 * N_C), F32)
        yf_p, yb_p, hfin_p = _rwkv_chunks(p, kk, lw, kd, bd, h0_p, 0, bp, lp_, col["r_c"], col["v_c"])
        ss.append(_wide_to_state(hfin_p))
        yf_s, yb_s, _ = _rwkv_chunks(p, kk, lw, kd, bd, _state_to_wide(state_rwkv[:, l]), tp, bs, ls,
                                     col["r_c"], col["v_c"])
        post_args = (col["r_c"], col["v_c"], col["g_c"], rwkv_r_k[l].reshape(1, c_c),
                     rwkv_gn_w[l][None, :], rwkv_gn_b[l][None, :], e2, tm_rw)
        bc = jnp.concatenate([_rwkv_post(yf_p, yb_p, p, kd, 0, *post_args),
                              _rwkv_post(yf_s, yb_s, p, kd, tp, *post_args)], axis=0)

        merged = _merge(ba, bb, bc, p, w_br_a[l].astype(BF16), w_br_b[l].astype(BF16), w_br_c[l].astype(BF16),
                        tm_mg, tn_mg)
        x = _out_proj(merged, w_out[l].astype(BF16), x, mod, final_norm_w[None, :], cond_of(tm_out), tm_out,
                      final=(l == depth - 1))

        ks.append(p[:tp, col["k"]:col["k"] + w_a].astype(F32).reshape(bp, lp_, n_ha, 2, DH_A))
        vs.append(p[:tp, col["v"]:col["v"] + w_a].astype(F32).reshape(bp, lp_, n_ha, HEAD_A))

    y_prompt = x[:tp].reshape(bp, lp_, d)
    y_sample = x[tp:].reshape(bs, ls, d)
    return (y_prompt, y_sample, jnp.stack(ks, axis=1), jnp.stack(vs, axis=1), jnp.stack(ss, axis=1))
```

```python
import functools
import math

import jax
import jax.numpy as jnp
from jax import lax
from jax.experimental import pallas as pl
from jax.experimental.pallas import tpu as pltpu

F32 = jnp.float32
BF16 = jnp.bfloat16

HEAD_A = 128
DH_A = 64
CONV_W = 31
CONV_HALO = 16
N_C = 64
LORA = 64
GRID_W = 64
ROPE_BASE = 10000.0
EPS = 1e-6
LN_EPS = 1e-5
GN_EPS = 64e-5
Q_SCALE = DH_A ** -0.5 * math.log2(math.e)

LANES = 128
SUBLANES = 8
MIB = 1024 * 1024


def _cparams(semantics, vmem_mib=48):
    return pltpu.CompilerParams(dimension_semantics=semantics, vmem_limit_bytes=vmem_mib * MIB)


def _split_bf16(x):
    hi = x.astype(BF16)
    lo = (x - hi.astype(F32)).astype(BF16)
    return hi, lo


_NT = (((1,), (1,)), ((), ()))


def _mm(a, b):
    return jnp.dot(a, b, preferred_element_type=F32)


def _dot3(a, b):
    ah, al = _split_bf16(a)
    bh, bl = _split_bf16(b)
    return _mm(ah, bh) + _mm(ah, bl) + _mm(al, bh)


def _dot2(a, b_bf16):
    ah, al = _split_bf16(a)
    return _mm(ah, b_bf16) + _mm(al, b_bf16)


def _silu(x):
    return x * jax.nn.sigmoid(x)


def _head_sums(x, e2):
    parts = [_dot2(x[:, c * LANES:(c + 1) * LANES], e2) for c in range(x.shape[1] // LANES)]
    return jnp.concatenate(parts, axis=1)


def _ada_kernel(c_ref, w_ref, b_ref, o_ref):
    o_ref[...] = _dot3(_silu(c_ref[...]), w_ref[...]) + b_ref[...]


def _ada(cond8, w, b):
    d, n = w.shape
    tn = 512
    return pl.pallas_call(
        _ada_kernel,
        grid=(n // tn,),
        in_specs=[pl.BlockSpec((SUBLANES, d), lambda j: (0, 0)),
                  pl.BlockSpec((d, tn), lambda j: (0, j)),
                  pl.BlockSpec((1, tn), lambda j: (0, j))],
        out_specs=pl.BlockSpec((SUBLANES, tn), lambda j: (0, j)),
        out_shape=jax.ShapeDtypeStruct((SUBLANES, n), F32),
        compiler_params=_cparams(("arbitrary",)),
        name="ada_mod",
    )(cond8, w, b)


def _inproj_kernel(x_ref, mod_ref, nw_ref, w_ref, o_ref, h_scr):
    @pl.when(pl.program_id(1) == 0)
    def _():
        x = x_ref[...]
        y = x * lax.rsqrt(jnp.mean(x * x, axis=-1, keepdims=True) + EPS) * nw_ref[...]
        h = y * (1.0 + mod_ref[0, 1:2, :]) + mod_ref[0, 0:1, :]
        h_scr[...] = h.astype(BF16)

    o_ref[...] = _mm(h_scr[...], w_ref[...]).astype(o_ref.dtype)


def _inproj(x, mod, norm_w, w_bf16, cond_of_tile, tm, tn):
    t, d = x.shape
    n = w_bf16.shape[1]
    return pl.pallas_call(
        _inproj_kernel,
        grid=(t // tm, n // tn),
        in_specs=[pl.BlockSpec((tm, d), lambda i, j: (i, 0)),
                  pl.BlockSpec((1, 3, d), lambda i, j: (cond_of_tile(i), 0, 0)),
                  pl.BlockSpec((1, d), lambda i, j: (0, 0)),
                  pl.BlockSpec((d, tn), lambda i, j: (0, j))],
        out_specs=pl.BlockSpec((tm, tn), lambda i, j: (i, j)),
        out_shape=jax.ShapeDtypeStruct((t, n), BF16),
        scratch_shapes=[pltpu.VMEM((tm, d), BF16)],
        compiler_params=_cparams(("parallel", "arbitrary")),
        name="inproj",
    )(x, mod, norm_w, w_bf16)


def _rope_tab_kernel(inv_ref, c_ref, s1_ref, s2_ref, *, tr):
    t = lax.broadcasted_iota(jnp.int32, (tr, LANES), 0) + pl.program_id(0) * tr
    lane = lax.broadcasted_iota(jnp.int32, (tr, LANES), 1)
    row = (t // GRID_W).astype(F32)
    col = (t % GRID_W).astype(F32)
    pos = jnp.where((lane % DH_A) >= DH_A // 2, col, row)
    ang = pos * inv_ref[...]
    c = jnp.cos(ang)
    s = jnp.sin(ang)
    first = (lane % (DH_A // 2)) < DH_A // 4
    c_ref[...] = c
    s1_ref[...] = jnp.where(first, -s, 0.0)
    s2_ref[...] = jnp.where(first, 0.0, s)


def _rope_tables(ls):
    tr = min(ls, 512)
    quarter = DH_A // 4
    inv = ROPE_BASE ** (-jnp.arange(quarter, dtype=F32) / quarter)
    inv = jnp.tile(inv, LANES // quarter)[None, :]
    shp = jax.ShapeDtypeStruct((ls, LANES), F32)
    spec = pl.BlockSpec((tr, LANES), lambda i: (i, 0))
    return pl.pallas_call(
        functools.partial(_rope_tab_kernel, tr=tr),
        grid=(ls // tr,),
        in_specs=[pl.BlockSpec((1, LANES), lambda i: (0, 0))],
        out_specs=[spec, spec, spec],
        out_shape=[shp, shp, shp],
        compiler_params=_cparams(("arbitrary",)),
        name="rope_tables",
    )(inv)


def _rope_kernel(x_ref, c_ref, s1_ref, s2_ref, o_ref, *, n_q):
    quarter = DH_A // 4
    c, s1, s2 = c_ref[...], s1_ref[...], s2_ref[...]
    for j in range(x_ref.shape[1] // LANES):
        x = x_ref[:, j * LANES:(j + 1) * LANES].astype(F32)
        y = x * c + pltpu.roll(x, LANES - quarter, 1) * s1 + pltpu.roll(x, quarter, 1) * s2
        if j < n_q:
            y = y * Q_SCALE
        o_ref[:, j * LANES:(j + 1) * LANES] = y.astype(BF16)


def _rope_qk(p, tabs, row0, ts, ls, col0, n_heads):
    tr = min(ls, 512)
    width = 2 * n_heads * LANES
    tab_spec = pl.BlockSpec((tr, LANES), lambda i: (i % (ls // tr), 0))
    return pl.pallas_call(
        functools.partial(_rope_kernel, n_q=n_heads),
        grid=(ts // tr,),
        in_specs=[pl.BlockSpec((tr, width), lambda i: (row0 // tr + i, col0 // width)),
                  tab_spec, tab_spec, tab_spec],
        out_specs=pl.BlockSpec((tr, width), lambda i: (i, 0)),
        out_shape=jax.ShapeDtypeStruct((ts, width), BF16),
        compiler_params=_cparams(("parallel",)),
        name="rope_qk",
    )(p, *tabs)


ATTN_SUB = 128


def _attn_kernel(lam_ref, sw_ref, q_ref, k_ref, v_ref, g_ref, *rest, lam_init, q_scale, has_cache):
    if has_cache:
        kc_ref, vc_ref, o_ref = rest
    else:
        (o_ref,) = rest
    lp = lam_ref[...]
    lam = (jnp.exp(jnp.sum(lp[0:1, :] * lp[1:2, :], axis=1, keepdims=True))
           - jnp.exp(jnp.sum(lp[2:3, :] * lp[3:4, :], axis=1, keepdims=True)) + lam_init)
    tq = q_ref.shape[0]
    rs = min(ATTN_SUB, tq)
    lane = lax.broadcasted_iota(jnp.int32, (rs, HEAD_A), 1)
    zero = jnp.zeros((rs, HEAD_A), BF16)

    def head_kv(hh):
        hs = slice(hh * HEAD_A, (hh + 1) * HEAD_A)
        k = k_ref[:, hs].astype(BF16)
        v = v_ref[:, hs].astype(BF16)
        if has_cache:
            k = jnp.concatenate([kc_ref[:, hs], k], axis=0)
            v = jnp.concatenate([vc_ref[:, hs], v], axis=0)
        return k, jnp.concatenate([v, jnp.ones_like(v)], axis=1)

    def logits(hh, i, k):
        q = q_ref[i * rs:(i + 1) * rs, hh * HEAD_A:(hh + 1) * HEAD_A]
        if q_scale != 1.0:
            q = q.astype(F32) * q_scale
        q = q.astype(BF16)
        lhs = jnp.concatenate([jnp.where(lane < DH_A, q, zero), jnp.where(lane < DH_A, zero, q)], axis=0)
        return lax.dot_general(lhs, k, _NT, preferred_element_type=F32)

    items = [(hh, i) for hh in range(q_ref.shape[1] // HEAD_A) for i in range(tq // rs)]
    kv = {}
    for idx, (hh, i) in enumerate(items):
        if hh not in kv:
            kv[hh] = head_kv(hh)
        if idx == 0:
            s = logits(hh, i, kv[hh][0])
        if idx + 1 < len(items):
            nh, ni = items[idx + 1]
            if nh not in kv:
                kv[nh] = head_kv(nh)
            s_next = logits(nh, ni, kv[nh][0])
        e = jnp.exp2(s - jnp.max(s, axis=-1, keepdims=True)).astype(BF16)
        oa = _mm(e, kv[hh][1])
        on = oa[:, :HEAD_A] * (1.0 / oa[:, HEAD_A:HEAD_A + 1])
        o = on[:rs] - lam * on[rs:]
        o = o * lax.rsqrt(jnp.mean(o * o, axis=-1, keepdims=True) + EPS) * sw_ref[...] * (1.0 - lam_init)
        rows, hs = slice(i * rs, (i + 1) * rs), slice(hh * HEAD_A, (hh + 1) * HEAD_A)
        o_ref[rows, hs] = (o * _silu(g_ref[rows, hs].astype(F32))).astype(BF16)
        if idx + 1 < len(items):
            s = s_next


def _attention(lam_p, subln_w, q_arr, q_map, k_arr, k_map, v_arr, v_map, g_arr, g_map, cache,
               n_b, n_heads, hps, lq, lk_new, tq, lam_init, q_scale):
    nq = lq // tq
    wide = hps * HEAD_A
    in_specs = [pl.BlockSpec(lam_p.shape, lambda b, h, i: (0, 0)),
                pl.BlockSpec((1, HEAD_A), lambda b, h, i: (0, 0)),
                pl.BlockSpec((tq, wide), q_map),
                pl.BlockSpec((lk_new, wide), k_map),
                pl.BlockSpec((lk_new, wide), v_map),
                pl.BlockSpec((tq, wide), g_map)]
    args = [lam_p, subln_w, q_arr, k_arr, v_arr, g_arr]
    if cache is not None:
        kc, vc, past = cache
        cspec = pl.BlockSpec((past, wide), lambda b, h, i: (b, h))
        in_specs += [cspec, cspec]
        args += [kc, vc]
    return pl.pallas_call(
        functools.partial(_attn_kernel, lam_init=lam_init, q_scale=q_scale, has_cache=cache is not None),
        grid=(n_b, n_heads // hps, nq),
        in_specs=in_specs,
        out_specs=pl.BlockSpec((tq, wide), lambda b, h, i: (b * nq + i, h)),
        out_shape=jax.ShapeDtypeStruct((n_b * lq, n_heads * HEAD_A), BF16),
        compiler_params=_cparams(("parallel", "parallel", "arbitrary")),
        name="diff_attn",
    )(*args)


def _conv_kernel(u_ref, up_ref, un_ref, g_ref, cw_ref, cb_ref, lw_ref, lb_ref, o_ref, zp_scr, acc_scr, zs_scr,
                 *, tt, n_prompt_blocks, blocks_per_prompt_seq, blocks_per_sample_seq):
    cb_ = u_ref.shape[1] // 2
    blk = pl.program_id(0)
    in_prompt = blk < n_prompt_blocks
    pos = jnp.where(in_prompt, blk % blocks_per_prompt_seq, (blk - n_prompt_blocks) % blocks_per_sample_seq)
    per_seq = jnp.where(in_prompt, blocks_per_prompt_seq, blocks_per_sample_seq)
    keep_prev = (pos > 0).astype(F32)
    keep_next = (pos < per_seq - 1).astype(F32)

    def glu(ref):
        u = ref[...].astype(F32)
        return u[:, :cb_] * jax.nn.sigmoid(u[:, cb_:])

    h = CONV_HALO
    zp_scr[pl.ds(0, h), :] = glu(up_ref) * keep_prev
    zp_scr[pl.ds(h, tt), :] = glu(u_ref)
    zp_scr[pl.ds(h + tt, h), :] = glu(un_ref) * keep_next

    rc = 64
    off = h - CONV_W // 2
    n_sh = zs_scr.shape[1]

    def lane_chunk(c, carry):
        ls = pl.ds(pl.multiple_of(c * LANES, LANES), LANES)
        w = cw_ref[:, ls]
        for s in range(SUBLANES):
            zs_scr[s] = zp_scr[pl.ds(s, n_sh), ls]
        for r0 in range(0, tt, rc):
            acc = jnp.zeros((rc, LANES), F32)
            for tau in range(CONV_W):
                o = tau + off
                acc = acc + zs_scr[o % SUBLANES, pl.ds(r0 + o - o % SUBLANES, rc), :] * w[tau:tau + 1, :]
            acc_scr[pl.ds(r0, rc), ls] = acc
        return carry

    lax.fori_loop(0, cb_ // LANES, lane_chunk, 0)

    z = acc_scr[...] + cb_ref[...]
    mu = jnp.mean(z, axis=-1, keepdims=True)
    d = z - mu
    var = jnp.mean(d * d, axis=-1, keepdims=True)
    zn = d * lax.rsqrt(var + LN_EPS) * lw_ref[...] + lb_ref[...]
    o_ref[...] = (_silu(zn) * _silu(g_ref[...].astype(F32))).astype(BF16)


def _conv(p, conv_w, conv_b, ln_w, ln_b, glu_col0, gb_col0, tt, n_prompt_blocks,
          blocks_per_prompt_seq, blocks_per_sample_seq):
    t = p.shape[0]
    cb_ = conv_w.shape[1]
    nblk = t // tt
    hb = tt // CONV_HALO
    n_hblk = t // CONV_HALO
    vec = pl.BlockSpec((1, cb_), lambda i: (0, 0))
    return pl.pallas_call(
        functools.partial(_conv_kernel, tt=tt, n_prompt_blocks=n_prompt_blocks,
                          blocks_per_prompt_seq=blocks_per_prompt_seq,
                          blocks_per_sample_seq=blocks_per_sample_seq),
        grid=(nblk,),
        in_specs=[pl.BlockSpec((tt, 2 * cb_), lambda i: (i, glu_col0 // (2 * cb_))),
                  pl.BlockSpec((CONV_HALO, 2 * cb_), lambda i: (jnp.maximum(i * hb - 1, 0), glu_col0 // (2 * cb_))),
                  pl.BlockSpec((CONV_HALO, 2 * cb_),
                               lambda i: (jnp.minimum((i + 1) * hb, n_hblk - 1), glu_col0 // (2 * cb_))),
                  pl.BlockSpec((tt, cb_), lambda i: (i, gb_col0 // cb_)),
                  pl.BlockSpec(conv_w.shape, lambda i: (0, 0)),
                  vec, vec, vec],
        out_specs=pl.BlockSpec((tt, cb_), lambda i: (i, 0)),
        out_shape=jax.ShapeDtypeStruct((t, cb_), BF16),
        scratch_shapes=[pltpu.VMEM((tt + 2 * CONV_HALO, cb_), F32), pltpu.VMEM((tt, cb_), F32),
                        pltpu.VMEM((SUBLANES, tt + 2 * CONV_HALO - SUBLANES, LANES), F32)],
        compiler_params=_cparams(("parallel",)),
        name="conformer_conv",
    )(p, p, p, p, conv_w, conv_b, ln_w, ln_b)


def _rwkv_prep_kernel(k_ref, xw_ref, xa_ref, w0_ref, wup_ref, a0_ref, aup_ref, kk_ref, ka_ref, e2_ref,
                      okk_ref, olw_ref, okd_ref, obd_ref):
    k = k_ref[...].astype(F32)
    e2 = e2_ref[...]
    kkr = k * kk_ref[...]
    kk = kkr / jnp.maximum(jnp.sqrt(_head_sums(kkr * kkr, e2)), 1e-12)
    okk_ref[...] = kk
    txw = jnp.tanh(xw_ref[...].astype(F32))
    xa = xa_ref[...].astype(F32)
    for d in range(2):
        w_log = -jax.nn.softplus(-(w0_ref[d:d + 1, :] + _dot3(txw, wup_ref[d]))) - 0.5
        olw_ref[d] = -jnp.exp(w_log)
        a = jax.nn.sigmoid(a0_ref[d:d + 1, :] + _dot3(xa, aup_ref[d]))
        okd_ref[d] = k * (1.0 + (a - 1.0) * ka_ref[...])
        obd_ref[d] = kk * a


def _rwkv_prep(p, kc_col0, xw_col0, xa_col0, w0, wup_pad, a0, aup_pad, k_k, k_a, e2, tm):
    t = p.shape[0]
    cc = w0.shape[1]
    vec = pl.BlockSpec((1, cc), lambda i: (0, 0))
    vec2 = pl.BlockSpec((2, cc), lambda i: (0, 0))
    mat = pl.BlockSpec((2, LANES, cc), lambda i: (0, 0, 0))
    one = jax.ShapeDtypeStruct((t, cc), F32)
    two = jax.ShapeDtypeStruct((2, t, cc), F32)
    two_spec = pl.BlockSpec((2, tm, cc), lambda i: (0, i, 0))
    return pl.pallas_call(
        _rwkv_prep_kernel,
        grid=(t // tm,),
        in_specs=[pl.BlockSpec((tm, cc), lambda i: (i, kc_col0 // cc)),
                  pl.BlockSpec((tm, LANES), lambda i: (i, xw_col0 // LANES)),
                  pl.BlockSpec((tm, LANES), lambda i: (i, xa_col0 // LANES)),
                  vec2, mat, vec2, mat, vec, vec,
                  pl.BlockSpec((LANES, LANES), lambda i: (0, 0))],
        out_specs=[pl.BlockSpec((tm, cc), lambda i: (i, 0)), two_spec, two_spec, two_spec],
        out_shape=[one, two, two, two],
        compiler_params=_cparams(("parallel",)),
        name="rwkv_prep",
    )(p, p, p, w0, wup_pad, a0, aup_pad, k_k, k_a, e2)


CHUNK = 64
PAIRS_PER_BODY = 8


def _mmb(a, b):
    return _mm(a.astype(BF16), b.astype(BF16))


def _chunk_unit(r, v, kk, lw, c, kd, bd, s0, reverse, lane_lo, eye2, strict, incl, same_head):
    w2 = 2 * CHUNK
    ctot = c[0:1, :] if reverse else c[CHUNK - 1:CHUNK, :]
    g_inv = jnp.exp(-c)
    g_rem = jnp.exp(ctot - c)
    rt = r * jnp.exp(c)
    bt = -(kk * jnp.exp(c - lw))

    def stack(x):
        xb = x.astype(BF16)
        z = jnp.zeros_like(xb)
        return jnp.concatenate([jnp.where(lane_lo, xb, z), jnp.where(lane_lo, z, xb)], axis=0)

    v_s = stack(v)
    s0b = s0.astype(BF16)
    lhs = jnp.concatenate([bt, rt], axis=0).astype(BF16)
    rhs = jnp.concatenate([stack(bd * g_inv), stack(kd * g_inv)], axis=0)
    nn = lax.dot_general(lhs, rhs, _NT, preferred_element_type=F32)
    yield None
    a_ab = jnp.where(strict, nn[:CHUNK, :w2], 0.0)
    a_bk = jnp.where(strict, nn[:CHUNK, w2:], 0.0)
    a_r = jnp.concatenate([jnp.where(incl, nn[CHUNK:, :w2], 0.0), jnp.where(incl, nn[CHUNK:, w2:], 0.0)], axis=1)
    p1 = _mm(a_bk.astype(BF16), v_s)
    n = a_ab
    t = eye2 + n
    for _ in range(int(math.log2(CHUNK)) - 1):
        n = _mm(n.astype(BF16), stack(n))
        yield None
        t = t + _mm(t.astype(BF16), stack(n))
    yield None
    wx = _mm(t.astype(BF16), jnp.concatenate([stack(bt), stack(p1)], axis=1))
    yield None
    w, x = wx[:, :w2], wx[:, w2:]
    rhs_big = jnp.concatenate(
        [jnp.concatenate([stack(w), stack(x)], axis=1),
         jnp.concatenate([jnp.zeros((w2, w2), BF16), v_s], axis=1)], axis=0)
    qy = _mm(a_r.astype(BF16), rhs_big)
    w_pad = jnp.concatenate([w, jnp.zeros_like(w)], axis=0).astype(BF16)
    uv_t = jnp.concatenate([x, v], axis=0).T + lax.dot_general(s0b, w_pad, _NT, preferred_element_type=F32)
    yield None
    qhat = rt + qy[:, :w2]
    y = lax.dot_general(qhat.astype(BF16), s0b, _NT, preferred_element_type=F32) + qy[:, w2:]
    upd = _mmb(uv_t, jnp.concatenate([bd * g_rem, kd * g_rem], axis=0))
    yield s0 * jnp.exp(ctot) + jnp.where(same_head, upd, 0.0), y


def _run_lockstep(gens):
    last = [None] * len(gens)
    live = list(range(len(gens)))
    while live:
        nxt = []
        for i in live:
            try:
                last[i] = next(gens[i])
                nxt.append(i)
            except StopIteration:
                pass
        live = nxt
    return last


def _rwkv_chunk_kernel(rf_ref, vf_ref, kkf_ref, lwf_ref, kdf_ref, bdf_ref,
                       rb_ref, vb_ref, kkb_ref, lwb_ref, kdb_ref, bdb_ref, h0_ref,
                       yf_ref, yb_ref, hfin_ref, h_scr, c_scr):
    @pl.when(pl.program_id(1) == 0)
    def _():
        h_scr[...] = h0_ref[0]

    ti = lax.broadcasted_iota(jnp.int32, (CHUNK, CHUNK), 0)
    si = lax.broadcasted_iota(jnp.int32, (CHUNK, CHUNK), 1)

    def cumsum(tri, x):
        hi = x.astype(BF16)
        r1 = x - hi.astype(F32)
        mid = r1.astype(BF16)
        lo = (r1 - mid.astype(F32)).astype(BF16)
        return _mm(tri, hi) + _mm(tri, mid) + _mm(tri, lo)

    c_scr[0] = cumsum((si <= ti).astype(BF16), lwf_ref[...])
    c_scr[1] = cumsum((si >= ti).astype(BF16), lwb_ref[...])

    w2 = 2 * CHUNK
    tpos = lax.broadcasted_iota(jnp.int32, (CHUNK, w2), 0)
    lane = lax.broadcasted_iota(jnp.int32, (CHUNK, w2), 1)
    spos = lane % CHUNK
    eye2 = (spos == tpos).astype(F32)
    lane_lo = lane < N_C
    same_head = (lax.broadcasted_iota(jnp.int32, (w2, w2), 0) // N_C
                 == lax.broadcasted_iota(jnp.int32, (w2, w2), 1) // N_C)
    dirs = ((rf_ref, vf_ref, kkf_ref, lwf_ref, kdf_ref, bdf_ref, yf_ref, spos < tpos, spos <= tpos),
            (rb_ref, vb_ref, kkb_ref, lwb_ref, kdb_ref, bdb_ref, yb_ref, spos > tpos, spos >= tpos))

    def pair_group(pg, carry):
        units = []
        for pi in range(PAIRS_PER_BODY):
            p = pg * PAIRS_PER_BODY + pi
            ls = pl.ds(pl.multiple_of(p * w2, w2), w2)
            for d, (r_ref, v_ref, kk_ref, lw_ref, kd_ref, bd_ref, y_ref, strict, incl) in enumerate(dirs):
                args = (r_ref[:, ls].astype(F32), v_ref[:, ls].astype(F32), kk_ref[:, ls], lw_ref[:, ls], c_scr[d, :, ls],
                        kd_ref[:, ls], bd_ref[:, ls], h_scr[d, p])
                units.append((d, p, ls, y_ref, args, strict, incl))
        results = _run_lockstep([_chunk_unit(*args, d == 1, lane_lo, eye2, strict, incl, same_head)
                                 for d, p, ls, y_ref, args, strict, incl in units])
        for (d, p, ls, y_ref, _, _, _), (h_new, y) in zip(units, results):
            h_scr[d, p] = h_new
            y_ref[:, ls] = y
        return carry

    lax.fori_loop(0, rf_ref.shape[1] // (w2 * PAIRS_PER_BODY), pair_group, 0)

    @pl.when(pl.program_id(1) == pl.num_programs(1) - 1)
    def _():
        hfin_ref[0] = h_scr[...]


def _rwkv_chunks(p, kk, lw, kd, bd, h0, row0, nseq, l, rc_col0, vc_col0):
    cc = kk.shape[1]
    nc = l // CHUNK
    rb0 = row0 // CHUNK
    npair = cc // (2 * N_C)

    def fwd(s, g):
        return rb0 + s * nc + g

    def bwd(s, g):
        return rb0 + s * nc + nc - 1 - g

    def specs(rowf, d):
        return [pl.BlockSpec((CHUNK, cc), lambda s, g: (rowf(s, g), rc_col0 // cc)),
                pl.BlockSpec((CHUNK, cc), lambda s, g: (rowf(s, g), vc_col0 // cc)),
                pl.BlockSpec((CHUNK, cc), lambda s, g: (rowf(s, g), 0)),
                pl.BlockSpec((None, CHUNK, cc), lambda s, g: (d, rowf(s, g), 0)),
                pl.BlockSpec((None, CHUNK, cc), lambda s, g: (d, rowf(s, g), 0)),
                pl.BlockSpec((None, CHUNK, cc), lambda s, g: (d, rowf(s, g), 0))]

    hspec = pl.BlockSpec((1, 2, npair, 2 * N_C, 2 * N_C), lambda s, g: (s, 0, 0, 0, 0))
    yshape = jax.ShapeDtypeStruct((nseq * l, cc), F32)
    return pl.pallas_call(
        _rwkv_chunk_kernel,
        grid=(nseq, nc),
        in_specs=specs(fwd, 0) + specs(bwd, 1) + [hspec],
        out_specs=[pl.BlockSpec((CHUNK, cc), lambda s, g: (s * nc + g, 0)),
                   pl.BlockSpec((CHUNK, cc), lambda s, g: (s * nc + nc - 1 - g, 0)),
                   hspec],
        out_shape=[yshape, yshape, jax.ShapeDtypeStruct(h0.shape, F32)],
        scratch_shapes=[pltpu.VMEM((2, npair, 2 * N_C, 2 * N_C), F32), pltpu.VMEM((2, CHUNK, cc), F32)],
        compiler_params=_cparams(("parallel", "arbitrary")),
        name="rwkv_chunks",
    )(p, p, kk, lw, kd, bd, p, p, kk, lw, kd, bd, h0)


def _state_to_wide(s):
    nseq, _, nh, n, _ = s.shape
    ht = s.reshape(nseq, 2, nh // 2, 2, n, n)
    z = jnp.zeros_like(ht[:, :, :, 0])
    top = jnp.concatenate([ht[:, :, :, 0], z], axis=-1)
    bot = jnp.concatenate([z, ht[:, :, :, 1]], axis=-1)
    return jnp.concatenate([top, bot], axis=-2)


def _wide_to_state(h):
    nseq, _, npair, w2, _ = h.shape
    n = w2 // 2
    blocks = jnp.stack([h[..., :n, :n], h[..., n:, n:]], axis=3)
    return blocks.reshape(nseq, 2, 2 * npair, n, n)


def _rwkv_post_kernel(yf_ref, yb_ref, r_ref, v_ref, kd_ref, g_ref, rk_ref, gw_ref, gb_ref, e2_ref, o_ref):
    e2 = e2_ref[...]
    y = yf_ref[...] + yb_ref[...]
    inv_n = 1.0 / N_C
    d = y - _head_sums(y, e2) * inv_n
    var = _head_sums(d * d, e2) * inv_n
    yn = d * lax.rsqrt(var + GN_EPS) * gw_ref[...] + gb_ref[...]
    bonus = (_head_sums(r_ref[...].astype(F32) * (kd_ref[0] + kd_ref[1]) * rk_ref[...], e2)
             * v_ref[...].astype(F32))
    o_ref[...] = ((yn + bonus) * _silu(g_ref[...].astype(F32))).astype(BF16)


def _rwkv_post(yf, yb, p, kd, row0, rc_col0, vc_col0, gc_col0, r_k, gn_w, gn_b, e2, tm):
    t, cc = yf.shape
    rb0 = row0 // tm
    vec = pl.BlockSpec((1, cc), lambda i: (0, 0))
    own = pl.BlockSpec((tm, cc), lambda i: (i, 0))
    return pl.pallas_call(
        _rwkv_post_kernel,
        grid=(t // tm,),
        in_specs=[own, own,
                  pl.BlockSpec((tm, cc), lambda i: (rb0 + i, rc_col0 // cc)),
                  pl.BlockSpec((tm, cc), lambda i: (rb0 + i, vc_col0 // cc)),
                  pl.BlockSpec((2, tm, cc), lambda i: (0, rb0 + i, 0)),
                  pl.BlockSpec((tm, cc), lambda i: (rb0 + i, gc_col0 // cc)),
                  vec, vec, vec,
                  pl.BlockSpec((LANES, LANES), lambda i: (0, 0))],
        out_specs=own,
        out_shape=jax.ShapeDtypeStruct((t, cc), BF16),
        compiler_params=_cparams(("parallel",)),
        name="rwkv_post",
    )(yf, yb, p, p, kd, p, r_k, gn_w, gn_b, e2)


def _merge_kernel(ba_ref, bb_ref, bc_ref, m0_ref, m1_ref, m2_ref, wa_ref, wb_ref, wc_ref, o_ref):
    o_ref[...] = (jax.nn.sigmoid(m0_ref[...].astype(F32)) * _mm(ba_ref[...], wa_ref[...])
                  + jax.nn.sigmoid(m1_ref[...].astype(F32)) * _mm(bb_ref[...], wb_ref[...])
                  + jax.nn.sigmoid(m2_ref[...].astype(F32)) * _mm(bc_ref[...], wc_ref[...])).astype(BF16)


def _merge(ba, bb, bc, p, mg_col0, wa, wb, wc, tm, tn):
    t, wbr = ba.shape
    d = wa.shape[1]
    nj = d // tn
    j0 = mg_col0 // tn
    br = pl.BlockSpec((tm, wbr), lambda i, j: (i, 0))
    wspec = pl.BlockSpec((wbr, tn), lambda i, j: (0, j))
    return pl.pallas_call(
        _merge_kernel,
        grid=(t // tm, nj),
        in_specs=[br, br, br,
                  pl.BlockSpec((tm, tn), lambda i, j: (i, j0 + j)),
                  pl.BlockSpec((tm, tn), lambda i, j: (i, j0 + nj + j)),
                  pl.BlockSpec((tm, tn), lambda i, j: (i, j0 + 2 * nj + j)),
                  wspec, wspec, wspec],
        out_specs=pl.BlockSpec((tm, tn), lambda i, j: (i, j)),
        out_shape=jax.ShapeDtypeStruct((t, d), BF16),
        compiler_params=_cparams(("parallel", "arbitrary")),
        name="branch_merge",
    )(ba, bb, bc, p, p, p, wa, wb, wc)


def _resid_kernel(m_ref, wo_ref, x_ref, mod_ref, fw_ref, o_ref, *, final):
    x = x_ref[...] + mod_ref[0, 2:3, :] * _mm(m_ref[...], wo_ref[...])
    if final:
        x = x * lax.rsqrt(jnp.mean(x * x, axis=-1, keepdims=True) + EPS) * fw_ref[...]
    o_ref[...] = x


def _out_proj(merged, wo, x, mod, final_w, cond_of_tile, tm, final):
    t, d = x.shape
    row = pl.BlockSpec((tm, d), lambda i: (i, 0))
    return pl.pallas_call(
        functools.partial(_resid_kernel, final=final),
        grid=(t // tm,),
        in_specs=[row,
                  pl.BlockSpec((d, d), lambda i: (0, 0)),
                  row,
                  pl.BlockSpec((1, 3, d), lambda i: (cond_of_tile(i), 0, 0)),
                  pl.BlockSpec((1, d), lambda i: (0, 0))],
        out_specs=row,
        out_shape=jax.ShapeDtypeStruct((t, d), F32),
        compiler_params=_cparams(("parallel",)),
        name="out_proj",
    )(merged, wo, x, mod, final_w)


def _largest_tile(n, cap):
    t = cap
    while n % t:
        t //= 2
    return t


def kernel(x_prompt, x_sample, cache_k, cache_v, state_rwkv, c, c_ctx, w_ada, b_ada, norm_w, w_in,
           lambda_q1, lambda_k1, lambda_q2, lambda_k2, subln_w, conv_w, conv_b, conv_ln_w, conv_ln_b,
           rwkv_w0, rwkv_w_up, rwkv_a0, rwkv_a_up, rwkv_k_k, rwkv_k_a, rwkv_r_k, rwkv_gn_w, rwkv_gn_b,
           w_br_a, w_br_b, w_br_c, w_out, final_norm_w):
    bp, lp_, d = x_prompt.shape
    bs, ls, _ = x_sample.shape
    depth = w_ada.shape[0]
    past = cache_k.shape[2]
    tp, ts = bp * lp_, bs * ls
    n_ha = d // 256
    w_a = n_ha * HEAD_A
    c_b = d // 2
    n_hc = d // (2 * N_C)
    c_c = n_hc * N_C
    n_mg = 3 * d
    assert n_ha * 2 * DH_A == w_a and w_a == c_b == c_c
    assert bs + 1 <= SUBLANES

    col = {}
    off = 0
    for name, width in (("q", w_a), ("k", w_a), ("v", w_a), ("g_a", w_a), ("glu", 2 * c_b),
                        ("g_b", c_b), ("r_c", c_c), ("k_c", c_c), ("v_c", c_c), ("g_c", c_c),
                        ("xw", 2 * LORA), ("xa", 2 * LORA), ("mg", n_mg)):
        col[name] = off
        off += width
    in_cols = off

    tm_in = _largest_tile(math.gcd(tp, ls), 1024)
    tm_out = _largest_tile(math.gcd(tp, ls), 512)
    tn_in = 768 if in_cols % 768 == 0 else 256
    tm_mg = _largest_tile(tp + ts, 1024)
    tn_mg = math.gcd(col["mg"], 512)
    tt = _largest_tile(math.gcd(lp_, ls), 256)
    tm_rw = _largest_tile(math.gcd(tp, ts), 256)

    def cond_of(tm):
        npt = tp // tm
        return lambda i: jnp.where(i < npt, 0, 1 + ((i - npt) * tm) // ls)

    x = jnp.concatenate([x_prompt.reshape(tp, d), x_sample.reshape(ts, d)], axis=0)
    cond8 = jnp.zeros((SUBLANES, d), F32).at[0].set(c_ctx).at[1:1 + bs].set(c)
    rope_tabs = _rope_tables(ls)
    blk = jnp.arange(LANES) // N_C
    e2 = (blk[:, None] == blk[None, :]).astype(BF16)
    zpad = jnp.zeros((LORA, c_c), F32)

    ks, vs, ss = [], [], []
    for l in range(depth):
        lam_init = 0.8 - 0.6 * math.exp(-0.3 * l)
        mod = _ada(cond8, w_ada[l], b_ada[l][None, :]).reshape(SUBLANES, 3, d)
        p = _inproj(x, mod, norm_w[l][None, :], w_in[l].astype(BF16), cond_of(tm_in), tm_in, tn_in)

        lam_p = jnp.stack([lambda_q1[l], lambda_k1[l], lambda_q2[l], lambda_k2[l]], axis=0)
        sw = subln_w[l][None, :]
        cq, ck, cv, cg = (col[n] // w_a for n in ("q", "k", "v", "g_a"))
        tq_p = _largest_tile(lp_, 256)
        npq = lp_ // tq_p
        ba_p = _attention(
            lam_p, sw,
            p, lambda b, h, i: (b * npq + i, cq),
            p, lambda b, h, i: (b, ck),
            p, lambda b, h, i: (b, cv),
            p, lambda b, h, i: (b * npq + i, cg),
            None, bp, n_ha, n_ha, lp_, lp_, tq_p, lam_init, Q_SCALE)
        qk_rot = _rope_qk(p, rope_tabs, tp, ts, ls, col["q"], n_ha)
        cache = (cache_k[:, l].reshape(bs * past, w_a).astype(BF16),
                 cache_v[:, l].reshape(bs * past, w_a).astype(BF16), past)
        cv1, cg1 = col["v"] // HEAD_A, col["g_a"] // HEAD_A
        tq_s = _largest_tile(ls, 512)
        nsq = ls // tq_s
        ba_s = _attention(
            lam_p, sw,
            qk_rot, lambda b, h, i: (b * nsq + i, h),
            qk_rot, lambda b, h, i: (b, n_ha + h),
            p, lambda b, h, i: (tp // ls + b, cv1 + h),
            p, lambda b, h, i: (tp // tq_s + b * nsq + i, cg1 + h),
            cache, bs, n_ha, 1, ls, ls, tq_s, lam_init, 1.0)
        ba = jnp.concatenate([ba_p, ba_s], axis=0)

        bb = _conv(p, conv_w[l], conv_b[l][None, :], conv_ln_w[l][None, :], conv_ln_b[l][None, :],
                   col["glu"], col["g_b"], tt, tp // tt, lp_ // tt, ls // tt)

        wup_pad = jnp.stack([jnp.concatenate([rwkv_w_up[l, 0], zpad], 0), jnp.concatenate([zpad, rwkv_w_up[l, 1]], 0)])
        aup_pad = jnp.stack([jnp.concatenate([rwkv_a_up[l, 0], zpad], 0), jnp.concatenate([zpad, rwkv_a_up[l, 1]], 0)])
        kk, lw, kd, bd = _rwkv_prep(p, col["k_c"], col["xw"], col["xa"], rwkv_w0[l], wup_pad, rwkv_a0[l], aup_pad,
                                    rwkv_k_k[l][None, :], rwkv_k_a[l][None, :], e2, tm_rw)
        h0_p = jnp.zeros((bp, 2, n_hc // 2, 2 * N_C, 2 * N_C), F32)
        yf_p, yb_p, hfin_p = _rwkv_chunks(p, kk, lw, kd, bd, h0_p, 0, bp, lp_, col["r_c"], col["v_c"])
        ss.append(_wide_to_state(hfin_p))
        yf_s, yb_s, _ = _rwkv_chunks(p, kk, lw, kd, bd, _state_to_wide(state_rwkv[:, l]), tp, bs, ls,
                                     col["r_c"], col["v_c"])
        post_args = (col["r_c"], col["v_c"], col["g_c"], rwkv_r_k[l].reshape(1, c_c),
                     rwkv_gn_w[l][None, :], rwkv_gn_b[l][None, :], e2, tm_rw)
        bc = jnp.concatenate([_rwkv_post(yf_p, yb_p, p, kd, 0, *post_args),
                              _rwkv_post(yf_s, yb_s, p, kd, tp, *post_args)], axis=0)

        merged = _merge(ba, bb, bc, p, col["mg"], w_br_a[l].astype(BF16), w_br_b[l].astype(BF16), w_br_c[l].astype(BF16),
                        tm_mg, tn_mg)
        x = _out_proj(merged, w_out[l].astype(BF16), x, mod, final_norm_w[None, :], cond_of(tm_out), tm_out,
                      final=(l == depth - 1))

        ks.append(p[:tp, col["k"]:col["k"] + w_a].astype(F32).reshape(bp, lp_, n_ha, 2, DH_A))
        vs.append(p[:tp, col["v"]:col["v"] + w_a].astype(F32).reshape(bp, lp_, n_ha, HEAD_A))

    y_prompt = x[:tp].reshape(bp, lp_, d)
    y_sample = x[tp:].reshape(bs, ls, d)
    return (y_prompt, y_sample, jnp.stack(ks, axis=1), jnp.stack(vs, axis=1), jnp.stack(ss, axis=1))
```

```python
import functools
import math

import jax
import jax.numpy as jnp
from jax import lax
from jax.experimental import pallas as pl
from jax.experimental.pallas import tpu as pltpu

F32 = jnp.float32
BF16 = jnp.bfloat16

HEAD_A = 128
DH_A = 64
CONV_W = 31
CONV_HALO = 16
N_C = 64
LORA = 64
GRID_W = 64
ROPE_BASE = 10000.0
EPS = 1e-6
LN_EPS = 1e-5
GN_EPS = 64e-5
Q_SCALE = DH_A ** -0.5 * math.log2(math.e)

LANES = 128
SUBLANES = 8
MIB = 1024 * 1024


def _cparams(semantics, vmem_mib=48):
    return pltpu.CompilerParams(dimension_semantics=semantics, vmem_limit_bytes=vmem_mib * MIB)


def _split_bf16(x):
    hi = x.astype(BF16)
    lo = (x - hi.astype(F32)).astype(BF16)
    return hi, lo


_NT = (((1,), (1,)), ((), ()))


def _mm(a, b):
    return jnp.dot(a, b, preferred_element_type=F32)


def _dot3(a, b):
    ah, al = _split_bf16(a)
    bh, bl = _split_bf16(b)
    return _mm(ah, bh) + _mm(ah, bl) + _mm(al, bh)


def _dot2(a, b_bf16):
    ah, al = _split_bf16(a)
    return _mm(ah, b_bf16) + _mm(al, b_bf16)


def _silu(x):
    return x * jax.nn.sigmoid(x)


def _head_sums(x, e2):
    parts = [_dot2(x[:, c * LANES:(c + 1) * LANES], e2) for c in range(x.shape[1] // LANES)]
    return jnp.concatenate(parts, axis=1)


def _ada_kernel(c_ref, w_ref, b_ref, o_ref):
    o_ref[...] = _dot3(_silu(c_ref[...]), w_ref[...]) + b_ref[...]


def _ada(cond8, w, b, layer):
    _, d, n = w.shape
    tn = 512
    return pl.pallas_call(
        _ada_kernel,
        grid=(n // tn,),
        in_specs=[pl.BlockSpec((SUBLANES, d), lambda j: (0, 0)),
                  pl.BlockSpec((None, d, tn), lambda j: (layer, 0, j)),
                  pl.BlockSpec((None, 1, tn), lambda j: (layer, 0, j))],
        out_specs=pl.BlockSpec((SUBLANES, tn), lambda j: (0, j)),
        out_shape=jax.ShapeDtypeStruct((SUBLANES, n), F32),
        compiler_params=_cparams(("arbitrary",)),
        name="ada_mod",
    )(cond8, w, b)


def _inproj_kernel(x_ref, mod_ref, nw_ref, w_ref, o_ref, h_scr):
    @pl.when(pl.program_id(1) == 0)
    def _():
        x = x_ref[...]
        y = x * lax.rsqrt(jnp.mean(x * x, axis=-1, keepdims=True) + EPS) * nw_ref[...]
        h = y * (1.0 + mod_ref[0, 1:2, :]) + mod_ref[0, 0:1, :]
        h_scr[...] = h.astype(BF16)

    o_ref[...] = _mm(h_scr[...], w_ref[...]).astype(o_ref.dtype)


def _inproj(x, mod, norm_w, w_bf16, cond_of_tile, tm, tn):
    t, d = x.shape
    n = w_bf16.shape[1]
    return pl.pallas_call(
        _inproj_kernel,
        grid=(t // tm, n // tn),
        in_specs=[pl.BlockSpec((tm, d), lambda i, j: (i, 0)),
                  pl.BlockSpec((1, 3, d), lambda i, j: (cond_of_tile(i), 0, 0)),
                  pl.BlockSpec((1, d), lambda i, j: (0, 0)),
                  pl.BlockSpec((d, tn), lambda i, j: (0, j))],
        out_specs=pl.BlockSpec((tm, tn), lambda i, j: (i, j)),
        out_shape=jax.ShapeDtypeStruct((t, n), BF16),
        scratch_shapes=[pltpu.VMEM((tm, d), BF16)],
        compiler_params=_cparams(("parallel", "arbitrary")),
        name="inproj",
    )(x, mod, norm_w, w_bf16)


def _rope_tab_kernel(inv_ref, c_ref, s1_ref, s2_ref, *, tr):
    t = lax.broadcasted_iota(jnp.int32, (tr, LANES), 0) + pl.program_id(0) * tr
    lane = lax.broadcasted_iota(jnp.int32, (tr, LANES), 1)
    row = (t // GRID_W).astype(F32)
    col = (t % GRID_W).astype(F32)
    pos = jnp.where((lane % DH_A) >= DH_A // 2, col, row)
    ang = pos * inv_ref[...]
    c = jnp.cos(ang)
    s = jnp.sin(ang)
    first = (lane % (DH_A // 2)) < DH_A // 4
    c_ref[...] = c
    s1_ref[...] = jnp.where(first, -s, 0.0)
    s2_ref[...] = jnp.where(first, 0.0, s)


def _rope_tables(ls):
    tr = min(ls, 512)
    quarter = DH_A // 4
    inv = ROPE_BASE ** (-jnp.arange(quarter, dtype=F32) / quarter)
    inv = jnp.tile(inv, LANES // quarter)[None, :]
    shp = jax.ShapeDtypeStruct((ls, LANES), F32)
    spec = pl.BlockSpec((tr, LANES), lambda i: (i, 0))
    return pl.pallas_call(
        functools.partial(_rope_tab_kernel, tr=tr),
        grid=(ls // tr,),
        in_specs=[pl.BlockSpec((1, LANES), lambda i: (0, 0))],
        out_specs=[spec, spec, spec],
        out_shape=[shp, shp, shp],
        compiler_params=_cparams(("arbitrary",)),
        name="rope_tables",
    )(inv)


def _rope_kernel(x_ref, c_ref, s1_ref, s2_ref, o_ref, *, n_q):
    quarter = DH_A // 4
    c, s1, s2 = c_ref[...], s1_ref[...], s2_ref[...]
    for j in range(x_ref.shape[1] // LANES):
        x = x_ref[:, j * LANES:(j + 1) * LANES].astype(F32)
        y = x * c + pltpu.roll(x, LANES - quarter, 1) * s1 + pltpu.roll(x, quarter, 1) * s2
        if j < n_q:
            y = y * Q_SCALE
        o_ref[:, j * LANES:(j + 1) * LANES] = y.astype(BF16)


def _rope_qk(p, tabs, row0, ts, ls, col0, n_heads):
    tr = min(ls, 512)
    width = 2 * n_heads * LANES
    tab_spec = pl.BlockSpec((tr, LANES), lambda i: (i % (ls // tr), 0))
    return pl.pallas_call(
        functools.partial(_rope_kernel, n_q=n_heads),
        grid=(ts // tr,),
        in_specs=[pl.BlockSpec((tr, width), lambda i: (row0 // tr + i, col0 // width)),
                  tab_spec, tab_spec, tab_spec],
        out_specs=pl.BlockSpec((tr, width), lambda i: (i, 0)),
        out_shape=jax.ShapeDtypeStruct((ts, width), BF16),
        compiler_params=_cparams(("parallel",)),
        name="rope_qk",
    )(p, *tabs)


ATTN_SUB = 128


def _attn_kernel(lam_ref, sw_ref, q_ref, k_ref, v_ref, g_ref, *rest, lam_init, q_scale, has_cache):
    if has_cache:
        kc_ref, vc_ref, o_ref = rest
    else:
        (o_ref,) = rest
    lp = lam_ref[...]
    lam = (jnp.exp(jnp.sum(lp[0:1, :] * lp[1:2, :], axis=1, keepdims=True))
           - jnp.exp(jnp.sum(lp[2:3, :] * lp[3:4, :], axis=1, keepdims=True)) + lam_init)
    tq = q_ref.shape[0]
    rs = min(ATTN_SUB, tq)
    lane = lax.broadcasted_iota(jnp.int32, (rs, HEAD_A), 1)
    zero = jnp.zeros((rs, HEAD_A), BF16)

    def head_kv(hh):
        hs = slice(hh * HEAD_A, (hh + 1) * HEAD_A)
        k = k_ref[:, hs].astype(BF16)
        v = v_ref[:, hs].astype(BF16)
        if has_cache:
            k = jnp.concatenate([kc_ref[:, hs], k], axis=0)
            v = jnp.concatenate([vc_ref[:, hs], v], axis=0)
        return k, jnp.concatenate([v, jnp.ones_like(v)], axis=1)

    def logits(hh, i, k):
        q = q_ref[i * rs:(i + 1) * rs, hh * HEAD_A:(hh + 1) * HEAD_A]
        if q_scale != 1.0:
            q = q.astype(F32) * q_scale
        q = q.astype(BF16)
        lhs = jnp.concatenate([jnp.where(lane < DH_A, q, zero), jnp.where(lane < DH_A, zero, q)], axis=0)
        return lax.dot_general(lhs, k, _NT, preferred_element_type=F32)

    items = [(hh, i) for hh in range(q_ref.shape[1] // HEAD_A) for i in range(tq // rs)]
    kv = {}
    for idx, (hh, i) in enumerate(items):
        if hh not in kv:
            kv[hh] = head_kv(hh)
        if idx == 0:
            s = logits(hh, i, kv[hh][0])
        if idx + 1 < len(items):
            nh, ni = items[idx + 1]
            if nh not in kv:
                kv[nh] = head_kv(nh)
            s_next = logits(nh, ni, kv[nh][0])
        e = jnp.exp2(s - jnp.max(s, axis=-1, keepdims=True)).astype(BF16)
        oa = _mm(e, kv[hh][1])
        on = oa[:, :HEAD_A] * (1.0 / oa[:, HEAD_A:HEAD_A + 1])
        o = on[:rs] - lam * on[rs:]
        o = o * lax.rsqrt(jnp.mean(o * o, axis=-1, keepdims=True) + EPS) * sw_ref[...] * (1.0 - lam_init)
        rows, hs = slice(i * rs, (i + 1) * rs), slice(hh * HEAD_A, (hh + 1) * HEAD_A)
        o_ref[rows, hs] = (o * _silu(g_ref[rows, hs].astype(F32))).astype(BF16)
        if idx + 1 < len(items):
            s = s_next


def _attention(lam_p, subln_w, q_arr, q_map, k_arr, k_map, v_arr, v_map, g_arr, g_map, cache,
               n_b, n_heads, hps, lq, lk_new, tq, lam_init, q_scale):
    nq = lq // tq
    wide = hps * HEAD_A
    in_specs = [pl.BlockSpec(lam_p.shape, lambda b, h, i: (0, 0)),
                pl.BlockSpec((1, HEAD_A), lambda b, h, i: (0, 0)),
                pl.BlockSpec((tq, wide), q_map),
                pl.BlockSpec((lk_new, wide), k_map),
                pl.BlockSpec((lk_new, wide), v_map),
                pl.BlockSpec((tq, wide), g_map)]
    args = [lam_p, subln_w, q_arr, k_arr, v_arr, g_arr]
    if cache is not None:
        kc, vc, past = cache
        cspec = pl.BlockSpec((past, wide), lambda b, h, i: (b, h))
        in_specs += [cspec, cspec]
        args += [kc, vc]
    return pl.pallas_call(
        functools.partial(_attn_kernel, lam_init=lam_init, q_scale=q_scale, has_cache=cache is not None),
        grid=(n_b, n_heads // hps, nq),
        in_specs=in_specs,
        out_specs=pl.BlockSpec((tq, wide), lambda b, h, i: (b * nq + i, h)),
        out_shape=jax.ShapeDtypeStruct((n_b * lq, n_heads * HEAD_A), BF16),
        compiler_params=_cparams(("parallel", "parallel", "arbitrary")),
        name="diff_attn",
    )(*args)


def _conv_kernel(u_ref, up_ref, un_ref, g_ref, cw_ref, cb_ref, lw_ref, lb_ref, o_ref, zp_scr, acc_scr, zs_scr,
                 *, tt, n_prompt_blocks, blocks_per_prompt_seq, blocks_per_sample_seq):
    cb_ = u_ref.shape[1] // 2
    blk = pl.program_id(0)
    in_prompt = blk < n_prompt_blocks
    pos = jnp.where(in_prompt, blk % blocks_per_prompt_seq, (blk - n_prompt_blocks) % blocks_per_sample_seq)
    per_seq = jnp.where(in_prompt, blocks_per_prompt_seq, blocks_per_sample_seq)
    keep_prev = (pos > 0).astype(F32)
    keep_next = (pos < per_seq - 1).astype(F32)

    def glu(ref):
        u = ref[...].astype(F32)
        return u[:, :cb_] * jax.nn.sigmoid(u[:, cb_:])

    h = CONV_HALO
    zp_scr[pl.ds(0, h), :] = glu(up_ref) * keep_prev
    zp_scr[pl.ds(h, tt), :] = glu(u_ref)
    zp_scr[pl.ds(h + tt, h), :] = glu(un_ref) * keep_next

    rc = 64
    off = h - CONV_W // 2
    n_sh = zs_scr.shape[1]

    def lane_chunk(c, carry):
        ls = pl.ds(pl.multiple_of(c * LANES, LANES), LANES)
        w = cw_ref[:, ls]
        for s in range(SUBLANES):
            zs_scr[s] = zp_scr[pl.ds(s, n_sh), ls]
        for r0 in range(0, tt, rc):
            acc = jnp.zeros((rc, LANES), F32)
            for tau in range(CONV_W):
                o = tau + off
                acc = acc + zs_scr[o % SUBLANES, pl.ds(r0 + o - o % SUBLANES, rc), :] * w[tau:tau + 1, :]
            acc_scr[pl.ds(r0, rc), ls] = acc
        return carry

    lax.fori_loop(0, cb_ // LANES, lane_chunk, 0)

    z = acc_scr[...] + cb_ref[...]
    mu = jnp.mean(z, axis=-1, keepdims=True)
    d = z - mu
    var = jnp.mean(d * d, axis=-1, keepdims=True)
    zn = d * lax.rsqrt(var + LN_EPS) * lw_ref[...] + lb_ref[...]
    o_ref[...] = (_silu(zn) * _silu(g_ref[...].astype(F32))).astype(BF16)


def _conv(p, conv_w, conv_b, ln_w, ln_b, glu_col0, gb_col0, tt, n_prompt_blocks,
          blocks_per_prompt_seq, blocks_per_sample_seq):
    t = p.shape[0]
    cb_ = conv_w.shape[1]
    nblk = t // tt
    hb = tt // CONV_HALO
    n_hblk = t // CONV_HALO
    vec = pl.BlockSpec((1, cb_), lambda i: (0, 0))
    return pl.pallas_call(
        functools.partial(_conv_kernel, tt=tt, n_prompt_blocks=n_prompt_blocks,
                          blocks_per_prompt_seq=blocks_per_prompt_seq,
                          blocks_per_sample_seq=blocks_per_sample_seq),
        grid=(nblk,),
        in_specs=[pl.BlockSpec((tt, 2 * cb_), lambda i: (i, glu_col0 // (2 * cb_))),
                  pl.BlockSpec((CONV_HALO, 2 * cb_), lambda i: (jnp.maximum(i * hb - 1, 0), glu_col0 // (2 * cb_))),
                  pl.BlockSpec((CONV_HALO, 2 * cb_),
                               lambda i: (jnp.minimum((i + 1) * hb, n_hblk - 1), glu_col0 // (2 * cb_))),
                  pl.BlockSpec((tt, cb_), lambda i: (i, gb_col0 // cb_)),
                  pl.BlockSpec(conv_w.shape, lambda i: (0, 0)),
                  vec, vec, vec],
        out_specs=pl.BlockSpec((tt, cb_), lambda i: (i, 0)),
        out_shape=jax.ShapeDtypeStruct((t, cb_), BF16),
        scratch_shapes=[pltpu.VMEM((tt + 2 * CONV_HALO, cb_), F32), pltpu.VMEM((tt, cb_), F32),
                        pltpu.VMEM((SUBLANES, tt + 2 * CONV_HALO - SUBLANES, LANES), F32)],
        compiler_params=_cparams(("parallel",)),
        name="conformer_conv",
    )(p, p, p, p, conv_w, conv_b, ln_w, ln_b)


def _rwkv_prep_kernel(k_ref, xw_ref, xa_ref, w0_ref, wup_ref, a0_ref, aup_ref, kk_ref, ka_ref, e2_ref,
                      okk_ref, olw_ref, okd_ref, obd_ref):
    k = k_ref[...].astype(F32)
    e2 = e2_ref[...]
    kkr = k * kk_ref[...]
    kk = kkr / jnp.maximum(jnp.sqrt(_head_sums(kkr * kkr, e2)), 1e-12)
    okk_ref[...] = kk
    txw = jnp.tanh(xw_ref[...].astype(F32))
    xa = xa_ref[...].astype(F32)
    for d in range(2):
        w_log = -jax.nn.softplus(-(w0_ref[d:d + 1, :] + _dot3(txw, wup_ref[d]))) - 0.5
        olw_ref[d] = -jnp.exp(w_log)
        a = jax.nn.sigmoid(a0_ref[d:d + 1, :] + _dot3(xa, aup_ref[d]))
        okd_ref[d] = k * (1.0 + (a - 1.0) * ka_ref[...])
        obd_ref[d] = kk * a


def _rwkv_prep(p, kc_col0, xw_col0, xa_col0, w0, wup_pad, a0, aup_pad, k_k, k_a, e2, tm):
    t = p.shape[0]
    cc = w0.shape[1]
    vec = pl.BlockSpec((1, cc), lambda i: (0, 0))
    vec2 = pl.BlockSpec((2, cc), lambda i: (0, 0))
    mat = pl.BlockSpec((2, LANES, cc), lambda i: (0, 0, 0))
    one = jax.ShapeDtypeStruct((t, cc), F32)
    two = jax.ShapeDtypeStruct((2, t, cc), F32)
    two_spec = pl.BlockSpec((2, tm, cc), lambda i: (0, i, 0))
    return pl.pallas_call(
        _rwkv_prep_kernel,
        grid=(t // tm,),
        in_specs=[pl.BlockSpec((tm, cc), lambda i: (i, kc_col0 // cc)),
                  pl.BlockSpec((tm, LANES), lambda i: (i, xw_col0 // LANES)),
                  pl.BlockSpec((tm, LANES), lambda i: (i, xa_col0 // LANES)),
                  vec2, mat, vec2, mat, vec, vec,
                  pl.BlockSpec((LANES, LANES), lambda i: (0, 0))],
        out_specs=[pl.BlockSpec((tm, cc), lambda i: (i, 0)), two_spec, two_spec, two_spec],
        out_shape=[one, two, two, two],
        compiler_params=_cparams(("parallel",)),
        name="rwkv_prep",
    )(p, p, p, w0, wup_pad, a0, aup_pad, k_k, k_a, e2)


CHUNK = 64
PAIRS_PER_BODY = 8


def _mmb(a, b):
    return _mm(a.astype(BF16), b.astype(BF16))


def _chunk_unit(r, v, kk, lw, c, kd, bd, s0, reverse, lane_lo, eye2, strict, incl, same_head):
    w2 = 2 * CHUNK
    ctot = c[0:1, :] if reverse else c[CHUNK - 1:CHUNK, :]
    g_inv = jnp.exp(-c)
    g_rem = jnp.exp(ctot - c)
    rt = r * jnp.exp(c)
    bt = -(kk * jnp.exp(c - lw))

    def stack(x):
        xb = x.astype(BF16)
        z = jnp.zeros_like(xb)
        return jnp.concatenate([jnp.where(lane_lo, xb, z), jnp.where(lane_lo, z, xb)], axis=0)

    v_s = stack(v)
    s0b = s0.astype(BF16)
    lhs = jnp.concatenate([bt, rt], axis=0).astype(BF16)
    rhs = jnp.concatenate([stack(bd * g_inv), stack(kd * g_inv)], axis=0)
    nn = lax.dot_general(lhs, rhs, _NT, preferred_element_type=F32)
    yield None
    a_ab = jnp.where(strict, nn[:CHUNK, :w2], 0.0)
    a_bk = jnp.where(strict, nn[:CHUNK, w2:], 0.0)
    a_r = jnp.concatenate([jnp.where(incl, nn[CHUNK:, :w2], 0.0), jnp.where(incl, nn[CHUNK:, w2:], 0.0)], axis=1)
    p1 = _mm(a_bk.astype(BF16), v_s)
    t = eye2 + a_ab
    n = _mm(a_ab.astype(BF16), stack(a_ab))
    yield None
    for _ in range(int(math.log2(CHUNK)) - 2):
        prod = _mm(jnp.concatenate([n, t], axis=0).astype(BF16), stack(n))
        yield None
        t = t + prod[CHUNK:]
        n = prod[:CHUNK]
    t = t + _mm(t.astype(BF16), stack(n))
    yield None
    wx = _mm(t.astype(BF16), jnp.concatenate([stack(bt), stack(p1)], axis=1))
    yield None
    w, x = wx[:, :w2], wx[:, w2:]
    rhs_big = jnp.concatenate(
        [jnp.concatenate([stack(w), stack(x)], axis=1),
         jnp.concatenate([jnp.zeros((w2, w2), BF16), v_s], axis=1)], axis=0)
    qy = _mm(a_r.astype(BF16), rhs_big)
    w_pad = jnp.concatenate([w, jnp.zeros_like(w)], axis=0).astype(BF16)
    uv_t = jnp.concatenate([x, v], axis=0).T + lax.dot_general(s0b, w_pad, _NT, preferred_element_type=F32)
    yield None
    qhat = rt + qy[:, :w2]
    y = lax.dot_general(qhat.astype(BF16), s0b, _NT, preferred_element_type=F32) + qy[:, w2:]
    upd = _mmb(uv_t, jnp.concatenate([bd * g_rem, kd * g_rem], axis=0))
    yield s0 * jnp.exp(ctot) + jnp.where(same_head, upd, 0.0), y


def _run_lockstep(gens):
    last = [None] * len(gens)
    live = list(range(len(gens)))
    while live:
        nxt = []
        for i in live:
            try:
                last[i] = next(gens[i])
                nxt.append(i)
            except StopIteration:
                pass
        live = nxt
    return last


def _rwkv_chunk_kernel(rf_ref, vf_ref, kkf_ref, lwf_ref, kdf_ref, bdf_ref,
                       rb_ref, vb_ref, kkb_ref, lwb_ref, kdb_ref, bdb_ref, *rest, has_init):
    if has_init:
        h0_ref, yf_ref, yb_ref, hfin_ref, h_scr, c_scr = rest
    else:
        yf_ref, yb_ref, hfin_ref, h_scr, c_scr = rest

    @pl.when(pl.program_id(1) == 0)
    def _():
        h_scr[...] = h0_ref[0] if has_init else jnp.zeros(h_scr.shape, F32)

    ti = lax.broadcasted_iota(jnp.int32, (CHUNK, CHUNK), 0)
    si = lax.broadcasted_iota(jnp.int32, (CHUNK, CHUNK), 1)

    def cumsum(tri, x):
        hi = x.astype(BF16)
        r1 = x - hi.astype(F32)
        mid = r1.astype(BF16)
        lo = (r1 - mid.astype(F32)).astype(BF16)
        return _mm(tri, hi) + _mm(tri, mid) + _mm(tri, lo)

    c_scr[0] = cumsum((si <= ti).astype(BF16), lwf_ref[...])
    c_scr[1] = cumsum((si >= ti).astype(BF16), lwb_ref[...])

    w2 = 2 * CHUNK
    tpos = lax.broadcasted_iota(jnp.int32, (CHUNK, w2), 0)
    lane = lax.broadcasted_iota(jnp.int32, (CHUNK, w2), 1)
    spos = lane % CHUNK
    eye2 = (spos == tpos).astype(F32)
    lane_lo = lane < N_C
    same_head = (lax.broadcasted_iota(jnp.int32, (w2, w2), 0) // N_C
                 == lax.broadcasted_iota(jnp.int32, (w2, w2), 1) // N_C)
    dirs = ((rf_ref, vf_ref, kkf_ref, lwf_ref, kdf_ref, bdf_ref, yf_ref, spos < tpos, spos <= tpos),
            (rb_ref, vb_ref, kkb_ref, lwb_ref, kdb_ref, bdb_ref, yb_ref, spos > tpos, spos >= tpos))

    def pair_group(pg, carry):
        units = []
        for pi in range(PAIRS_PER_BODY):
            p = pg * PAIRS_PER_BODY + pi
            ls = pl.ds(pl.multiple_of(p * w2, w2), w2)
            for d, (r_ref, v_ref, kk_ref, lw_ref, kd_ref, bd_ref, y_ref, strict, incl) in enumerate(dirs):
                args = (r_ref[:, ls].astype(F32), v_ref[:, ls].astype(F32), kk_ref[:, ls], lw_ref[:, ls], c_scr[d, :, ls],
                        kd_ref[:, ls], bd_ref[:, ls], h_scr[d, p])
                units.append((d, p, ls, y_ref, args, strict, incl))
        results = _run_lockstep([_chunk_unit(*args, d == 1, lane_lo, eye2, strict, incl, same_head)
                                 for d, p, ls, y_ref, args, strict, incl in units])
        for (d, p, ls, y_ref, _, _, _), (h_new, y) in zip(units, results):
            h_scr[d, p] = h_new
            y_ref[:, ls] = y
        return carry

    lax.fori_loop(0, rf_ref.shape[1] // (w2 * PAIRS_PER_BODY), pair_group, 0)

    @pl.when(pl.program_id(1) == pl.num_programs(1) - 1)
    def _():
        hfin_ref[0] = h_scr[...]


def _rwkv_chunks(p, kk, lw, kd, bd, h0, row0, nseq, l, rc_col0, vc_col0):
    cc = kk.shape[1]
    nc = l // CHUNK
    rb0 = row0 // CHUNK
    npair = cc // (2 * N_C)

    def fwd(s, g):
        return rb0 + s * nc + g

    def bwd(s, g):
        return rb0 + s * nc + nc - 1 - g

    def specs(rowf, d):
        return [pl.BlockSpec((CHUNK, cc), lambda s, g: (rowf(s, g), rc_col0 // cc)),
                pl.BlockSpec((CHUNK, cc), lambda s, g: (rowf(s, g), vc_col0 // cc)),
                pl.BlockSpec((CHUNK, cc), lambda s, g: (rowf(s, g), 0)),
                pl.BlockSpec((None, CHUNK, cc), lambda s, g: (d, rowf(s, g), 0)),
                pl.BlockSpec((None, CHUNK, cc), lambda s, g: (d, rowf(s, g), 0)),
                pl.BlockSpec((None, CHUNK, cc), lambda s, g: (d, rowf(s, g), 0))]

    hshape = (nseq, 2, npair, 2 * N_C, 2 * N_C)
    hspec = pl.BlockSpec((1,) + hshape[1:], lambda s, g: (s, 0, 0, 0, 0))
    yshape = jax.ShapeDtypeStruct((nseq * l, cc), F32)
    init = [] if h0 is None else [h0]
    return pl.pallas_call(
        functools.partial(_rwkv_chunk_kernel, has_init=h0 is not None),
        grid=(nseq, nc),
        in_specs=specs(fwd, 0) + specs(bwd, 1) + [hspec] * len(init),
        out_specs=[pl.BlockSpec((CHUNK, cc), lambda s, g: (s * nc + g, 0)),
                   pl.BlockSpec((CHUNK, cc), lambda s, g: (s * nc + nc - 1 - g, 0)),
                   hspec],
        out_shape=[yshape, yshape, jax.ShapeDtypeStruct(hshape, F32)],
        scratch_shapes=[pltpu.VMEM(hshape[1:], F32), pltpu.VMEM((2, CHUNK, cc), F32)],
        compiler_params=_cparams(("parallel", "arbitrary")),
        name="rwkv_chunks",
    )(p, p, kk, lw, kd, bd, p, p, kk, lw, kd, bd, *init)


def _state_to_wide(s):
    nseq, _, nh, n, _ = s.shape
    ht = s.reshape(nseq, 2, nh // 2, 2, n, n)
    z = jnp.zeros_like(ht[:, :, :, 0])
    top = jnp.concatenate([ht[:, :, :, 0], z], axis=-1)
    bot = jnp.concatenate([z, ht[:, :, :, 1]], axis=-1)
    return jnp.concatenate([top, bot], axis=-2)


def _wide_to_state(h):
    nseq, _, npair, w2, _ = h.shape
    n = w2 // 2
    blocks = jnp.stack([h[..., :n, :n], h[..., n:, n:]], axis=3)
    return blocks.reshape(nseq, 2, 2 * npair, n, n)


def _rwkv_post_kernel(yf_ref, yb_ref, r_ref, v_ref, kd_ref, g_ref, rk_ref, gw_ref, gb_ref, e2_ref, o_ref):
    e2 = e2_ref[...]
    y = yf_ref[...] + yb_ref[...]
    inv_n = 1.0 / N_C
    d = y - _head_sums(y, e2) * inv_n
    var = _head_sums(d * d, e2) * inv_n
    yn = d * lax.rsqrt(var + GN_EPS) * gw_ref[...] + gb_ref[...]
    bonus = (_head_sums(r_ref[...].astype(F32) * (kd_ref[0] + kd_ref[1]) * rk_ref[...], e2)
             * v_ref[...].astype(F32))
    o_ref[...] = ((yn + bonus) * _silu(g_ref[...].astype(F32))).astype(BF16)


def _rwkv_post(yf, yb, p, kd, row0, rc_col0, vc_col0, gc_col0, r_k, gn_w, gn_b, e2, tm):
    t, cc = yf.shape
    rb0 = row0 // tm
    vec = pl.BlockSpec((1, cc), lambda i: (0, 0))
    own = pl.BlockSpec((tm, cc), lambda i: (i, 0))
    return pl.pallas_call(
        _rwkv_post_kernel,
        grid=(t // tm,),
        in_specs=[own, own,
                  pl.BlockSpec((tm, cc), lambda i: (rb0 + i, rc_col0 // cc)),
                  pl.BlockSpec((tm, cc), lambda i: (rb0 + i, vc_col0 // cc)),
                  pl.BlockSpec((2, tm, cc), lambda i: (0, rb0 + i, 0)),
                  pl.BlockSpec((tm, cc), lambda i: (rb0 + i, gc_col0 // cc)),
                  vec, vec, vec,
                  pl.BlockSpec((LANES, LANES), lambda i: (0, 0))],
        out_specs=own,
        out_shape=jax.ShapeDtypeStruct((t, cc), BF16),
        compiler_params=_cparams(("parallel",)),
        name="rwkv_post",
    )(yf, yb, p, p, kd, p, r_k, gn_w, gn_b, e2)


def _merge_kernel(bap_ref, bas_ref, bb_ref, bcp_ref, bcs_ref, m0_ref, m1_ref, m2_ref, wa_ref, wb_ref, wc_ref, o_ref,
                  *, n_prompt_tiles):
    in_prompt = pl.program_id(0) < n_prompt_tiles
    ba = jnp.where(in_prompt, bap_ref[...], bas_ref[...])
    bc = jnp.where(in_prompt, bcp_ref[...], bcs_ref[...])
    o_ref[...] = (jax.nn.sigmoid(m0_ref[...].astype(F32)) * _mm(ba, wa_ref[...])
                  + jax.nn.sigmoid(m1_ref[...].astype(F32)) * _mm(bb_ref[...], wb_ref[...])
                  + jax.nn.sigmoid(m2_ref[...].astype(F32)) * _mm(bc, wc_ref[...])).astype(BF16)


def _merge(ba_p, ba_s, bb, bc_p, bc_s, p, mg_col0, wa, wb, wc, tm, tn):
    t, wbr = bb.shape
    d = wa.shape[1]
    nj = d // tn
    j0 = mg_col0 // tn
    npt = ba_p.shape[0] // tm
    prm = pl.BlockSpec((tm, wbr), lambda i, j: (jnp.minimum(i, npt - 1), 0))
    smp = pl.BlockSpec((tm, wbr), lambda i, j: (jnp.maximum(i - npt, 0), 0))
    wspec = pl.BlockSpec((wbr, tn), lambda i, j: (0, j))
    return pl.pallas_call(
        functools.partial(_merge_kernel, n_prompt_tiles=npt),
        grid=(t // tm, nj),
        in_specs=[prm, smp, pl.BlockSpec((tm, wbr), lambda i, j: (i, 0)), prm, smp,
                  pl.BlockSpec((tm, tn), lambda i, j: (i, j0 + j)),
                  pl.BlockSpec((tm, tn), lambda i, j: (i, j0 + nj + j)),
                  pl.BlockSpec((tm, tn), lambda i, j: (i, j0 + 2 * nj + j)),
                  wspec, wspec, wspec],
        out_specs=pl.BlockSpec((tm, tn), lambda i, j: (i, j)),
        out_shape=jax.ShapeDtypeStruct((t, d), BF16),
        compiler_params=_cparams(("parallel", "arbitrary")),
        name="branch_merge",
    )(ba_p, ba_s, bb, bc_p, bc_s, p, p, p, wa, wb, wc)


def _resid_kernel(m_ref, wo_ref, x_ref, mod_ref, fw_ref, *o_refs, n_prompt_tiles):
    x = x_ref[...] + mod_ref[0, 2:3, :] * _mm(m_ref[...], wo_ref[...])
    if len(o_refs) == 1:
        o_refs[0][...] = x
        return
    x = x * lax.rsqrt(jnp.mean(x * x, axis=-1, keepdims=True) + EPS) * fw_ref[...]
    in_prompt = pl.program_id(0) < n_prompt_tiles

    @pl.when(in_prompt)
    def _():
        o_refs[0][...] = x

    @pl.when(jnp.logical_not(in_prompt))
    def _():
        o_refs[1][...] = x


def _out_proj(merged, wo, x, mod, final_w, cond_of_tile, tm, tp, final):
    t, d = x.shape
    npt = tp // tm
    row = pl.BlockSpec((tm, d), lambda i: (i, 0))
    if final:
        out_specs = [pl.BlockSpec((tm, d), lambda i: (jnp.minimum(i, npt - 1), 0)),
                     pl.BlockSpec((tm, d), lambda i: (jnp.maximum(i - npt, 0), 0))]
        out_shape = [jax.ShapeDtypeStruct((tp, d), F32), jax.ShapeDtypeStruct((t - tp, d), F32)]
    else:
        out_specs, out_shape = row, jax.ShapeDtypeStruct((t, d), F32)
    return pl.pallas_call(
        functools.partial(_resid_kernel, n_prompt_tiles=npt),
        grid=(t // tm,),
        in_specs=[row,
                  pl.BlockSpec((d, d), lambda i: (0, 0)),
                  row,
                  pl.BlockSpec((1, 3, d), lambda i: (cond_of_tile(i), 0, 0)),
                  pl.BlockSpec((1, d), lambda i: (0, 0))],
        out_specs=out_specs,
        out_shape=out_shape,
        compiler_params=_cparams(("arbitrary",)),
        name="out_proj",
    )(merged, wo, x, mod, final_w)


def _largest_tile(n, cap):
    t = cap
    while n % t:
        t //= 2
    return t


def kernel(x_prompt, x_sample, cache_k, cache_v, state_rwkv, c, c_ctx, w_ada, b_ada, norm_w, w_in,
           lambda_q1, lambda_k1, lambda_q2, lambda_k2, subln_w, conv_w, conv_b, conv_ln_w, conv_ln_b,
           rwkv_w0, rwkv_w_up, rwkv_a0, rwkv_a_up, rwkv_k_k, rwkv_k_a, rwkv_r_k, rwkv_gn_w, rwkv_gn_b,
           w_br_a, w_br_b, w_br_c, w_out, final_norm_w):
    bp, lp_, d = x_prompt.shape
    bs, ls, _ = x_sample.shape
    depth = w_ada.shape[0]
    past = cache_k.shape[2]
    tp, ts = bp * lp_, bs * ls
    n_ha = d // 256
    w_a = n_ha * HEAD_A
    c_b = d // 2
    n_hc = d // (2 * N_C)
    c_c = n_hc * N_C
    n_mg = 3 * d
    assert n_ha * 2 * DH_A == w_a and w_a == c_b == c_c
    assert bs + 1 <= SUBLANES

    col = {}
    off = 0
    for name, width in (("q", w_a), ("k", w_a), ("v", w_a), ("g_a", w_a), ("glu", 2 * c_b),
                        ("g_b", c_b), ("r_c", c_c), ("k_c", c_c), ("v_c", c_c), ("g_c", c_c),
                        ("xw", 2 * LORA), ("xa", 2 * LORA), ("mg", n_mg)):
        col[name] = off
        off += width
    in_cols = off

    tm_in = _largest_tile(math.gcd(tp, ls), 1024)
    tm_out = _largest_tile(math.gcd(tp, ls), 512)
    tn_in = 768 if in_cols % 768 == 0 else 256
    tm_mg = _largest_tile(tp + ts, 1024)
    tn_mg = math.gcd(col["mg"], 512)
    tt = _largest_tile(math.gcd(lp_, ls), 256)
    tm_rw = _largest_tile(math.gcd(tp, ts), 256)

    def cond_of(tm):
        npt = tp // tm
        return lambda i: jnp.where(i < npt, 0, 1 + ((i - npt) * tm) // ls)

    x = jnp.concatenate([x_prompt.reshape(tp, d), x_sample.reshape(ts, d)], axis=0)
    cond8 = jnp.zeros((SUBLANES, d), F32).at[0].set(c_ctx).at[1:1 + bs].set(c)
    rope_tabs = _rope_tables(ls)
    blk = jnp.arange(LANES) // N_C
    e2 = (blk[:, None] == blk[None, :]).astype(BF16)
    zpad = jnp.zeros((LORA, c_c), F32)

    ks, vs, ss = [], [], []
    for l in range(depth):
        lam_init = 0.8 - 0.6 * math.exp(-0.3 * l)
        mod = _ada(cond8, w_ada, b_ada[:, None, :], l).reshape(SUBLANES, 3, d)
        p = _inproj(x, mod, norm_w[l][None, :], w_in[l].astype(BF16), cond_of(tm_in), tm_in, tn_in)

        lam_p = jnp.stack([lambda_q1[l], lambda_k1[l], lambda_q2[l], lambda_k2[l]], axis=0)
        sw = subln_w[l][None, :]
        cq, ck, cv, cg = (col[n] // w_a for n in ("q", "k", "v", "g_a"))
        tq_p = _largest_tile(lp_, 256)
        npq = lp_ // tq_p
        ba_p = _attention(
            lam_p, sw,
            p, lambda b, h, i: (b * npq + i, cq),
            p, lambda b, h, i: (b, ck),
            p, lambda b, h, i: (b, cv),
            p, lambda b, h, i: (b * npq + i, cg),
            None, bp, n_ha, n_ha, lp_, lp_, tq_p, lam_init, Q_SCALE)
        qk_rot = _rope_qk(p, rope_tabs, tp, ts, ls, col["q"], n_ha)
        cache = (cache_k[:, l].reshape(bs * past, w_a).astype(BF16),
                 cache_v[:, l].reshape(bs * past, w_a).astype(BF16), past)
        cv1, cg1 = col["v"] // HEAD_A, col["g_a"] // HEAD_A
        tq_s = _largest_tile(ls, 512)
        nsq = ls // tq_s
        ba_s = _attention(
            lam_p, sw,
            qk_rot, lambda b, h, i: (b * nsq + i, h),
            qk_rot, lambda b, h, i: (b, n_ha + h),
            p, lambda b, h, i: (tp // ls + b, cv1 + h),
            p, lambda b, h, i: (tp // tq_s + b * nsq + i, cg1 + h),
            cache, bs, n_ha, 1, ls, ls, tq_s, lam_init, 1.0)

        bb = _conv(p, conv_w[l], conv_b[l][None, :], conv_ln_w[l][None, :], conv_ln_b[l][None, :],
                   col["glu"], col["g_b"], tt, tp // tt, lp_ // tt, ls // tt)

        wup_pad = jnp.stack([jnp.concatenate([rwkv_w_up[l, 0], zpad], 0), jnp.concatenate([zpad, rwkv_w_up[l, 1]], 0)])
        aup_pad = jnp.stack([jnp.concatenate([rwkv_a_up[l, 0], zpad], 0), jnp.concatenate([zpad, rwkv_a_up[l, 1]], 0)])
        kk, lw, kd, bd = _rwkv_prep(p, col["k_c"], col["xw"], col["xa"], rwkv_w0[l], wup_pad, rwkv_a0[l], aup_pad,
                                    rwkv_k_k[l][None, :], rwkv_k_a[l][None, :], e2, tm_rw)
        yf_p, yb_p, hfin_p = _rwkv_chunks(p, kk, lw, kd, bd, None, 0, bp, lp_, col["r_c"], col["v_c"])
        ss.append(_wide_to_state(hfin_p))
        yf_s, yb_s, _ = _rwkv_chunks(p, kk, lw, kd, bd, _state_to_wide(state_rwkv[:, l]), tp, bs, ls,
                                     col["r_c"], col["v_c"])
        post_args = (col["r_c"], col["v_c"], col["g_c"], rwkv_r_k[l].reshape(1, c_c),
                     rwkv_gn_w[l][None, :], rwkv_gn_b[l][None, :], e2, tm_rw)
        bc_p = _rwkv_post(yf_p, yb_p, p, kd, 0, *post_args)
        bc_s = _rwkv_post(yf_s, yb_s, p, kd, tp, *post_args)

        merged = _merge(ba_p, ba_s, bb, bc_p, bc_s, p, col["mg"], w_br_a[l].astype(BF16), w_br_b[l].astype(BF16), w_br_c[l].astype(BF16),
                        tm_mg, tn_mg)
        x = _out_proj(merged, w_out[l].astype(BF16), x, mod, final_norm_w[None, :], cond_of(tm_out), tm_out, tp,
                      final=(l == depth - 1))

        ks.append(p[:tp, col["k"]:col["k"] + w_a].astype(F32).reshape(bp, lp_, n_ha, 2, DH_A))
        vs.append(p[:tp, col["v"]:col["v"] + w_a].astype(F32).reshape(bp, lp_, n_ha, HEAD_A))

    y_prompt = x[0].reshape(bp, lp_, d)
    y_sample = x[1].reshape(bs, ls, d)
    return (y_prompt, y_sample, jnp.stack(ks, axis=1), jnp.stack(vs, axis=1), jnp.stack(ss, axis=1))
```

```python
import functools
import math

import jax
import jax.numpy as jnp
from jax import lax
from jax.experimental import pallas as pl
from jax.experimental.pallas import tpu as pltpu

F32 = jnp.float32
BF16 = jnp.bfloat16

HEAD_A = 128
DH_A = 64
CONV_W = 31
CONV_HALO = 16
N_C = 64
LORA = 64
GRID_W = 64
ROPE_BASE = 10000.0
EPS = 1e-6
LN_EPS = 1e-5
GN_EPS = 64e-5
Q_SCALE = DH_A ** -0.5 * math.log2(math.e)

LANES = 128
SUBLANES = 8
MIB = 1024 * 1024


def _cparams(semantics, vmem_mib=48):
    return pltpu.CompilerParams(dimension_semantics=semantics, vmem_limit_bytes=vmem_mib * MIB)


def _split_bf16(x):
    hi = x.astype(BF16)
    lo = (x - hi.astype(F32)).astype(BF16)
    return hi, lo


_NT = (((1,), (1,)), ((), ()))


def _mm(a, b):
    return jnp.dot(a, b, preferred_element_type=F32)


def _dot3(a, b):
    ah, al = _split_bf16(a)
    bh, bl = _split_bf16(b)
    return _mm(ah, bh) + _mm(ah, bl) + _mm(al, bh)


def _dot2(a, b_bf16):
    ah, al = _split_bf16(a)
    return _mm(ah, b_bf16) + _mm(al, b_bf16)


def _silu(x):
    return x * jax.nn.sigmoid(x)


def _run_lockstep(gens):
    last = [None] * len(gens)
    live = list(range(len(gens)))
    while live:
        nxt = []
        for i in live:
            try:
                last[i] = next(gens[i])
                nxt.append(i)
            except StopIteration:
                pass
        live = nxt
    return last


def _head_sums(x, e2):
    parts = [_dot2(x[:, c * LANES:(c + 1) * LANES], e2) for c in range(x.shape[1] // LANES)]
    return jnp.concatenate(parts, axis=1)


def _ada_kernel(c_ref, w_ref, b_ref, o_ref):
    o_ref[...] = _dot3(_silu(c_ref[...]), w_ref[...]) + b_ref[...]


def _ada(cond8, w, b, layer):
    _, d, n = w.shape
    tn = 512
    return pl.pallas_call(
        _ada_kernel,
        grid=(n // tn,),
        in_specs=[pl.BlockSpec((SUBLANES, d), lambda j: (0, 0)),
                  pl.BlockSpec((None, d, tn), lambda j: (layer, 0, j)),
                  pl.BlockSpec((None, 1, tn), lambda j: (layer, 0, j))],
        out_specs=pl.BlockSpec((SUBLANES, tn), lambda j: (0, j)),
        out_shape=jax.ShapeDtypeStruct((SUBLANES, n), F32),
        compiler_params=_cparams(("arbitrary",)),
        name="ada_mod",
    )(cond8, w, b)


def _inproj_kernel(x_ref, mod_ref, nw_ref, w_ref, o_ref, h_scr):
    @pl.when(pl.program_id(1) == 0)
    def _():
        x = x_ref[...]
        y = x * lax.rsqrt(jnp.mean(x * x, axis=-1, keepdims=True) + EPS) * nw_ref[...]
        h = y * (1.0 + mod_ref[0, 1:2, :]) + mod_ref[0, 0:1, :]
        h_scr[...] = h.astype(BF16)

    o_ref[...] = _mm(h_scr[...], w_ref[...]).astype(o_ref.dtype)


def _inproj(x, mod, norm_w, w_bf16, cond_of_tile, tm, tn):
    t, d = x.shape
    n = w_bf16.shape[1]
    return pl.pallas_call(
        _inproj_kernel,
        grid=(t // tm, n // tn),
        in_specs=[pl.BlockSpec((tm, d), lambda i, j: (i, 0)),
                  pl.BlockSpec((1, 3, d), lambda i, j: (cond_of_tile(i), 0, 0)),
                  pl.BlockSpec((1, d), lambda i, j: (0, 0)),
                  pl.BlockSpec((d, tn), lambda i, j: (0, j))],
        out_specs=pl.BlockSpec((tm, tn), lambda i, j: (i, j)),
        out_shape=jax.ShapeDtypeStruct((t, n), BF16),
        scratch_shapes=[pltpu.VMEM((tm, d), BF16)],
        compiler_params=_cparams(("parallel", "arbitrary")),
        name="inproj",
    )(x, mod, norm_w, w_bf16)


def _rope_tab_kernel(inv_ref, c_ref, s1_ref, s2_ref, *, tr):
    t = lax.broadcasted_iota(jnp.int32, (tr, LANES), 0) + pl.program_id(0) * tr
    lane = lax.broadcasted_iota(jnp.int32, (tr, LANES), 1)
    row = (t // GRID_W).astype(F32)
    col = (t % GRID_W).astype(F32)
    pos = jnp.where((lane % DH_A) >= DH_A // 2, col, row)
    ang = pos * inv_ref[...]
    c = jnp.cos(ang)
    s = jnp.sin(ang)
    first = (lane % (DH_A // 2)) < DH_A // 4
    c_ref[...] = c
    s1_ref[...] = jnp.where(first, -s, 0.0)
    s2_ref[...] = jnp.where(first, 0.0, s)


def _rope_tables(ls):
    tr = min(ls, 512)
    quarter = DH_A // 4
    inv = ROPE_BASE ** (-jnp.arange(quarter, dtype=F32) / quarter)
    inv = jnp.tile(inv, LANES // quarter)[None, :]
    shp = jax.ShapeDtypeStruct((ls, LANES), F32)
    spec = pl.BlockSpec((tr, LANES), lambda i: (i, 0))
    return pl.pallas_call(
        functools.partial(_rope_tab_kernel, tr=tr),
        grid=(ls // tr,),
        in_specs=[pl.BlockSpec((1, LANES), lambda i: (0, 0))],
        out_specs=[spec, spec, spec],
        out_shape=[shp, shp, shp],
        compiler_params=_cparams(("arbitrary",)),
        name="rope_tables",
    )(inv)


def _rope_kernel(x_ref, c_ref, s1_ref, s2_ref, o_ref, *, n_q):
    quarter = DH_A // 4
    c, s1, s2 = c_ref[...], s1_ref[...], s2_ref[...]
    for j in range(x_ref.shape[1] // LANES):
        x = x_ref[:, j * LANES:(j + 1) * LANES].astype(F32)
        y = x * c + pltpu.roll(x, LANES - quarter, 1) * s1 + pltpu.roll(x, quarter, 1) * s2
        if j < n_q:
            y = y * Q_SCALE
        o_ref[:, j * LANES:(j + 1) * LANES] = y.astype(BF16)


def _rope_qk(p, tabs, row0, ts, ls, col0, n_heads):
    tr = min(ls, 512)
    width = 2 * n_heads * LANES
    tab_spec = pl.BlockSpec((tr, LANES), lambda i: (i % (ls // tr), 0))
    return pl.pallas_call(
        functools.partial(_rope_kernel, n_q=n_heads),
        grid=(ts // tr,),
        in_specs=[pl.BlockSpec((tr, width), lambda i: (row0 // tr + i, col0 // width)),
                  tab_spec, tab_spec, tab_spec],
        out_specs=pl.BlockSpec((tr, width), lambda i: (i, 0)),
        out_shape=jax.ShapeDtypeStruct((ts, width), BF16),
        compiler_params=_cparams(("parallel",)),
        name="rope_qk",
    )(p, *tabs)


ATTN_SUB = 128
ATTN_CHAINS = 4


def _attn_kernel(lam_ref, sw_ref, q_ref, k_ref, v_ref, g_ref, *rest, lam_init, q_scale, has_cache):
    if has_cache:
        kc_ref, vc_ref, o_ref = rest
    else:
        (o_ref,) = rest
    lp = lam_ref[...]
    lam = (jnp.exp(jnp.sum(lp[0:1, :] * lp[1:2, :], axis=1, keepdims=True))
           - jnp.exp(jnp.sum(lp[2:3, :] * lp[3:4, :], axis=1, keepdims=True)) + lam_init)
    tq = q_ref.shape[0]
    rs = min(ATTN_SUB, tq)
    lane = lax.broadcasted_iota(jnp.int32, (rs, HEAD_A), 1)
    zero = jnp.zeros((rs, HEAD_A), BF16)

    def head_kv(hh):
        hs = slice(hh * HEAD_A, (hh + 1) * HEAD_A)
        k = k_ref[:, hs].astype(BF16)
        v = v_ref[:, hs].astype(BF16)
        if has_cache:
            k = jnp.concatenate([kc_ref[:, hs], k], axis=0)
            v = jnp.concatenate([vc_ref[:, hs], v], axis=0)
        return k, jnp.concatenate([v, jnp.ones_like(v)], axis=1)

    def logits(hh, i, k):
        q = q_ref[i * rs:(i + 1) * rs, hh * HEAD_A:(hh + 1) * HEAD_A]
        if q_scale != 1.0:
            q = q.astype(F32) * q_scale
        q = q.astype(BF16)
        lhs = jnp.concatenate([jnp.where(lane < DH_A, q, zero), jnp.where(lane < DH_A, zero, q)], axis=0)
        return lax.dot_general(lhs, k, _NT, preferred_element_type=F32)

    kv = {}

    def kv_of(hh):
        if hh not in kv:
            kv[hh] = head_kv(hh)
        return kv[hh]

    def chain(items):
        s = logits(*items[0], kv_of(items[0][0])[0])
        yield
        for idx, (hh, i) in enumerate(items):
            if idx + 1 < len(items):
                s_next = logits(*items[idx + 1], kv_of(items[idx + 1][0])[0])
                yield
            e = jnp.exp2(s - jnp.max(s, axis=-1, keepdims=True)).astype(BF16)
            oa = _mm(e, kv_of(hh)[1])
            yield
            on = oa[:, :HEAD_A] * (1.0 / oa[:, HEAD_A:HEAD_A + 1])
            o = on[:rs] - lam * on[rs:]
            o = o * lax.rsqrt(jnp.mean(o * o, axis=-1, keepdims=True) + EPS) * sw_ref[...] * (1.0 - lam_init)
            rows, hs = slice(i * rs, (i + 1) * rs), slice(hh * HEAD_A, (hh + 1) * HEAD_A)
            o_ref[rows, hs] = (o * _silu(g_ref[rows, hs].astype(F32))).astype(BF16)
            if idx + 1 < len(items):
                s = s_next

    n_heads_blk = q_ref.shape[1] // HEAD_A
    n_chains = min(n_heads_blk, ATTN_CHAINS)
    items = [(hh, i) for hh in range(n_heads_blk) for i in range(tq // rs)]
    per = len(items) // n_chains
    _run_lockstep([chain(items[c * per:(c + 1) * per]) for c in range(n_chains)])


def _attention(lam_p, subln_w, q_arr, q_map, k_arr, k_map, v_arr, v_map, g_arr, g_map, cache,
               n_b, n_heads, hps, lq, lk_new, tq, lam_init, q_scale):
    nq = lq // tq
    wide = hps * HEAD_A
    in_specs = [pl.BlockSpec(lam_p.shape, lambda b, h, i: (0, 0)),
                pl.BlockSpec((1, HEAD_A), lambda b, h, i: (0, 0)),
                pl.BlockSpec((tq, wide), q_map),
                pl.BlockSpec((lk_new, wide), k_map),
                pl.BlockSpec((lk_new, wide), v_map),
                pl.BlockSpec((tq, wide), g_map)]
    args = [lam_p, subln_w, q_arr, k_arr, v_arr, g_arr]
    if cache is not None:
        kc, vc, past = cache
        cspec = pl.BlockSpec((past, wide), lambda b, h, i: (b, h))
        in_specs += [cspec, cspec]
        args += [kc, vc]
    return pl.pallas_call(
        functools.partial(_attn_kernel, lam_init=lam_init, q_scale=q_scale, has_cache=cache is not None),
        grid=(n_b, n_heads // hps, nq),
        in_specs=in_specs,
        out_specs=pl.BlockSpec((tq, wide), lambda b, h, i: (b * nq + i, h)),
        out_shape=jax.ShapeDtypeStruct((n_b * lq, n_heads * HEAD_A), BF16),
        compiler_params=_cparams(("parallel", "parallel", "arbitrary")),
        name="diff_attn",
    )(*args)


def _conv_kernel(u_ref, up_ref, un_ref, g_ref, cw_ref, cb_ref, lw_ref, lb_ref, o_ref, zp_scr, acc_scr, zs_scr,
                 *, tt, n_prompt_blocks, blocks_per_prompt_seq, blocks_per_sample_seq):
    cb_ = u_ref.shape[1] // 2
    blk = pl.program_id(0)
    in_prompt = blk < n_prompt_blocks
    pos = jnp.where(in_prompt, blk % blocks_per_prompt_seq, (blk - n_prompt_blocks) % blocks_per_sample_seq)
    per_seq = jnp.where(in_prompt, blocks_per_prompt_seq, blocks_per_sample_seq)
    keep_prev = (pos > 0).astype(F32)
    keep_next = (pos < per_seq - 1).astype(F32)

    def glu(ref):
        u = ref[...].astype(F32)
        return u[:, :cb_] * jax.nn.sigmoid(u[:, cb_:])

    h = CONV_HALO
    zp_scr[pl.ds(0, h), :] = glu(up_ref) * keep_prev
    zp_scr[pl.ds(h, tt), :] = glu(u_ref)
    zp_scr[pl.ds(h + tt, h), :] = glu(un_ref) * keep_next

    rc = 64
    off = h - CONV_W // 2
    n_sh = zs_scr.shape[1]

    def lane_chunk(c, carry):
        ls = pl.ds(pl.multiple_of(c * LANES, LANES), LANES)
        w = cw_ref[:, ls]
        for s in range(SUBLANES):
            zs_scr[s] = zp_scr[pl.ds(s, n_sh), ls]
        for r0 in range(0, tt, rc):
            acc = jnp.zeros((rc, LANES), F32)
            for tau in range(CONV_W):
                o = tau + off
                acc = acc + zs_scr[o % SUBLANES, pl.ds(r0 + o - o % SUBLANES, rc), :] * w[tau:tau + 1, :]
            acc_scr[pl.ds(r0, rc), ls] = acc
        return carry

    lax.fori_loop(0, cb_ // LANES, lane_chunk, 0)

    z = acc_scr[...] + cb_ref[...]
    mu = jnp.mean(z, axis=-1, keepdims=True)
    d = z - mu
    var = jnp.mean(d * d, axis=-1, keepdims=True)
    zn = d * lax.rsqrt(var + LN_EPS) * lw_ref[...] + lb_ref[...]
    o_ref[...] = (_silu(zn) * _silu(g_ref[...].astype(F32))).astype(BF16)


def _conv(p, conv_w, conv_b, ln_w, ln_b, glu_col0, gb_col0, tt, n_prompt_blocks,
          blocks_per_prompt_seq, blocks_per_sample_seq):
    t = p.shape[0]
    cb_ = conv_w.shape[1]
    nblk = t // tt
    hb = tt // CONV_HALO
    n_hblk = t // CONV_HALO
    vec = pl.BlockSpec((1, cb_), lambda i: (0, 0))
    return pl.pallas_call(
        functools.partial(_conv_kernel, tt=tt, n_prompt_blocks=n_prompt_blocks,
                          blocks_per_prompt_seq=blocks_per_prompt_seq,
                          blocks_per_sample_seq=blocks_per_sample_seq),
        grid=(nblk,),
        in_specs=[pl.BlockSpec((tt, 2 * cb_), lambda i: (i, glu_col0 // (2 * cb_))),
                  pl.BlockSpec((CONV_HALO, 2 * cb_), lambda i: (jnp.maximum(i * hb - 1, 0), glu_col0 // (2 * cb_))),
                  pl.BlockSpec((CONV_HALO, 2 * cb_),
                               lambda i: (jnp.minimum((i + 1) * hb, n_hblk - 1), glu_col0 // (2 * cb_))),
                  pl.BlockSpec((tt, cb_), lambda i: (i, gb_col0 // cb_)),
                  pl.BlockSpec(conv_w.shape, lambda i: (0, 0)),
                  vec, vec, vec],
        out_specs=pl.BlockSpec((tt, cb_), lambda i: (i, 0)),
        out_shape=jax.ShapeDtypeStruct((t, cb_), BF16),
        scratch_shapes=[pltpu.VMEM((tt + 2 * CONV_HALO, cb_), F32), pltpu.VMEM((tt, cb_), F32),
                        pltpu.VMEM((SUBLANES, tt + 2 * CONV_HALO - SUBLANES, LANES), F32)],
        compiler_params=_cparams(("parallel",)),
        name="conformer_conv",
    )(p, p, p, p, conv_w, conv_b, ln_w, ln_b)


def _rwkv_prep_kernel(k_ref, xw_ref, xa_ref, w0_ref, wup_ref, a0_ref, aup_ref, kk_ref, ka_ref, e2_ref,
                      okk_ref, olw_ref, okd_ref, obd_ref):
    k = k_ref[...].astype(F32)
    e2 = e2_ref[...]
    kkr = k * kk_ref[...]
    kk = kkr / jnp.maximum(jnp.sqrt(_head_sums(kkr * kkr, e2)), 1e-12)
    okk_ref[...] = kk
    txw = jnp.tanh(xw_ref[...].astype(F32))
    xa = xa_ref[...].astype(F32)
    for d in range(2):
        olw_ref[d] = -math.exp(-0.5) * jax.nn.sigmoid(w0_ref[d:d + 1, :] + _dot3(txw, wup_ref[d]))
        a = jax.nn.sigmoid(a0_ref[d:d + 1, :] + _dot3(xa, aup_ref[d]))
        okd_ref[d] = k * (1.0 + (a - 1.0) * ka_ref[...])
        obd_ref[d] = kk * a


def _rwkv_prep(p, kc_col0, xw_col0, xa_col0, w0, wup_pad, a0, aup_pad, k_k, k_a, e2, tm):
    t = p.shape[0]
    cc = w0.shape[1]
    vec = pl.BlockSpec((1, cc), lambda i: (0, 0))
    vec2 = pl.BlockSpec((2, cc), lambda i: (0, 0))
    mat = pl.BlockSpec((2, LANES, cc), lambda i: (0, 0, 0))
    one = jax.ShapeDtypeStruct((t, cc), F32)
    two = jax.ShapeDtypeStruct((2, t, cc), F32)
    two_spec = pl.BlockSpec((2, tm, cc), lambda i: (0, i, 0))
    return pl.pallas_call(
        _rwkv_prep_kernel,
        grid=(t // tm,),
        in_specs=[pl.BlockSpec((tm, cc), lambda i: (i, kc_col0 // cc)),
                  pl.BlockSpec((tm, LANES), lambda i: (i, xw_col0 // LANES)),
                  pl.BlockSpec((tm, LANES), lambda i: (i, xa_col0 // LANES)),
                  vec2, mat, vec2, mat, vec, vec,
                  pl.BlockSpec((LANES, LANES), lambda i: (0, 0))],
        out_specs=[pl.BlockSpec((tm, cc), lambda i: (i, 0)), two_spec, two_spec, two_spec],
        out_shape=[one, two, two, two],
        compiler_params=_cparams(("parallel",)),
        name="rwkv_prep",
    )(p, p, p, w0, wup_pad, a0, aup_pad, k_k, k_a, e2)


CHUNK = 64
PAIRS_PER_BODY = 8


def _mmb(a, b):
    return _mm(a.astype(BF16), b.astype(BF16))


def _chunk_unit(r, v, kk, lw, c, kd, bd, s0, reverse, lane_lo, eye2, strict, incl, same_head):
    w2 = 2 * CHUNK
    ctot = c[0:1, :] if reverse else c[CHUNK - 1:CHUNK, :]
    g_inv = jnp.exp(-c)
    g_rem = jnp.exp(ctot - c)
    rt = r * jnp.exp(c)
    bt = -(kk * jnp.exp(c - lw))

    def stack(x):
        xb = x.astype(BF16)
        z = jnp.zeros_like(xb)
        return jnp.concatenate([jnp.where(lane_lo, xb, z), jnp.where(lane_lo, z, xb)], axis=0)

    v_s = stack(v)
    s0b = s0.astype(BF16)
    lhs = jnp.concatenate([bt, rt], axis=0).astype(BF16)
    rhs = jnp.concatenate([stack(bd * g_inv), stack(kd * g_inv)], axis=0)
    nn = lax.dot_general(lhs, rhs, _NT, preferred_element_type=F32)
    yield None
    a_ab = jnp.where(strict, nn[:CHUNK, :w2], 0.0)
    a_bk = jnp.where(strict, nn[:CHUNK, w2:], 0.0)
    a_r = jnp.concatenate([jnp.where(incl, nn[CHUNK:, :w2], 0.0), jnp.where(incl, nn[CHUNK:, w2:], 0.0)], axis=1)
    p1 = _mm(a_bk.astype(BF16), v_s)
    t = eye2 + a_ab
    n = _mm(a_ab.astype(BF16), stack(a_ab))
    yield None
    for _ in range(int(math.log2(CHUNK)) - 2):
        prod = _mm(jnp.concatenate([n, t], axis=0).astype(BF16), stack(n))
        yield None
        t = t + prod[CHUNK:]
        n = prod[:CHUNK]
    t = t + _mm(t.astype(BF16), stack(n))
    yield None
    wx = _mm(t.astype(BF16), jnp.concatenate([stack(bt), stack(p1)], axis=1))
    yield None
    w, x = wx[:, :w2], wx[:, w2:]
    rhs_big = jnp.concatenate(
        [jnp.concatenate([stack(w), stack(x)], axis=1),
         jnp.concatenate([jnp.zeros((w2, w2), BF16), v_s], axis=1)], axis=0)
    qy = _mm(a_r.astype(BF16), rhs_big)
    w_pad = jnp.concatenate([w, jnp.zeros_like(w)], axis=0).astype(BF16)
    uv_t = jnp.concatenate([x, v], axis=0).T + lax.dot_general(s0b, w_pad, _NT, preferred_element_type=F32)
    yield None
    qhat = rt + qy[:, :w2]
    y = lax.dot_general(qhat.astype(BF16), s0b, _NT, preferred_element_type=F32) + qy[:, w2:]
    upd = _mmb(uv_t, jnp.concatenate([bd * g_rem, kd * g_rem], axis=0))
    yield s0 * jnp.exp(ctot) + jnp.where(same_head, upd, 0.0), y


def _rwkv_chunk_kernel(rf_ref, vf_ref, kkf_ref, lwf_ref, kdf_ref, bdf_ref,
                       rb_ref, vb_ref, kkb_ref, lwb_ref, kdb_ref, bdb_ref, *rest, has_init):
    if has_init:
        h0_ref, yf_ref, yb_ref, hfin_ref, h_scr, c_scr = rest
    else:
        yf_ref, yb_ref, hfin_ref, h_scr, c_scr = rest

    @pl.when(pl.program_id(1) == 0)
    def _():
        h_scr[...] = h0_ref[0] if has_init else jnp.zeros(h_scr.shape, F32)

    ti = lax.broadcasted_iota(jnp.int32, (CHUNK, CHUNK), 0)
    si = lax.broadcasted_iota(jnp.int32, (CHUNK, CHUNK), 1)

    def cumsum(tri, x):
        hi = x.astype(BF16)
        r1 = x - hi.astype(F32)
        mid = r1.astype(BF16)
        lo = (r1 - mid.astype(F32)).astype(BF16)
        return _mm(tri, hi) + _mm(tri, mid) + _mm(tri, lo)

    c_scr[0] = cumsum((si <= ti).astype(BF16), lwf_ref[...])
    c_scr[1] = cumsum((si >= ti).astype(BF16), lwb_ref[...])

    w2 = 2 * CHUNK
    tpos = lax.broadcasted_iota(jnp.int32, (CHUNK, w2), 0)
    lane = lax.broadcasted_iota(jnp.int32, (CHUNK, w2), 1)
    spos = lane % CHUNK
    eye2 = (spos == tpos).astype(F32)
    lane_lo = lane < N_C
    same_head = (lax.broadcasted_iota(jnp.int32, (w2, w2), 0) // N_C
                 == lax.broadcasted_iota(jnp.int32, (w2, w2), 1) // N_C)
    dirs = ((rf_ref, vf_ref, kkf_ref, lwf_ref, kdf_ref, bdf_ref, yf_ref, spos < tpos, spos <= tpos),
            (rb_ref, vb_ref, kkb_ref, lwb_ref, kdb_ref, bdb_ref, yb_ref, spos > tpos, spos >= tpos))

    def pair_group(pg, carry):
        units = []
        for pi in range(PAIRS_PER_BODY):
            p = pg * PAIRS_PER_BODY + pi
            ls = pl.ds(pl.multiple_of(p * w2, w2), w2)
            for d, (r_ref, v_ref, kk_ref, lw_ref, kd_ref, bd_ref, y_ref, strict, incl) in enumerate(dirs):
                args = (r_ref[:, ls].astype(F32), v_ref[:, ls].astype(F32), kk_ref[:, ls], lw_ref[:, ls], c_scr[d, :, ls],
                        kd_ref[:, ls], bd_ref[:, ls], h_scr[d, p])
                units.append((d, p, ls, y_ref, args, strict, incl))
        results = _run_lockstep([_chunk_unit(*args, d == 1, lane_lo, eye2, strict, incl, same_head)
                                 for d, p, ls, y_ref, args, strict, incl in units])
        for (d, p, ls, y_ref, _, _, _), (h_new, y) in zip(units, results):
            h_scr[d, p] = h_new
            y_ref[:, ls] = y
        return carry

    lax.fori_loop(0, rf_ref.shape[1] // (w2 * PAIRS_PER_BODY), pair_group, 0)

    @pl.when(pl.program_id(1) == pl.num_programs(1) - 1)
    def _():
        hfin_ref[0] = h_scr[...]


def _rwkv_chunks(p, kk, lw, kd, bd, h0, row0, nseq, l, rc_col0, vc_col0):
    cc = kk.shape[1]
    nc = l // CHUNK
    rb0 = row0 // CHUNK
    npair = cc // (2 * N_C)

    def fwd(s, g):
        return rb0 + s * nc + g

    def bwd(s, g):
        return rb0 + s * nc + nc - 1 - g

    def specs(rowf, d):
        return [pl.BlockSpec((CHUNK, cc), lambda s, g: (rowf(s, g), rc_col0 // cc)),
                pl.BlockSpec((CHUNK, cc), lambda s, g: (rowf(s, g), vc_col0 // cc)),
                pl.BlockSpec((CHUNK, cc), lambda s, g: (rowf(s, g), 0)),
                pl.BlockSpec((None, CHUNK, cc), lambda s, g: (d, rowf(s, g), 0)),
                pl.BlockSpec((None, CHUNK, cc), lambda s, g: (d, rowf(s, g), 0)),
                pl.BlockSpec((None, CHUNK, cc), lambda s, g: (d, rowf(s, g), 0))]

    hshape = (nseq, 2, npair, 2 * N_C, 2 * N_C)
    hspec = pl.BlockSpec((1,) + hshape[1:], lambda s, g: (s, 0, 0, 0, 0))
    yshape = jax.ShapeDtypeStruct((nseq * l, cc), F32)
    init = [] if h0 is None else [h0]
    return pl.pallas_call(
        functools.partial(_rwkv_chunk_kernel, has_init=h0 is not None),
        grid=(nseq, nc),
        in_specs=specs(fwd, 0) + specs(bwd, 1) + [hspec] * len(init),
        out_specs=[pl.BlockSpec((CHUNK, cc), lambda s, g: (s * nc + g, 0)),
                   pl.BlockSpec((CHUNK, cc), lambda s, g: (s * nc + nc - 1 - g, 0)),
                   hspec],
        out_shape=[yshape, yshape, jax.ShapeDtypeStruct(hshape, F32)],
        scratch_shapes=[pltpu.VMEM(hshape[1:], F32), pltpu.VMEM((2, CHUNK, cc), F32)],
        compiler_params=_cparams(("parallel", "arbitrary")),
        name="rwkv_chunks",
    )(p, p, kk, lw, kd, bd, p, p, kk, lw, kd, bd, *init)


def _state_to_wide(s):
    nseq, _, nh, n, _ = s.shape
    ht = s.reshape(nseq, 2, nh // 2, 2, n, n)
    z = jnp.zeros_like(ht[:, :, :, 0])
    top = jnp.concatenate([ht[:, :, :, 0], z], axis=-1)
    bot = jnp.concatenate([z, ht[:, :, :, 1]], axis=-1)
    return jnp.concatenate([top, bot], axis=-2)


def _wide_to_state(h):
    nseq, _, npair, w2, _ = h.shape
    n = w2 // 2
    blocks = jnp.stack([h[..., :n, :n], h[..., n:, n:]], axis=3)
    return blocks.reshape(nseq, 2, 2 * npair, n, n)


def _rwkv_post_kernel(yf_ref, yb_ref, r_ref, v_ref, kd_ref, g_ref, rk_ref, gw_ref, gb_ref, e2_ref, o_ref):
    e2 = e2_ref[...]
    y = yf_ref[...] + yb_ref[...]
    inv_n = 1.0 / N_C
    d = y - _head_sums(y, e2) * inv_n
    var = _head_sums(d * d, e2) * inv_n
    yn = d * lax.rsqrt(var + GN_EPS) * gw_ref[...] + gb_ref[...]
    bonus = (_head_sums(r_ref[...].astype(F32) * (kd_ref[0] + kd_ref[1]) * rk_ref[...], e2)
             * v_ref[...].astype(F32))
    o_ref[...] = ((yn + bonus) * _silu(g_ref[...].astype(F32))).astype(BF16)


def _rwkv_post(yf, yb, p, kd, row0, rc_col0, vc_col0, gc_col0, r_k, gn_w, gn_b, e2, tm):
    t, cc = yf.shape
    rb0 = row0 // tm
    vec = pl.BlockSpec((1, cc), lambda i: (0, 0))
    own = pl.BlockSpec((tm, cc), lambda i: (i, 0))
    return pl.pallas_call(
        _rwkv_post_kernel,
        grid=(t // tm,),
        in_specs=[own, own,
                  pl.BlockSpec((tm, cc), lambda i: (rb0 + i, rc_col0 // cc)),
                  pl.BlockSpec((tm, cc), lambda i: (rb0 + i, vc_col0 // cc)),
                  pl.BlockSpec((2, tm, cc), lambda i: (0, rb0 + i, 0)),
                  pl.BlockSpec((tm, cc), lambda i: (rb0 + i, gc_col0 // cc)),
                  vec, vec, vec,
                  pl.BlockSpec((LANES, LANES), lambda i: (0, 0))],
        out_specs=own,
        out_shape=jax.ShapeDtypeStruct((t, cc), BF16),
        compiler_params=_cparams(("parallel",)),
        name="rwkv_post",
    )(yf, yb, p, p, kd, p, r_k, gn_w, gn_b, e2)


def _merge_kernel(bap_ref, bas_ref, bb_ref, bcp_ref, bcs_ref, m0_ref, m1_ref, m2_ref, wa_ref, wb_ref, wc_ref, o_ref,
                  *, n_prompt_tiles):
    in_prompt = pl.program_id(0) < n_prompt_tiles
    ba = jnp.where(in_prompt, bap_ref[...], bas_ref[...])
    bc = jnp.where(in_prompt, bcp_ref[...], bcs_ref[...])
    o_ref[...] = (jax.nn.sigmoid(m0_ref[...].astype(F32)) * _mm(ba, wa_ref[...])
                  + jax.nn.sigmoid(m1_ref[...].astype(F32)) * _mm(bb_ref[...], wb_ref[...])
                  + jax.nn.sigmoid(m2_ref[...].astype(F32)) * _mm(bc, wc_ref[...])).astype(BF16)


def _merge(ba_p, ba_s, bb, bc_p, bc_s, p, mg_col0, wa, wb, wc, tm, tn):
    t, wbr = bb.shape
    d = wa.shape[1]
    nj = d // tn
    j0 = mg_col0 // tn
    npt = ba_p.shape[0] // tm
    prm = pl.BlockSpec((tm, wbr), lambda i, j: (jnp.minimum(i, npt - 1), 0))
    smp = pl.BlockSpec((tm, wbr), lambda i, j: (jnp.maximum(i - npt, 0), 0))
    wspec = pl.BlockSpec((wbr, tn), lambda i, j: (0, j))
    return pl.pallas_call(
        functools.partial(_merge_kernel, n_prompt_tiles=npt),
        grid=(t // tm, nj),
        in_specs=[prm, smp, pl.BlockSpec((tm, wbr), lambda i, j: (i, 0)), prm, smp,
                  pl.BlockSpec((tm, tn), lambda i, j: (i, j0 + j)),
                  pl.BlockSpec((tm, tn), lambda i, j: (i, j0 + nj + j)),
                  pl.BlockSpec((tm, tn), lambda i, j: (i, j0 + 2 * nj + j)),
                  wspec, wspec, wspec],
        out_specs=pl.BlockSpec((tm, tn), lambda i, j: (i, j)),
        out_shape=jax.ShapeDtypeStruct((t, d), BF16),
        compiler_params=_cparams(("parallel", "arbitrary")),
        name="branch_merge",
    )(ba_p, ba_s, bb, bc_p, bc_s, p, p, p, wa, wb, wc)


def _resid_kernel(m_ref, wo_ref, x_ref, mod_ref, fw_ref, *o_refs, n_prompt_tiles):
    x = x_ref[...] + mod_ref[0, 2:3, :] * _mm(m_ref[...], wo_ref[...])
    if len(o_refs) == 1:
        o_refs[0][...] = x
        return
    x = x * lax.rsqrt(jnp.mean(x * x, axis=-1, keepdims=True) + EPS) * fw_ref[...]
    in_prompt = pl.program_id(0) < n_prompt_tiles

    @pl.when(in_prompt)
    def _():
        o_refs[0][...] = x

    @pl.when(jnp.logical_not(in_prompt))
    def _():
        o_refs[1][...] = x


def _out_proj(merged, wo, x, mod, final_w, cond_of_tile, tm, tp, final):
    t, d = x.shape
    npt = tp // tm
    row = pl.BlockSpec((tm, d), lambda i: (i, 0))
    if final:
        out_specs = [pl.BlockSpec((tm, d), lambda i: (jnp.minimum(i, npt - 1), 0)),
                     pl.BlockSpec((tm, d), lambda i: (jnp.maximum(i - npt, 0), 0))]
        out_shape = [jax.ShapeDtypeStruct((tp, d), F32), jax.ShapeDtypeStruct((t - tp, d), F32)]
    else:
        out_specs, out_shape = row, jax.ShapeDtypeStruct((t, d), F32)
    return pl.pallas_call(
        functools.partial(_resid_kernel, n_prompt_tiles=npt),
        grid=(t // tm,),
        in_specs=[row,
                  pl.BlockSpec((d, d), lambda i: (0, 0)),
                  row,
                  pl.BlockSpec((1, 3, d), lambda i: (cond_of_tile(i), 0, 0)),
                  pl.BlockSpec((1, d), lambda i: (0, 0))],
        out_specs=out_specs,
        out_shape=out_shape,
        compiler_params=_cparams(("arbitrary",)),
        name="out_proj",
    )(merged, wo, x, mod, final_w)


def _largest_tile(n, cap):
    t = cap
    while n % t:
        t //= 2
    return t


def kernel(x_prompt, x_sample, cache_k, cache_v, state_rwkv, c, c_ctx, w_ada, b_ada, norm_w, w_in,
           lambda_q1, lambda_k1, lambda_q2, lambda_k2, subln_w, conv_w, conv_b, conv_ln_w, conv_ln_b,
           rwkv_w0, rwkv_w_up, rwkv_a0, rwkv_a_up, rwkv_k_k, rwkv_k_a, rwkv_r_k, rwkv_gn_w, rwkv_gn_b,
           w_br_a, w_br_b, w_br_c, w_out, final_norm_w):
    bp, lp_, d = x_prompt.shape
    bs, ls, _ = x_sample.shape
    depth = w_ada.shape[0]
    past = cache_k.shape[2]
    tp, ts = bp * lp_, bs * ls
    n_ha = d // 256
    w_a = n_ha * HEAD_A
    c_b = d // 2
    n_hc = d // (2 * N_C)
    c_c = n_hc * N_C
    n_mg = 3 * d
    assert n_ha * 2 * DH_A == w_a and w_a == c_b == c_c
    assert bs + 1 <= SUBLANES

    col = {}
    off = 0
    for name, width in (("q", w_a), ("k", w_a), ("v", w_a), ("g_a", w_a), ("glu", 2 * c_b),
                        ("g_b", c_b), ("r_c", c_c), ("k_c", c_c), ("v_c", c_c), ("g_c", c_c),
                        ("xw", 2 * LORA), ("xa", 2 * LORA), ("mg", n_mg)):
        col[name] = off
        off += width
    in_cols = off

    tm_in = _largest_tile(math.gcd(tp, ls), 1024)
    tm_out = _largest_tile(math.gcd(tp, ls), 512)
    tn_in = 768 if in_cols % 768 == 0 else 256
    tm_mg = _largest_tile(tp + ts, 1024)
    tn_mg = math.gcd(col["mg"], 512)
    tt = _largest_tile(math.gcd(lp_, ls), 256)
    tm_rw = _largest_tile(math.gcd(tp, ts), 256)

    def cond_of(tm):
        npt = tp // tm
        return lambda i: jnp.where(i < npt, 0, 1 + ((i - npt) * tm) // ls)

    x = jnp.concatenate([x_prompt.reshape(tp, d), x_sample.reshape(ts, d)], axis=0)
    cond8 = jnp.zeros((SUBLANES, d), F32).at[0].set(c_ctx).at[1:1 + bs].set(c)
    rope_tabs = _rope_tables(ls)
    blk = jnp.arange(LANES) // N_C
    e2 = (blk[:, None] == blk[None, :]).astype(BF16)
    zpad = jnp.zeros((LORA, c_c), F32)

    ks, vs, ss = [], [], []
    for l in range(depth):
        lam_init = 0.8 - 0.6 * math.exp(-0.3 * l)
        mod = _ada(cond8, w_ada, b_ada[:, None, :], l).reshape(SUBLANES, 3, d)
        p = _inproj(x, mod, norm_w[l][None, :], w_in[l].astype(BF16), cond_of(tm_in), tm_in, tn_in)

        lam_p = jnp.stack([lambda_q1[l], lambda_k1[l], lambda_q2[l], lambda_k2[l]], axis=0)
        sw = subln_w[l][None, :]
        cq, ck, cv, cg = (col[n] // w_a for n in ("q", "k", "v", "g_a"))
        tq_p = _largest_tile(lp_, 256)
        npq = lp_ // tq_p
        ba_p = _attention(
            lam_p, sw,
            p, lambda b, h, i: (b * npq + i, cq),
            p, lambda b, h, i: (b, ck),
            p, lambda b, h, i: (b, cv),
            p, lambda b, h, i: (b * npq + i, cg),
            None, bp, n_ha, n_ha, lp_, lp_, tq_p, lam_init, Q_SCALE)
        qk_rot = _rope_qk(p, rope_tabs, tp, ts, ls, col["q"], n_ha)
        cache = (cache_k[:, l].reshape(bs * past, w_a).astype(BF16),
                 cache_v[:, l].reshape(bs * past, w_a).astype(BF16), past)
        cv1, cg1 = col["v"] // HEAD_A, col["g_a"] // HEAD_A
        tq_s = _largest_tile(ls, 1024)
        nsq = ls // tq_s
        ba_s = _attention(
            lam_p, sw,
            qk_rot, lambda b, h, i: (b * nsq + i, h),
            qk_rot, lambda b, h, i: (b, n_ha + h),
            p, lambda b, h, i: (tp // ls + b, cv1 + h),
            p, lambda b, h, i: (tp // tq_s + b * nsq + i, cg1 + h),
            cache, bs, n_ha, 1, ls, ls, tq_s, lam_init, 1.0)

        bb = _conv(p, conv_w[l], conv_b[l][None, :], conv_ln_w[l][None, :], conv_ln_b[l][None, :],
                   col["glu"], col["g_b"], tt, tp // tt, lp_ // tt, ls // tt)

        wup_pad = jnp.stack([jnp.concatenate([rwkv_w_up[l, 0], zpad], 0), jnp.concatenate([zpad, rwkv_w_up[l, 1]], 0)])
        aup_pad = jnp.stack([jnp.concatenate([rwkv_a_up[l, 0], zpad], 0), jnp.concatenate([zpad, rwkv_a_up[l, 1]], 0)])
        kk, lw, kd, bd = _rwkv_prep(p, col["k_c"], col["xw"], col["xa"], rwkv_w0[l], wup_pad, rwkv_a0[l], aup_pad,
                                    rwkv_k_k[l][None, :], rwkv_k_a[l][None, :], e2, tm_rw)
        yf_p, yb_p, hfin_p = _rwkv_chunks(p, kk, lw, kd, bd, None, 0, bp, lp_, col["r_c"], col["v_c"])
        ss.append(_wide_to_state(hfin_p))
        yf_s, yb_s, _ = _rwkv_chunks(p, kk, lw, kd, bd, _state_to_wide(state_rwkv[:, l]), tp, bs, ls,
                                     col["r_c"], col["v_c"])
        post_args = (col["r_c"], col["v_c"], col["g_c"], rwkv_r_k[l].reshape(1, c_c),
                     rwkv_gn_w[l][None, :], rwkv_gn_b[l][None, :], e2, tm_rw)
        bc_p = _rwkv_post(yf_p, yb_p, p, kd, 0, *post_args)
        bc_s = _rwkv_post(yf_s, yb_s, p, kd, tp, *post_args)

        merged = _merge(ba_p, ba_s, bb, bc_p, bc_s, p, col["mg"], w_br_a[l].astype(BF16), w_br_b[l].astype(BF16), w_br_c[l].astype(BF16),
                        tm_mg, tn_mg)
        x = _out_proj(merged, w_out[l].astype(BF16), x, mod, final_norm_w[None, :], cond_of(tm_out), tm_out, tp,
                      final=(l == depth - 1))

        ks.append(p[:tp, col["k"]:col["k"] + w_a].astype(F32).reshape(bp, lp_, n_ha, 2, DH_A))
        vs.append(p[:tp, col["v"]:col["v"] + w_a].astype(F32).reshape(bp, lp_, n_ha, HEAD_A))

    y_prompt = x[0].reshape(bp, lp_, d)
    y_sample = x[1].reshape(bs, ls, d)
    return (y_prompt, y_sample, jnp.stack(ks, axis=1), jnp.stack(vs, axis=1), jnp.stack(ss, axis=1))
```

```python
import functools
import math

import jax
import jax.numpy as jnp
from jax import lax
from jax.experimental import pallas as pl
from jax.experimental.pallas import tpu as pltpu

F32 = jnp.float32
BF16 = jnp.bfloat16

HEAD_A = 128
DH_A = 64
CONV_W = 31
CONV_HALO = 16
N_C = 64
LORA = 64
GRID_W = 64
ROPE_BASE = 10000.0
EPS = 1e-6
LN_EPS = 1e-5
GN_EPS = 64e-5
Q_SCALE = DH_A ** -0.5 * math.log2(math.e)

LANES = 128
SUBLANES = 8
MIB = 1024 * 1024


def _cparams(semantics, vmem_mib=48):
    return pltpu.CompilerParams(dimension_semantics=semantics, vmem_limit_bytes=vmem_mib * MIB)


def _split_bf16(x):
    hi = x.astype(BF16)
    lo = (x - hi.astype(F32)).astype(BF16)
    return hi, lo


_NT = (((1,), (1,)), ((), ()))


def _mm(a, b):
    return jnp.dot(a, b, preferred_element_type=F32)


def _dot3(a, b):
    ah, al = _split_bf16(a)
    bh, bl = _split_bf16(b)
    return _mm(ah, bh) + _mm(ah, bl) + _mm(al, bh)


def _silu(x):
    return x * jax.nn.sigmoid(x)


def _run_lockstep(gens):
    last = [None] * len(gens)
    live = list(range(len(gens)))
    while live:
        nxt = []
        for i in live:
            try:
                last[i] = next(gens[i])
                nxt.append(i)
            except StopIteration:
                pass
        live = nxt
    return last


def _head_sums(x, e2):
    xb = x.astype(BF16)
    parts = [_mm(xb[:, c * LANES:(c + 1) * LANES], e2) for c in range(x.shape[1] // LANES)]
    return jnp.concatenate(parts, axis=1)


def _ada_kernel(c_ref, w_ref, b_ref, o_ref):
    o_ref[...] = _dot3(_silu(c_ref[...]), w_ref[...]) + b_ref[...]


def _ada(cond8, w, b, layer):
    _, d, n = w.shape
    tn = 512
    return pl.pallas_call(
        _ada_kernel,
        grid=(n // tn,),
        in_specs=[pl.BlockSpec((SUBLANES, d), lambda j: (0, 0)),
                  pl.BlockSpec((None, d, tn), lambda j: (layer, 0, j)),
                  pl.BlockSpec((None, 1, tn), lambda j: (layer, 0, j))],
        out_specs=pl.BlockSpec((SUBLANES, tn), lambda j: (0, j)),
        out_shape=jax.ShapeDtypeStruct((SUBLANES, n), F32),
        compiler_params=_cparams(("arbitrary",)),
        name="ada_mod",
    )(cond8, w, b)


def _inproj_kernel(x_ref, mod_ref, nw_ref, w_ref, o_ref, h_scr):
    @pl.when(pl.program_id(1) == 0)
    def _():
        x = x_ref[...]
        y = x * lax.rsqrt(jnp.mean(x * x, axis=-1, keepdims=True) + EPS) * nw_ref[...]
        h = y * (1.0 + mod_ref[0, 1:2, :]) + mod_ref[0, 0:1, :]
        h_scr[...] = h.astype(BF16)

    o_ref[...] = _mm(h_scr[...], w_ref[...]).astype(o_ref.dtype)


def _inproj(x, mod, norm_w, w_bf16, cond_of_tile, tm, tn):
    t, d = x.shape
    n = w_bf16.shape[1]
    return pl.pallas_call(
        _inproj_kernel,
        grid=(t // tm, n // tn),
        in_specs=[pl.BlockSpec((tm, d), lambda i, j: (i, 0)),
                  pl.BlockSpec((1, 3, d), lambda i, j: (cond_of_tile(i), 0, 0)),
                  pl.BlockSpec((1, d), lambda i, j: (0, 0)),
                  pl.BlockSpec((d, tn), lambda i, j: (0, j))],
        out_specs=pl.BlockSpec((tm, tn), lambda i, j: (i, j)),
        out_shape=jax.ShapeDtypeStruct((t, n), BF16),
        scratch_shapes=[pltpu.VMEM((tm, d), BF16)],
        compiler_params=_cparams(("parallel", "arbitrary")),
        name="inproj",
    )(x, mod, norm_w, w_bf16)


def _rope_tab_kernel(inv_ref, c_ref, s1_ref, s2_ref, *, tr):
    t = lax.broadcasted_iota(jnp.int32, (tr, LANES), 0) + pl.program_id(0) * tr
    lane = lax.broadcasted_iota(jnp.int32, (tr, LANES), 1)
    row = (t // GRID_W).astype(F32)
    col = (t % GRID_W).astype(F32)
    pos = jnp.where((lane % DH_A) >= DH_A // 2, col, row)
    ang = pos * inv_ref[...]
    c = jnp.cos(ang)
    s = jnp.sin(ang)
    first = (lane % (DH_A // 2)) < DH_A // 4
    c_ref[...] = c
    s1_ref[...] = jnp.where(first, -s, 0.0)
    s2_ref[...] = jnp.where(first, 0.0, s)


def _rope_tables(ls):
    tr = min(ls, 512)
    quarter = DH_A // 4
    inv = ROPE_BASE ** (-jnp.arange(quarter, dtype=F32) / quarter)
    inv = jnp.tile(inv, LANES // quarter)[None, :]
    shp = jax.ShapeDtypeStruct((ls, LANES), F32)
    spec = pl.BlockSpec((tr, LANES), lambda i: (i, 0))
    return pl.pallas_call(
        functools.partial(_rope_tab_kernel, tr=tr),
        grid=(ls // tr,),
        in_specs=[pl.BlockSpec((1, LANES), lambda i: (0, 0))],
        out_specs=[spec, spec, spec],
        out_shape=[shp, shp, shp],
        compiler_params=_cparams(("arbitrary",)),
        name="rope_tables",
    )(inv)


def _rope_kernel(x_ref, c_ref, s1_ref, s2_ref, o_ref, *, n_q):
    quarter = DH_A // 4
    c, s1, s2 = c_ref[...], s1_ref[...], s2_ref[...]
    for j in range(x_ref.shape[1] // LANES):
        x = x_ref[:, j * LANES:(j + 1) * LANES].astype(F32)
        y = x * c + pltpu.roll(x, LANES - quarter, 1) * s1 + pltpu.roll(x, quarter, 1) * s2
        if j < n_q:
            y = y * Q_SCALE
        o_ref[:, j * LANES:(j + 1) * LANES] = y.astype(BF16)


def _rope_qk(p, tabs, row0, ts, ls, col0, n_heads):
    tr = min(ls, 512)
    width = 2 * n_heads * LANES
    tab_spec = pl.BlockSpec((tr, LANES), lambda i: (i % (ls // tr), 0))
    return pl.pallas_call(
        functools.partial(_rope_kernel, n_q=n_heads),
        grid=(ts // tr,),
        in_specs=[pl.BlockSpec((tr, width), lambda i: (row0 // tr + i, col0 // width)),
                  tab_spec, tab_spec, tab_spec],
        out_specs=pl.BlockSpec((tr, width), lambda i: (i, 0)),
        out_shape=jax.ShapeDtypeStruct((ts, width), BF16),
        compiler_params=_cparams(("parallel",)),
        name="rope_qk",
    )(p, *tabs)


ATTN_SUB = 128
ATTN_CHAINS = 4


def _attn_kernel(lam_ref, sw_ref, q_ref, k_ref, v_ref, g_ref, *rest, lam_init, q_scale, has_cache):
    if has_cache:
        kc_ref, vc_ref, o_ref = rest
    else:
        (o_ref,) = rest
    lp = lam_ref[...]
    lam = (jnp.exp(jnp.sum(lp[0:1, :] * lp[1:2, :], axis=1, keepdims=True))
           - jnp.exp(jnp.sum(lp[2:3, :] * lp[3:4, :], axis=1, keepdims=True)) + lam_init)
    tq = q_ref.shape[0]
    rs = min(ATTN_SUB, tq)
    lane = lax.broadcasted_iota(jnp.int32, (rs, HEAD_A), 1)
    zero = jnp.zeros((rs, HEAD_A), BF16)

    def head_kv(hh):
        hs = slice(hh * HEAD_A, (hh + 1) * HEAD_A)
        k = k_ref[:, hs].astype(BF16)
        v = v_ref[:, hs].astype(BF16)
        if has_cache:
            k = jnp.concatenate([kc_ref[:, hs], k], axis=0)
            v = jnp.concatenate([vc_ref[:, hs], v], axis=0)
        return k, jnp.concatenate([v, jnp.ones_like(v)], axis=1)

    def logits(hh, i, k):
        q = q_ref[i * rs:(i + 1) * rs, hh * HEAD_A:(hh + 1) * HEAD_A]
        if q_scale != 1.0:
            q = q.astype(F32) * q_scale
        q = q.astype(BF16)
        lhs = jnp.concatenate([jnp.where(lane < DH_A, q, zero), jnp.where(lane < DH_A, zero, q)], axis=0)
        return lax.dot_general(lhs, k, _NT, preferred_element_type=F32)

    kv = {}

    def kv_of(hh):
        if hh not in kv:
            kv[hh] = head_kv(hh)
        return kv[hh]

    def chain(items):
        s = logits(*items[0], kv_of(items[0][0])[0])
        yield
        for idx, (hh, i) in enumerate(items):
            if idx + 1 < len(items):
                s_next = logits(*items[idx + 1], kv_of(items[idx + 1][0])[0])
                yield
            e = jnp.exp2(s - jnp.max(s, axis=-1, keepdims=True)).astype(BF16)
            oa = _mm(e, kv_of(hh)[1])
            yield
            on = oa[:, :HEAD_A] * (1.0 / oa[:, HEAD_A:HEAD_A + 1])
            o = on[:rs] - lam * on[rs:]
            o = o * lax.rsqrt(jnp.mean(o * o, axis=-1, keepdims=True) + EPS) * sw_ref[...] * (1.0 - lam_init)
            rows, hs = slice(i * rs, (i + 1) * rs), slice(hh * HEAD_A, (hh + 1) * HEAD_A)
            o_ref[rows, hs] = (o * _silu(g_ref[rows, hs].astype(F32))).astype(BF16)
            if idx + 1 < len(items):
                s = s_next

    n_heads_blk = q_ref.shape[1] // HEAD_A
    n_chains = min(n_heads_blk, ATTN_CHAINS)
    items = [(hh, i) for hh in range(n_heads_blk) for i in range(tq // rs)]
    per = len(items) // n_chains
    _run_lockstep([chain(items[c * per:(c + 1) * per]) for c in range(n_chains)])


def _attention(lam_p, subln_w, q_arr, q_map, k_arr, k_map, v_arr, v_map, g_arr, g_map, cache,
               n_b, n_heads, hps, lq, lk_new, tq, lam_init, q_scale):
    nq = lq // tq
    wide = hps * HEAD_A
    in_specs = [pl.BlockSpec(lam_p.shape, lambda b, h, i: (0, 0)),
                pl.BlockSpec((1, HEAD_A), lambda b, h, i: (0, 0)),
                pl.BlockSpec((tq, wide), q_map),
                pl.BlockSpec((lk_new, wide), k_map),
                pl.BlockSpec((lk_new, wide), v_map),
                pl.BlockSpec((tq, wide), g_map)]
    args = [lam_p, subln_w, q_arr, k_arr, v_arr, g_arr]
    if cache is not None:
        kc, vc, past = cache
        cspec = pl.BlockSpec((past, wide), lambda b, h, i: (b, h))
        in_specs += [cspec, cspec]
        args += [kc, vc]
    return pl.pallas_call(
        functools.partial(_attn_kernel, lam_init=lam_init, q_scale=q_scale, has_cache=cache is not None),
        grid=(n_b, n_heads // hps, nq),
        in_specs=in_specs,
        out_specs=pl.BlockSpec((tq, wide), lambda b, h, i: (b * nq + i, h)),
        out_shape=jax.ShapeDtypeStruct((n_b * lq, n_heads * HEAD_A), BF16),
        compiler_params=_cparams(("parallel", "parallel", "arbitrary")),
        name="diff_attn",
    )(*args)


def _conv_kernel(u_ref, up_ref, un_ref, g_ref, cw_ref, cb_ref, lw_ref, lb_ref, o_ref, zp_scr, acc_scr, zs_scr,
                 *, tt, n_prompt_blocks, blocks_per_prompt_seq, blocks_per_sample_seq):
    cb_ = u_ref.shape[1] // 2
    blk = pl.program_id(0)
    in_prompt = blk < n_prompt_blocks
    pos = jnp.where(in_prompt, blk % blocks_per_prompt_seq, (blk - n_prompt_blocks) % blocks_per_sample_seq)
    per_seq = jnp.where(in_prompt, blocks_per_prompt_seq, blocks_per_sample_seq)
    keep_prev = (pos > 0).astype(F32)
    keep_next = (pos < per_seq - 1).astype(F32)

    def glu(ref):
        u = ref[...].astype(F32)
        return u[:, :cb_] * jax.nn.sigmoid(u[:, cb_:])

    h = CONV_HALO
    zp_scr[pl.ds(0, h), :] = glu(up_ref) * keep_prev
    zp_scr[pl.ds(h, tt), :] = glu(u_ref)
    zp_scr[pl.ds(h + tt, h), :] = glu(un_ref) * keep_next

    rc = 64
    off = h - CONV_W // 2
    n_sh = zs_scr.shape[1]

    def lane_chunk(c, carry):
        ls = pl.ds(pl.multiple_of(c * LANES, LANES), LANES)
        w = cw_ref[:, ls]
        for s in range(SUBLANES):
            zs_scr[s] = zp_scr[pl.ds(s, n_sh), ls]
        for r0 in range(0, tt, rc):
            acc = jnp.zeros((rc, LANES), F32)
            for tau in range(CONV_W):
                o = tau + off
                acc = acc + zs_scr[o % SUBLANES, pl.ds(r0 + o - o % SUBLANES, rc), :] * w[tau:tau + 1, :]
            acc_scr[pl.ds(r0, rc), ls] = acc
        return carry

    lax.fori_loop(0, cb_ // LANES, lane_chunk, 0)

    z = acc_scr[...] + cb_ref[...]
    mu = jnp.mean(z, axis=-1, keepdims=True)
    d = z - mu
    var = jnp.mean(d * d, axis=-1, keepdims=True)
    zn = d * lax.rsqrt(var + LN_EPS) * lw_ref[...] + lb_ref[...]
    o_ref[...] = (_silu(zn) * _silu(g_ref[...].astype(F32))).astype(BF16)


def _conv(p, conv_w, conv_b, ln_w, ln_b, glu_col0, gb_col0, tt, n_prompt_blocks,
          blocks_per_prompt_seq, blocks_per_sample_seq):
    t = p.shape[0]
    cb_ = conv_w.shape[1]
    nblk = t // tt
    hb = tt // CONV_HALO
    n_hblk = t // CONV_HALO
    vec = pl.BlockSpec((1, cb_), lambda i: (0, 0))
    return pl.pallas_call(
        functools.partial(_conv_kernel, tt=tt, n_prompt_blocks=n_prompt_blocks,
                          blocks_per_prompt_seq=blocks_per_prompt_seq,
                          blocks_per_sample_seq=blocks_per_sample_seq),
        grid=(nblk,),
        in_specs=[pl.BlockSpec((tt, 2 * cb_), lambda i: (i, glu_col0 // (2 * cb_))),
                  pl.BlockSpec((CONV_HALO, 2 * cb_), lambda i: (jnp.maximum(i * hb - 1, 0), glu_col0 // (2 * cb_))),
                  pl.BlockSpec((CONV_HALO, 2 * cb_),
                               lambda i: (jnp.minimum((i + 1) * hb, n_hblk - 1), glu_col0 // (2 * cb_))),
                  pl.BlockSpec((tt, cb_), lambda i: (i, gb_col0 // cb_)),
                  pl.BlockSpec(conv_w.shape, lambda i: (0, 0)),
                  vec, vec, vec],
        out_specs=pl.BlockSpec((tt, cb_), lambda i: (i, 0)),
        out_shape=jax.ShapeDtypeStruct((t, cb_), BF16),
        scratch_shapes=[pltpu.VMEM((tt + 2 * CONV_HALO, cb_), F32), pltpu.VMEM((tt, cb_), F32),
                        pltpu.VMEM((SUBLANES, tt + 2 * CONV_HALO - SUBLANES, LANES), F32)],
        compiler_params=_cparams(("parallel",)),
        name="conformer_conv",
    )(p, p, p, p, conv_w, conv_b, ln_w, ln_b)


def _rwkv_prep_kernel(k_ref, xw_ref, xa_ref, w0_ref, wup_ref, a0_ref, aup_ref, kk_ref, ka_ref, e2_ref,
                      okk_ref, olw_ref, okd_ref, obd_ref):
    k = k_ref[...].astype(F32)
    e2 = e2_ref[...]
    kkr = k * kk_ref[...]
    kk = kkr / jnp.maximum(jnp.sqrt(_head_sums(kkr * kkr, e2)), 1e-12)
    okk_ref[...] = kk
    txw = jnp.tanh(xw_ref[...].astype(F32))
    xa = xa_ref[...].astype(F32)
    for d in range(2):
        olw_ref[d] = -math.exp(-0.5) * jax.nn.sigmoid(w0_ref[d:d + 1, :] + _dot3(txw, wup_ref[d]))
        a = jax.nn.sigmoid(a0_ref[d:d + 1, :] + _dot3(xa, aup_ref[d]))
        okd_ref[d] = k * (1.0 + (a - 1.0) * ka_ref[...])
        obd_ref[d] = kk * a


def _rwkv_prep(p, kc_col0, xw_col0, xa_col0, w0, wup_pad, a0, aup_pad, k_k, k_a, e2, tm):
    t = p.shape[0]
    cc = w0.shape[1]
    vec = pl.BlockSpec((1, cc), lambda i: (0, 0))
    vec2 = pl.BlockSpec((2, cc), lambda i: (0, 0))
    mat = pl.BlockSpec((2, LANES, cc), lambda i: (0, 0, 0))
    one = jax.ShapeDtypeStruct((t, cc), F32)
    two = jax.ShapeDtypeStruct((2, t, cc), F32)
    two_spec = pl.BlockSpec((2, tm, cc), lambda i: (0, i, 0))
    return pl.pallas_call(
        _rwkv_prep_kernel,
        grid=(t // tm,),
        in_specs=[pl.BlockSpec((tm, cc), lambda i: (i, kc_col0 // cc)),
                  pl.BlockSpec((tm, LANES), lambda i: (i, xw_col0 // LANES)),
                  pl.BlockSpec((tm, LANES), lambda i: (i, xa_col0 // LANES)),
                  vec2, mat, vec2, mat, vec, vec,
                  pl.BlockSpec((LANES, LANES), lambda i: (0, 0))],
        out_specs=[pl.BlockSpec((tm, cc), lambda i: (i, 0)), two_spec, two_spec, two_spec],
        out_shape=[one, two, two, two],
        compiler_params=_cparams(("parallel",)),
        name="rwkv_prep",
    )(p, p, p, w0, wup_pad, a0, aup_pad, k_k, k_a, e2)


CHUNK = 64
HEADS_PER_UNIT = 2
UNIT_W = HEADS_PER_UNIT * N_C
UNITS_PER_BODY = 8


def _mmb(a, b):
    return _mm(a.astype(BF16), b.astype(BF16))


def _chunk_unit(r, v, kk, lw, c, kd, bd, s0, reverse, head_masks, eye2, strict, incl, same_head):
    w2 = UNIT_W
    ctot = c[0:1, :] if reverse else c[CHUNK - 1:CHUNK, :]
    g_inv = jnp.exp(-c)
    g_rem = jnp.exp(ctot - c)
    rt = r * jnp.exp(c)
    bt = -(kk * jnp.exp(c - lw))

    def stack(x):
        xb = x.astype(BF16)
        z = jnp.zeros_like(xb)
        return jnp.concatenate([jnp.where(m, xb, z) for m in head_masks], axis=0)

    v_s = stack(v)
    s0b = s0.astype(BF16)
    lhs = jnp.concatenate([bt, rt], axis=0).astype(BF16)
    rhs = jnp.concatenate([stack(bd * g_inv), stack(kd * g_inv)], axis=0)
    nn = lax.dot_general(lhs, rhs, _NT, preferred_element_type=F32)
    yield None
    a_ab = jnp.where(strict, nn[:CHUNK, :w2], 0.0)
    a_bk = jnp.where(strict, nn[:CHUNK, w2:], 0.0)
    a_r = jnp.concatenate([jnp.where(incl, nn[CHUNK:, :w2], 0.0), jnp.where(incl, nn[CHUNK:, w2:], 0.0)], axis=1)
    p1 = _mm(a_bk.astype(BF16), v_s)
    t = eye2 + a_ab
    n = _mm(a_ab.astype(BF16), stack(a_ab))
    yield None
    for _ in range(int(math.log2(CHUNK)) - 2):
        prod = _mm(jnp.concatenate([n, t], axis=0).astype(BF16), stack(n))
        yield None
        t = t + prod[CHUNK:]
        n = prod[:CHUNK]
    t = t + _mm(t.astype(BF16), stack(n))
    yield None
    wx = _mm(t.astype(BF16), jnp.concatenate([stack(bt), stack(p1)], axis=1))
    yield None
    w, x = wx[:, :w2], wx[:, w2:]
    rhs_big = jnp.concatenate(
        [jnp.concatenate([stack(w), stack(x)], axis=1),
         jnp.concatenate([jnp.zeros((w2, w2), BF16), v_s], axis=1)], axis=0)
    qy = _mm(a_r.astype(BF16), rhs_big)
    w_pad = jnp.concatenate([w, jnp.zeros_like(w)], axis=0).astype(BF16)
    uv_t = jnp.concatenate([x, v], axis=0).T + lax.dot_general(s0b, w_pad, _NT, preferred_element_type=F32)
    yield None
    qhat = rt + qy[:, :w2]
    y = lax.dot_general(qhat.astype(BF16), s0b, _NT, preferred_element_type=F32) + qy[:, w2:]
    upd = _mmb(uv_t, jnp.concatenate([bd * g_rem, kd * g_rem], axis=0))
    yield s0 * jnp.exp(ctot) + jnp.where(same_head, upd, 0.0), y


def _rwkv_chunk_kernel(rf_ref, vf_ref, kkf_ref, lwf_ref, kdf_ref, bdf_ref,
                       rb_ref, vb_ref, kkb_ref, lwb_ref, kdb_ref, bdb_ref, *rest, has_init):
    if has_init:
        h0_ref, yf_ref, yb_ref, hfin_ref, h_scr, c_scr = rest
    else:
        yf_ref, yb_ref, hfin_ref, h_scr, c_scr = rest

    @pl.when(pl.program_id(1) == 0)
    def _():
        h_scr[...] = h0_ref[0] if has_init else jnp.zeros(h_scr.shape, F32)

    ti = lax.broadcasted_iota(jnp.int32, (CHUNK, CHUNK), 0)
    si = lax.broadcasted_iota(jnp.int32, (CHUNK, CHUNK), 1)

    def cumsum(tri, x):
        hi = x.astype(BF16)
        r1 = x - hi.astype(F32)
        mid = r1.astype(BF16)
        lo = (r1 - mid.astype(F32)).astype(BF16)
        return _mm(tri, hi) + _mm(tri, mid) + _mm(tri, lo)

    c_scr[0] = cumsum((si <= ti).astype(BF16), lwf_ref[...])
    c_scr[1] = cumsum((si >= ti).astype(BF16), lwb_ref[...])

    w2 = UNIT_W
    tpos = lax.broadcasted_iota(jnp.int32, (CHUNK, w2), 0)
    lane = lax.broadcasted_iota(jnp.int32, (CHUNK, w2), 1)
    spos = lane % CHUNK
    eye2 = (spos == tpos).astype(F32)
    head_masks = [lane // N_C == hh for hh in range(HEADS_PER_UNIT)]
    same_head = (lax.broadcasted_iota(jnp.int32, (w2, w2), 0) // N_C
                 == lax.broadcasted_iota(jnp.int32, (w2, w2), 1) // N_C)
    dirs = ((rf_ref, vf_ref, kkf_ref, lwf_ref, kdf_ref, bdf_ref, yf_ref, spos < tpos, spos <= tpos),
            (rb_ref, vb_ref, kkb_ref, lwb_ref, kdb_ref, bdb_ref, yb_ref, spos > tpos, spos >= tpos))

    def pair_group(pg, carry):
        units = []
        for pi in range(UNITS_PER_BODY):
            p = pg * UNITS_PER_BODY + pi
            ls = pl.ds(pl.multiple_of(p * w2, w2), w2)
            for d, (r_ref, v_ref, kk_ref, lw_ref, kd_ref, bd_ref, y_ref, strict, incl) in enumerate(dirs):
                args = (r_ref[:, ls].astype(F32), v_ref[:, ls].astype(F32), kk_ref[:, ls], lw_ref[:, ls], c_scr[d, :, ls],
                        kd_ref[:, ls], bd_ref[:, ls], h_scr[d, p])
                units.append((d, p, ls, y_ref, args, strict, incl))
        results = _run_lockstep([_chunk_unit(*args, d == 1, head_masks, eye2, strict, incl, same_head)
                                 for d, p, ls, y_ref, args, strict, incl in units])
        for (d, p, ls, y_ref, _, _, _), (h_new, y) in zip(units, results):
            h_scr[d, p] = h_new
            y_ref[:, ls] = y
        return carry

    lax.fori_loop(0, rf_ref.shape[1] // (w2 * UNITS_PER_BODY), pair_group, 0)

    @pl.when(pl.program_id(1) == pl.num_programs(1) - 1)
    def _():
        hfin_ref[0] = h_scr[...]


def _rwkv_chunks(p, kk, lw, kd, bd, h0, row0, nseq, l, rc_col0, vc_col0):
    cc = kk.shape[1]
    nc = l // CHUNK
    rb0 = row0 // CHUNK
    nunit = cc // UNIT_W

    def fwd(s, g):
        return rb0 + s * nc + g

    def bwd(s, g):
        return rb0 + s * nc + nc - 1 - g

    def specs(rowf, d):
        return [pl.BlockSpec((CHUNK, cc), lambda s, g: (rowf(s, g), rc_col0 // cc)),
                pl.BlockSpec((CHUNK, cc), lambda s, g: (rowf(s, g), vc_col0 // cc)),
                pl.BlockSpec((CHUNK, cc), lambda s, g: (rowf(s, g), 0)),
                pl.BlockSpec((None, CHUNK, cc), lambda s, g: (d, rowf(s, g), 0)),
                pl.BlockSpec((None, CHUNK, cc), lambda s, g: (d, rowf(s, g), 0)),
                pl.BlockSpec((None, CHUNK, cc), lambda s, g: (d, rowf(s, g), 0))]

    hshape = (nseq, 2, nunit, UNIT_W, UNIT_W)
    hspec = pl.BlockSpec((1,) + hshape[1:], lambda s, g: (s, 0, 0, 0, 0))
    yshape = jax.ShapeDtypeStruct((nseq * l, cc), F32)
    init = [] if h0 is None else [h0]
    return pl.pallas_call(
        functools.partial(_rwkv_chunk_kernel, has_init=h0 is not None),
        grid=(nseq, nc),
        in_specs=specs(fwd, 0) + specs(bwd, 1) + [hspec] * len(init),
        out_specs=[pl.BlockSpec((CHUNK, cc), lambda s, g: (s * nc + g, 0)),
                   pl.BlockSpec((CHUNK, cc), lambda s, g: (s * nc + nc - 1 - g, 0)),
                   hspec],
        out_shape=[yshape, yshape, jax.ShapeDtypeStruct(hshape, F32)],
        scratch_shapes=[pltpu.VMEM(hshape[1:], F32), pltpu.VMEM((2, CHUNK, cc), F32)],
        compiler_params=_cparams(("parallel", "arbitrary")),
        name="rwkv_chunks",
    )(p, p, kk, lw, kd, bd, p, p, kk, lw, kd, bd, *init)


def _state_to_wide(s):
    nseq, _, nh, n, _ = s.shape
    g = HEADS_PER_UNIT
    ht = s.reshape(nseq, 2, nh // g, g, n, n)
    z = jnp.zeros_like(ht[:, :, :, 0])
    rows = [jnp.concatenate([ht[:, :, :, a] if a == b else z for b in range(g)], axis=-1) for a in range(g)]
    return jnp.concatenate(rows, axis=-2)


def _wide_to_state(h):
    nseq, _, nunit, _, _ = h.shape
    g, n = HEADS_PER_UNIT, N_C
    blocks = jnp.stack([h[..., a * n:(a + 1) * n, a * n:(a + 1) * n] for a in range(g)], axis=3)
    return blocks.reshape(nseq, 2, g * nunit, n, n)


def _rwkv_post_kernel(yf_ref, yb_ref, r_ref, v_ref, kd_ref, g_ref, rk_ref, gw_ref, gb_ref, e2_ref, o_ref):
    e2 = e2_ref[...]
    y = yf_ref[...] + yb_ref[...]
    inv_n = 1.0 / N_C
    d = y - _head_sums(y, e2) * inv_n
    var = _head_sums(d * d, e2) * inv_n
    yn = d * lax.rsqrt(var + GN_EPS) * gw_ref[...] + gb_ref[...]
    bonus = (_head_sums(r_ref[...].astype(F32) * (kd_ref[0] + kd_ref[1]) * rk_ref[...], e2)
             * v_ref[...].astype(F32))
    o_ref[...] = ((yn + bonus) * _silu(g_ref[...].astype(F32))).astype(BF16)


def _rwkv_post(yf, yb, p, kd, row0, rc_col0, vc_col0, gc_col0, r_k, gn_w, gn_b, e2, tm):
    t, cc = yf.shape
    rb0 = row0 // tm
    vec = pl.BlockSpec((1, cc), lambda i: (0, 0))
    own = pl.BlockSpec((tm, cc), lambda i: (i, 0))
    return pl.pallas_call(
        _rwkv_post_kernel,
        grid=(t // tm,),
        in_specs=[own, own,
                  pl.BlockSpec((tm, cc), lambda i: (rb0 + i, rc_col0 // cc)),
                  pl.BlockSpec((tm, cc), lambda i: (rb0 + i, vc_col0 // cc)),
                  pl.BlockSpec((2, tm, cc), lambda i: (0, rb0 + i, 0)),
                  pl.BlockSpec((tm, cc), lambda i: (rb0 + i, gc_col0 // cc)),
                  vec, vec, vec,
                  pl.BlockSpec((LANES, LANES), lambda i: (0, 0))],
        out_specs=own,
        out_shape=jax.ShapeDtypeStruct((t, cc), BF16),
        compiler_params=_cparams(("parallel",)),
        name="rwkv_post",
    )(yf, yb, p, p, kd, p, r_k, gn_w, gn_b, e2)


def _merge_kernel(bap_ref, bas_ref, bb_ref, bcp_ref, bcs_ref, m0_ref, m1_ref, m2_ref, wa_ref, wb_ref, wc_ref, o_ref,
                  *, n_prompt_tiles):
    in_prompt = pl.program_id(0) < n_prompt_tiles
    ba = jnp.where(in_prompt, bap_ref[...], bas_ref[...])
    bc = jnp.where(in_prompt, bcp_ref[...], bcs_ref[...])
    o_ref[...] = (jax.nn.sigmoid(m0_ref[...].astype(F32)) * _mm(ba, wa_ref[...])
                  + jax.nn.sigmoid(m1_ref[...].astype(F32)) * _mm(bb_ref[...], wb_ref[...])
                  + jax.nn.sigmoid(m2_ref[...].astype(F32)) * _mm(bc, wc_ref[...])).astype(BF16)


def _merge(ba_p, ba_s, bb, bc_p, bc_s, p, mg_col0, wa, wb, wc, tm, tn):
    t, wbr = bb.shape
    d = wa.shape[1]
    nj = d // tn
    j0 = mg_col0 // tn
    npt = ba_p.shape[0] // tm
    prm = pl.BlockSpec((tm, wbr), lambda i, j: (jnp.minimum(i, npt - 1), 0))
    smp = pl.BlockSpec((tm, wbr), lambda i, j: (jnp.maximum(i - npt, 0), 0))
    wspec = pl.BlockSpec((wbr, tn), lambda i, j: (0, j))
    return pl.pallas_call(
        functools.partial(_merge_kernel, n_prompt_tiles=npt),
        grid=(t // tm, nj),
        in_specs=[prm, smp, pl.BlockSpec((tm, wbr), lambda i, j: (i, 0)), prm, smp,
                  pl.BlockSpec((tm, tn), lambda i, j: (i, j0 + j)),
                  pl.BlockSpec((tm, tn), lambda i, j: (i, j0 + nj + j)),
                  pl.BlockSpec((tm, tn), lambda i, j: (i, j0 + 2 * nj + j)),
                  wspec, wspec, wspec],
        out_specs=pl.BlockSpec((tm, tn), lambda i, j: (i, j)),
        out_shape=jax.ShapeDtypeStruct((t, d), BF16),
        compiler_params=_cparams(("parallel", "arbitrary")),
        name="branch_merge",
    )(ba_p, ba_s, bb, bc_p, bc_s, p, p, p, wa, wb, wc)


def _resid_kernel(m_ref, wo_ref, x_ref, mod_ref, fw_ref, *o_refs, n_prompt_tiles):
    x = x_ref[...] + mod_ref[0, 2:3, :] * _mm(m_ref[...], wo_ref[...])
    if len(o_refs) == 1:
        o_refs[0][...] = x
        return
    x = x * lax.rsqrt(jnp.mean(x * x, axis=-1, keepdims=True) + EPS) * fw_ref[...]
    in_prompt = pl.program_id(0) < n_prompt_tiles

    @pl.when(in_prompt)
    def _():
        o_refs[0][...] = x

    @pl.when(jnp.logical_not(in_prompt))
    def _():
        o_refs[1][...] = x


def _out_proj(merged, wo, x, mod, final_w, cond_of_tile, tm, tp, final):
    t, d = x.shape
    npt = tp // tm
    row = pl.BlockSpec((tm, d), lambda i: (i, 0))
    if final:
        out_specs = [pl.BlockSpec((tm, d), lambda i: (jnp.minimum(i, npt - 1), 0)),
                     pl.BlockSpec((tm, d), lambda i: (jnp.maximum(i - npt, 0), 0))]
        out_shape = [jax.ShapeDtypeStruct((tp, d), F32), jax.ShapeDtypeStruct((t - tp, d), F32)]
    else:
        out_specs, out_shape = row, jax.ShapeDtypeStruct((t, d), F32)
    return pl.pallas_call(
        functools.partial(_resid_kernel, n_prompt_tiles=npt),
        grid=(t // tm,),
        in_specs=[row,
                  pl.BlockSpec((d, d), lambda i: (0, 0)),
                  row,
                  pl.BlockSpec((1, 3, d), lambda i: (cond_of_tile(i), 0, 0)),
                  pl.BlockSpec((1, d), lambda i: (0, 0))],
        out_specs=out_specs,
        out_shape=out_shape,
        compiler_params=_cparams(("arbitrary",)),
        name="out_proj",
    )(merged, wo, x, mod, final_w)


def _largest_tile(n, cap):
    t = cap
    while n % t:
        t //= 2
    return t


def kernel(x_prompt, x_sample, cache_k, cache_v, state_rwkv, c, c_ctx, w_ada, b_ada, norm_w, w_in,
           lambda_q1, lambda_k1, lambda_q2, lambda_k2, subln_w, conv_w, conv_b, conv_ln_w, conv_ln_b,
           rwkv_w0, rwkv_w_up, rwkv_a0, rwkv_a_up, rwkv_k_k, rwkv_k_a, rwkv_r_k, rwkv_gn_w, rwkv_gn_b,
           w_br_a, w_br_b, w_br_c, w_out, final_norm_w):
    bp, lp_, d = x_prompt.shape
    bs, ls, _ = x_sample.shape
    depth = w_ada.shape[0]
    past = cache_k.shape[2]
    tp, ts = bp * lp_, bs * ls
    n_ha = d // 256
    w_a = n_ha * HEAD_A
    c_b = d // 2
    n_hc = d // (2 * N_C)
    c_c = n_hc * N_C
    n_mg = 3 * d
    assert n_ha * 2 * DH_A == w_a and w_a == c_b == c_c
    assert bs + 1 <= SUBLANES

    col = {}
    off = 0
    for name, width in (("mg", n_mg), ("q", w_a), ("k", w_a), ("v", w_a), ("g_a", w_a), ("glu", 2 * c_b),
                        ("g_b", c_b), ("r_c", c_c), ("k_c", c_c), ("v_c", c_c), ("g_c", c_c),
                        ("xw", 2 * LORA), ("xa", 2 * LORA)):
        col[name] = off
        off += width
    in_cols = off
    n_lead = in_cols - n_mg

    tm_in = _largest_tile(math.gcd(tp, ls), 1024)
    tm_out = _largest_tile(math.gcd(tp, ls), 512)
    tn_in = 768 if in_cols % 768 == 0 else 256
    tm_mg = _largest_tile(tp + ts, 1024)
    tn_mg = math.gcd(col["mg"], 512)
    tt = _largest_tile(math.gcd(lp_, ls), 256)
    tm_rw = _largest_tile(math.gcd(tp, ts), 256)

    def cond_of(tm):
        npt = tp // tm
        return lambda i: jnp.where(i < npt, 0, 1 + ((i - npt) * tm) // ls)

    x = jnp.concatenate([x_prompt.reshape(tp, d), x_sample.reshape(ts, d)], axis=0)
    cond8 = jnp.zeros((SUBLANES, d), F32).at[0].set(c_ctx).at[1:1 + bs].set(c)
    rope_tabs = _rope_tables(ls)
    blk = jnp.arange(LANES) // N_C
    e2 = (blk[:, None] == blk[None, :]).astype(BF16)
    zpad = jnp.zeros((LORA, c_c), F32)

    ks, vs, ss = [], [], []
    for l in range(depth):
        lam_init = 0.8 - 0.6 * math.exp(-0.3 * l)
        mod = _ada(cond8, w_ada, b_ada[:, None, :], l).reshape(SUBLANES, 3, d)
        w_rot = jnp.concatenate([w_in[l][:, n_lead:], w_in[l][:, :n_lead]], axis=1).astype(BF16)
        p = _inproj(x, mod, norm_w[l][None, :], w_rot, cond_of(tm_in), tm_in, tn_in)

        lam_p = jnp.stack([lambda_q1[l], lambda_k1[l], lambda_q2[l], lambda_k2[l]], axis=0)
        sw = subln_w[l][None, :]
        cq, ck, cv, cg = (col[n] // w_a for n in ("q", "k", "v", "g_a"))
        tq_p = _largest_tile(lp_, 256)
        npq = lp_ // tq_p
        ba_p = _attention(
            lam_p, sw,
            p, lambda b, h, i: (b * npq + i, cq),
            p, lambda b, h, i: (b, ck),
            p, lambda b, h, i: (b, cv),
            p, lambda b, h, i: (b * npq + i, cg),
            None, bp, n_ha, n_ha, lp_, lp_, tq_p, lam_init, Q_SCALE)
        qk_rot = _rope_qk(p, rope_tabs, tp, ts, ls, col["q"], n_ha)
        cache = (cache_k[:, l].reshape(bs * past, w_a).astype(BF16),
                 cache_v[:, l].reshape(bs * past, w_a).astype(BF16), past)
        cv1, cg1 = col["v"] // HEAD_A, col["g_a"] // HEAD_A
        tq_s = _largest_tile(ls, 1024)
        nsq = ls // tq_s
        ba_s = _attention(
            lam_p, sw,
            qk_rot, lambda b, h, i: (b * nsq + i, h),
            qk_rot, lambda b, h, i: (b, n_ha + h),
            p, lambda b, h, i: (tp // ls + b, cv1 + h),
            p, lambda b, h, i: (tp // tq_s + b * nsq + i, cg1 + h),
            cache, bs, n_ha, 1, ls, ls, tq_s, lam_init, 1.0)

        bb = _conv(p, conv_w[l], conv_b[l][None, :], conv_ln_w[l][None, :], conv_ln_b[l][None, :],
                   col["glu"], col["g_b"], tt, tp // tt, lp_ // tt, ls // tt)

        wup_pad = jnp.stack([jnp.concatenate([rwkv_w_up[l, 0], zpad], 0), jnp.concatenate([zpad, rwkv_w_up[l, 1]], 0)])
        aup_pad = jnp.stack([jnp.concatenate([rwkv_a_up[l, 0], zpad], 0), jnp.concatenate([zpad, rwkv_a_up[l, 1]], 0)])
        kk, lw, kd, bd = _rwkv_prep(p, col["k_c"], col["xw"], col["xa"], rwkv_w0[l], wup_pad, rwkv_a0[l], aup_pad,
                                    rwkv_k_k[l][None, :], rwkv_k_a[l][None, :], e2, tm_rw)
        yf_p, yb_p, hfin_p = _rwkv_chunks(p, kk, lw, kd, bd, None, 0, bp, lp_, col["r_c"], col["v_c"])
        ss.append(_wide_to_state(hfin_p))
        yf_s, yb_s, _ = _rwkv_chunks(p, kk, lw, kd, bd, _state_to_wide(state_rwkv[:, l]), tp, bs, ls,
                                     col["r_c"], col["v_c"])
        post_args = (col["r_c"], col["v_c"], col["g_c"], rwkv_r_k[l].reshape(1, c_c),
                     rwkv_gn_w[l][None, :], rwkv_gn_b[l][None, :], e2, tm_rw)
        bc_p = _rwkv_post(yf_p, yb_p, p, kd, 0, *post_args)
        bc_s = _rwkv_post(yf_s, yb_s, p, kd, tp, *post_args)

        merged = _merge(ba_p, ba_s, bb, bc_p, bc_s, p, col["mg"], w_br_a[l].astype(BF16), w_br_b[l].astype(BF16), w_br_c[l].astype(BF16),
                        tm_mg, tn_mg)
        x = _out_proj(merged, w_out[l].astype(BF16), x, mod, final_norm_w[None, :], cond_of(tm_out), tm_out, tp,
                      final=(l == depth - 1))

        ks.append(p[:tp, col["k"]:col["k"] + w_a].astype(F32).reshape(bp, lp_, n_ha, 2, DH_A))
        vs.append(p[:tp, col["v"]:col["v"] + w_a].astype(F32).reshape(bp, lp_, n_ha, HEAD_A))

    y_prompt = x[0].reshape(bp, lp_, d)
    y_sample = x[1].reshape(bs, ls, d)
    return (y_prompt, y_sample, jnp.stack(ks, axis=1), jnp.stack(vs, axis=1), jnp.stack(ss, axis=1))
```

```python
import functools
import math

import jax
import jax.numpy as jnp
from jax import lax
from jax.experimental import pallas as pl
from jax.experimental.pallas import tpu as pltpu

F32 = jnp.float32
BF16 = jnp.bfloat16

HEAD_A = 128
DH_A = 64
CONV_W = 31
CONV_HALO = 16
N_C = 64
LORA = 64
GRID_W = 64
ROPE_BASE = 10000.0
EPS = 1e-6
LN_EPS = 1e-5
GN_EPS = 64e-5
Q_SCALE = DH_A ** -0.5 * math.log2(math.e)

LANES = 128
SUBLANES = 8
MIB = 1024 * 1024


def _cparams(semantics, vmem_mib=48):
    return pltpu.CompilerParams(dimension_semantics=semantics, vmem_limit_bytes=vmem_mib * MIB)


def _split_bf16(x):
    hi = x.astype(BF16)
    lo = (x - hi.astype(F32)).astype(BF16)
    return hi, lo


_NT = (((1,), (1,)), ((), ()))


def _mm(a, b):
    return jnp.dot(a, b, preferred_element_type=F32)


def _dot3(a, b):
    ah, al = _split_bf16(a)
    bh, bl = _split_bf16(b)
    return _mm(ah, bh) + _mm(ah, bl) + _mm(al, bh)


def _silu(x):
    return x * jax.nn.sigmoid(x)


def _run_lockstep(gens):
    last = [None] * len(gens)
    live = list(range(len(gens)))
    while live:
        nxt = []
        for i in live:
            try:
                last[i] = next(gens[i])
                nxt.append(i)
            except StopIteration:
                pass
        live = nxt
    return last


def _head_sums(x, e2):
    xb = x.astype(BF16)
    parts = [_mm(xb[:, c * LANES:(c + 1) * LANES], e2) for c in range(x.shape[1] // LANES)]
    return jnp.concatenate(parts, axis=1)


def _ada_kernel(c_ref, w_ref, b_ref, o_ref):
    o_ref[...] = _dot3(_silu(c_ref[...]), w_ref[...]) + b_ref[...]


def _ada(cond8, w, b, layer):
    _, d, n = w.shape
    tn = 512
    return pl.pallas_call(
        _ada_kernel,
        grid=(n // tn,),
        in_specs=[pl.BlockSpec((SUBLANES, d), lambda j: (0, 0)),
                  pl.BlockSpec((None, d, tn), lambda j: (layer, 0, j)),
                  pl.BlockSpec((None, 1, tn), lambda j: (layer, 0, j))],
        out_specs=pl.BlockSpec((SUBLANES, tn), lambda j: (0, j)),
        out_shape=jax.ShapeDtypeStruct((SUBLANES, n), F32),
        compiler_params=_cparams(("arbitrary",)),
        name="ada_mod",
    )(cond8, w, b)


def _inproj_kernel(x_ref, mod_ref, nw_ref, w_ref, o_ref, h_scr):
    @pl.when(pl.program_id(1) == 0)
    def _():
        x = x_ref[...]
        y = x * lax.rsqrt(jnp.mean(x * x, axis=-1, keepdims=True) + EPS) * nw_ref[...]
        h = y * (1.0 + mod_ref[0, 1:2, :]) + mod_ref[0, 0:1, :]
        h_scr[...] = h.astype(BF16)

    o_ref[...] = _mm(h_scr[...], w_ref[...]).astype(o_ref.dtype)


def _inproj(x, mod, norm_w, w_bf16, cond_of_tile, tm, tn):
    t, d = x.shape
    n = w_bf16.shape[1]
    return pl.pallas_call(
        _inproj_kernel,
        grid=(t // tm, n // tn),
        in_specs=[pl.BlockSpec((tm, d), lambda i, j: (i, 0)),
                  pl.BlockSpec((1, 3, d), lambda i, j: (cond_of_tile(i), 0, 0)),
                  pl.BlockSpec((1, d), lambda i, j: (0, 0)),
                  pl.BlockSpec((d, tn), lambda i, j: (0, j))],
        out_specs=pl.BlockSpec((tm, tn), lambda i, j: (i, j)),
        out_shape=jax.ShapeDtypeStruct((t, n), BF16),
        scratch_shapes=[pltpu.VMEM((tm, d), BF16)],
        compiler_params=_cparams(("parallel", "arbitrary")),
        name="inproj",
    )(x, mod, norm_w, w_bf16)


def _rope_tab_kernel(inv_ref, c_ref, s1_ref, s2_ref, *, tr):
    t = lax.broadcasted_iota(jnp.int32, (tr, LANES), 0) + pl.program_id(0) * tr
    lane = lax.broadcasted_iota(jnp.int32, (tr, LANES), 1)
    row = (t // GRID_W).astype(F32)
    col = (t % GRID_W).astype(F32)
    pos = jnp.where((lane % DH_A) >= DH_A // 2, col, row)
    ang = pos * inv_ref[...]
    c = jnp.cos(ang)
    s = jnp.sin(ang)
    first = (lane % (DH_A // 2)) < DH_A // 4
    c_ref[...] = c
    s1_ref[...] = jnp.where(first, -s, 0.0)
    s2_ref[...] = jnp.where(first, 0.0, s)


def _rope_tables(ls):
    tr = min(ls, 512)
    quarter = DH_A // 4
    inv = ROPE_BASE ** (-jnp.arange(quarter, dtype=F32) / quarter)
    inv = jnp.tile(inv, LANES // quarter)[None, :]
    shp = jax.ShapeDtypeStruct((ls, LANES), F32)
    spec = pl.BlockSpec((tr, LANES), lambda i: (i, 0))
    return pl.pallas_call(
        functools.partial(_rope_tab_kernel, tr=tr),
        grid=(ls // tr,),
        in_specs=[pl.BlockSpec((1, LANES), lambda i: (0, 0))],
        out_specs=[spec, spec, spec],
        out_shape=[shp, shp, shp],
        compiler_params=_cparams(("arbitrary",)),
        name="rope_tables",
    )(inv)


def _rope_kernel(x_ref, c_ref, s1_ref, s2_ref, o_ref, *, n_q):
    quarter = DH_A // 4
    c, s1, s2 = c_ref[...], s1_ref[...], s2_ref[...]
    for j in range(x_ref.shape[1] // LANES):
        x = x_ref[:, j * LANES:(j + 1) * LANES].astype(F32)
        y = x * c + pltpu.roll(x, LANES - quarter, 1) * s1 + pltpu.roll(x, quarter, 1) * s2
        if j < n_q:
            y = y * Q_SCALE
        o_ref[:, j * LANES:(j + 1) * LANES] = y.astype(BF16)


def _rope_qk(p, tabs, row0, ts, ls, col0, n_heads):
    tr = min(ls, 512)
    width = 2 * n_heads * LANES
    tab_spec = pl.BlockSpec((tr, LANES), lambda i: (i % (ls // tr), 0))
    return pl.pallas_call(
        functools.partial(_rope_kernel, n_q=n_heads),
        grid=(ts // tr,),
        in_specs=[pl.BlockSpec((tr, width), lambda i: (row0 // tr + i, col0 // width)),
                  tab_spec, tab_spec, tab_spec],
        out_specs=pl.BlockSpec((tr, width), lambda i: (i, 0)),
        out_shape=jax.ShapeDtypeStruct((ts, width), BF16),
        compiler_params=_cparams(("parallel",)),
        name="rope_qk",
    )(p, *tabs)


ATTN_SUB = 128
ATTN_CHAINS = 4


def _attn_kernel(lam_ref, sw_ref, q_ref, k_ref, v_ref, g_ref, *rest, lam_init, q_scale, has_cache):
    if has_cache:
        kc_ref, vc_ref, o_ref = rest
    else:
        (o_ref,) = rest
    lp = lam_ref[...]
    lam = (jnp.exp(jnp.sum(lp[0:1, :] * lp[1:2, :], axis=1, keepdims=True))
           - jnp.exp(jnp.sum(lp[2:3, :] * lp[3:4, :], axis=1, keepdims=True)) + lam_init)
    tq = q_ref.shape[0]
    rs = min(ATTN_SUB, tq)
    lane = lax.broadcasted_iota(jnp.int32, (rs, HEAD_A), 1)
    zero = jnp.zeros((rs, HEAD_A), BF16)

    def head_kv(hh):
        hs = slice(hh * HEAD_A, (hh + 1) * HEAD_A)
        k = k_ref[:, hs].astype(BF16)
        v = v_ref[:, hs].astype(BF16)
        if has_cache:
            k = jnp.concatenate([kc_ref[:, hs], k], axis=0)
            v = jnp.concatenate([vc_ref[:, hs], v], axis=0)
        return k, jnp.concatenate([v, jnp.ones_like(v)], axis=1)

    def logits(hh, i, k):
        q = q_ref[i * rs:(i + 1) * rs, hh * HEAD_A:(hh + 1) * HEAD_A]
        if q_scale != 1.0:
            q = q.astype(F32) * q_scale
        q = q.astype(BF16)
        lhs = jnp.concatenate([jnp.where(lane < DH_A, q, zero), jnp.where(lane < DH_A, zero, q)], axis=0)
        return lax.dot_general(lhs, k, _NT, preferred_element_type=F32)

    kv = {}

    def kv_of(hh):
        if hh not in kv:
            kv[hh] = head_kv(hh)
        return kv[hh]

    def chain(items):
        s = logits(*items[0], kv_of(items[0][0])[0])
        yield
        for idx, (hh, i) in enumerate(items):
            if idx + 1 < len(items):
                s_next = logits(*items[idx + 1], kv_of(items[idx + 1][0])[0])
                yield
            e = jnp.exp2(s - jnp.max(s, axis=-1, keepdims=True)).astype(BF16)
            oa = _mm(e, kv_of(hh)[1])
            yield
            on = oa[:, :HEAD_A] * (1.0 / oa[:, HEAD_A:HEAD_A + 1])
            o = on[:rs] - lam * on[rs:]
            o = o * lax.rsqrt(jnp.mean(o * o, axis=-1, keepdims=True) + EPS) * sw_ref[...] * (1.0 - lam_init)
            rows, hs = slice(i * rs, (i + 1) * rs), slice(hh * HEAD_A, (hh + 1) * HEAD_A)
            o_ref[rows, hs] = (o * _silu(g_ref[rows, hs].astype(F32))).astype(BF16)
            if idx + 1 < len(items):
                s = s_next

    n_heads_blk = q_ref.shape[1] // HEAD_A
    n_chains = min(n_heads_blk, ATTN_CHAINS)
    items = [(hh, i) for hh in range(n_heads_blk) for i in range(tq // rs)]
    per = len(items) // n_chains
    _run_lockstep([chain(items[c * per:(c + 1) * per]) for c in range(n_chains)])


def _attention(lam_p, subln_w, q_arr, q_map, k_arr, k_map, v_arr, v_map, g_arr, g_map, cache,
               n_b, n_heads, hps, lq, lk_new, tq, lam_init, q_scale):
    nq = lq // tq
    wide = hps * HEAD_A
    in_specs = [pl.BlockSpec(lam_p.shape, lambda b, h, i: (0, 0)),
                pl.BlockSpec((1, HEAD_A), lambda b, h, i: (0, 0)),
                pl.BlockSpec((tq, wide), q_map),
                pl.BlockSpec((lk_new, wide), k_map),
                pl.BlockSpec((lk_new, wide), v_map),
                pl.BlockSpec((tq, wide), g_map)]
    args = [lam_p, subln_w, q_arr, k_arr, v_arr, g_arr]
    if cache is not None:
        kc, vc, past = cache
        cspec = pl.BlockSpec((past, wide), lambda b, h, i: (b, h))
        in_specs += [cspec, cspec]
        args += [kc, vc]
    return pl.pallas_call(
        functools.partial(_attn_kernel, lam_init=lam_init, q_scale=q_scale, has_cache=cache is not None),
        grid=(n_b, n_heads // hps, nq),
        in_specs=in_specs,
        out_specs=pl.BlockSpec((tq, wide), lambda b, h, i: (b * nq + i, h)),
        out_shape=jax.ShapeDtypeStruct((n_b * lq, n_heads * HEAD_A), BF16),
        compiler_params=_cparams(("parallel", "parallel", "arbitrary")),
        name="diff_attn",
    )(*args)


def _conv_kernel(u_ref, up_ref, un_ref, g_ref, cw_ref, cb_ref, lw_ref, lb_ref, o_ref, zp_scr, acc_scr, zs_scr,
                 *, tt, n_prompt_blocks, blocks_per_prompt_seq, blocks_per_sample_seq):
    cb_ = u_ref.shape[1] // 2
    blk = pl.program_id(0)
    in_prompt = blk < n_prompt_blocks
    pos = jnp.where(in_prompt, blk % blocks_per_prompt_seq, (blk - n_prompt_blocks) % blocks_per_sample_seq)
    per_seq = jnp.where(in_prompt, blocks_per_prompt_seq, blocks_per_sample_seq)
    keep_prev = (pos > 0).astype(F32)
    keep_next = (pos < per_seq - 1).astype(F32)

    def glu(ref):
        u = ref[...].astype(F32)
        return u[:, :cb_] * jax.nn.sigmoid(u[:, cb_:])

    h = CONV_HALO
    zp_scr[pl.ds(0, h), :] = glu(up_ref) * keep_prev
    zp_scr[pl.ds(h, tt), :] = glu(u_ref)
    zp_scr[pl.ds(h + tt, h), :] = glu(un_ref) * keep_next

    rc = 64
    off = h - CONV_W // 2
    n_sh = zs_scr.shape[1]

    def lane_chunk(c, carry):
        ls = pl.ds(pl.multiple_of(c * LANES, LANES), LANES)
        w = cw_ref[:, ls]
        for s in range(SUBLANES):
            zs_scr[s] = zp_scr[pl.ds(s, n_sh), ls]
        for r0 in range(0, tt, rc):
            acc = jnp.zeros((rc, LANES), F32)
            for tau in range(CONV_W):
                o = tau + off
                acc = acc + zs_scr[o % SUBLANES, pl.ds(r0 + o - o % SUBLANES, rc), :] * w[tau:tau + 1, :]
            acc_scr[pl.ds(r0, rc), ls] = acc
        return carry

    lax.fori_loop(0, cb_ // LANES, lane_chunk, 0)

    z = acc_scr[...] + cb_ref[...]
    mu = jnp.mean(z, axis=-1, keepdims=True)
    d = z - mu
    var = jnp.mean(d * d, axis=-1, keepdims=True)
    zn = d * lax.rsqrt(var + LN_EPS) * lw_ref[...] + lb_ref[...]
    o_ref[...] = (_silu(zn) * _silu(g_ref[...].astype(F32))).astype(BF16)


def _conv(p, conv_w, conv_b, ln_w, ln_b, glu_col0, gb_col0, tt, n_prompt_blocks,
          blocks_per_prompt_seq, blocks_per_sample_seq):
    t = p.shape[0]
    cb_ = conv_w.shape[1]
    nblk = t // tt
    hb = tt // CONV_HALO
    n_hblk = t // CONV_HALO
    vec = pl.BlockSpec((1, cb_), lambda i: (0, 0))
    return pl.pallas_call(
        functools.partial(_conv_kernel, tt=tt, n_prompt_blocks=n_prompt_blocks,
                          blocks_per_prompt_seq=blocks_per_prompt_seq,
                          blocks_per_sample_seq=blocks_per_sample_seq),
        grid=(nblk,),
        in_specs=[pl.BlockSpec((tt, 2 * cb_), lambda i: (i, glu_col0 // (2 * cb_))),
                  pl.BlockSpec((CONV_HALO, 2 * cb_), lambda i: (jnp.maximum(i * hb - 1, 0), glu_col0 // (2 * cb_))),
                  pl.BlockSpec((CONV_HALO, 2 * cb_),
                               lambda i: (jnp.minimum((i + 1) * hb, n_hblk - 1), glu_col0 // (2 * cb_))),
                  pl.BlockSpec((tt, cb_), lambda i: (i, gb_col0 // cb_)),
                  pl.BlockSpec(conv_w.shape, lambda i: (0, 0)),
                  vec, vec, vec],
        out_specs=pl.BlockSpec((tt, cb_), lambda i: (i, 0)),
        out_shape=jax.ShapeDtypeStruct((t, cb_), BF16),
        scratch_shapes=[pltpu.VMEM((tt + 2 * CONV_HALO, cb_), F32), pltpu.VMEM((tt, cb_), F32),
                        pltpu.VMEM((SUBLANES, tt + 2 * CONV_HALO - SUBLANES, LANES), F32)],
        compiler_params=_cparams(("parallel",)),
        name="conformer_conv",
    )(p, p, p, p, conv_w, conv_b, ln_w, ln_b)


def _rwkv_prep_kernel(k_ref, xw_ref, xa_ref, w0_ref, wup_ref, a0_ref, aup_ref, kk_ref, ka_ref, e2_ref,
                      okk_ref, olw_ref, okd_ref, obd_ref):
    k = k_ref[...].astype(F32)
    e2 = e2_ref[...]
    kkr = k * kk_ref[...]
    kk = kkr / jnp.maximum(jnp.sqrt(_head_sums(kkr * kkr, e2)), 1e-12)
    okk_ref[...] = kk
    txw = jnp.tanh(xw_ref[...].astype(F32))
    xa = xa_ref[...].astype(F32)
    for d in range(2):
        olw_ref[d] = -math.exp(-0.5) * jax.nn.sigmoid(w0_ref[d:d + 1, :] + _dot3(txw, wup_ref[d]))
        a = jax.nn.sigmoid(a0_ref[d:d + 1, :] + _dot3(xa, aup_ref[d]))
        okd_ref[d] = k * (1.0 + (a - 1.0) * ka_ref[...])
        obd_ref[d] = kk * a


def _rwkv_prep(p, kc_col0, xw_col0, xa_col0, w0, wup_pad, a0, aup_pad, k_k, k_a, e2, tm):
    t = p.shape[0]
    cc = w0.shape[1]
    vec = pl.BlockSpec((1, cc), lambda i: (0, 0))
    vec2 = pl.BlockSpec((2, cc), lambda i: (0, 0))
    mat = pl.BlockSpec((2, LANES, cc), lambda i: (0, 0, 0))
    one = jax.ShapeDtypeStruct((t, cc), F32)
    two = jax.ShapeDtypeStruct((2, t, cc), F32)
    two_spec = pl.BlockSpec((2, tm, cc), lambda i: (0, i, 0))
    return pl.pallas_call(
        _rwkv_prep_kernel,
        grid=(t // tm,),
        in_specs=[pl.BlockSpec((tm, cc), lambda i: (i, kc_col0 // cc)),
                  pl.BlockSpec((tm, LANES), lambda i: (i, xw_col0 // LANES)),
                  pl.BlockSpec((tm, LANES), lambda i: (i, xa_col0 // LANES)),
                  vec2, mat, vec2, mat, vec, vec,
                  pl.BlockSpec((LANES, LANES), lambda i: (0, 0))],
        out_specs=[pl.BlockSpec((tm, cc), lambda i: (i, 0)), two_spec, two_spec, two_spec],
        out_shape=[one, two, two, two],
        compiler_params=_cparams(("parallel",)),
        name="rwkv_prep",
    )(p, p, p, w0, wup_pad, a0, aup_pad, k_k, k_a, e2)


CHUNK = 64
HEADS_PER_UNIT = 2
UNIT_W = HEADS_PER_UNIT * N_C
UNITS_PER_BODY = 8


def _mmb(a, b):
    return _mm(a.astype(BF16), b.astype(BF16))


def _chunk_unit(r, v, kk, lw, c, kd, bd, s0, reverse, head_masks, eye2, strict, incl, same_head):
    w2 = UNIT_W
    ctot = c[0:1, :] if reverse else c[CHUNK - 1:CHUNK, :]
    g_inv = jnp.exp(-c)
    g_rem = jnp.exp(ctot - c)
    rt = r * jnp.exp(c)
    bt = -(kk * jnp.exp(c - lw))

    def stack(x):
        xb = x.astype(BF16)
        z = jnp.zeros_like(xb)
        return jnp.concatenate([jnp.where(m, xb, z) for m in head_masks], axis=0)

    v_s = stack(v)
    s0b = s0.astype(BF16)
    lhs = jnp.concatenate([bt, rt], axis=0).astype(BF16)
    rhs = jnp.concatenate([stack(bd * g_inv), stack(kd * g_inv)], axis=0)
    nn = lax.dot_general(lhs, rhs, _NT, preferred_element_type=F32)
    yield None
    a_ab = jnp.where(strict, nn[:CHUNK, :w2], 0.0)
    a_bk = jnp.where(strict, nn[:CHUNK, w2:], 0.0)
    a_r = jnp.concatenate([jnp.where(incl, nn[CHUNK:, :w2], 0.0), jnp.where(incl, nn[CHUNK:, w2:], 0.0)], axis=1)
    p1 = _mm(a_bk.astype(BF16), v_s)
    t = eye2 + a_ab
    n = _mm(a_ab.astype(BF16), stack(a_ab))
    yield None
    for _ in range(int(math.log2(CHUNK)) - 2):
        prod = _mm(jnp.concatenate([n, t], axis=0).astype(BF16), stack(n))
        yield None
        t = t + prod[CHUNK:]
        n = prod[:CHUNK]
    t = t + _mm(t.astype(BF16), stack(n))
    yield None
    wx = _mm(t.astype(BF16), jnp.concatenate([stack(bt), stack(p1)], axis=1))
    yield None
    w, x = wx[:, :w2], wx[:, w2:]
    rhs_big = jnp.concatenate(
        [jnp.concatenate([stack(w), stack(x)], axis=1),
         jnp.concatenate([jnp.zeros((w2, w2), BF16), v_s], axis=1)], axis=0)
    qy = _mm(a_r.astype(BF16), rhs_big)
    w_pad = jnp.concatenate([w, jnp.zeros_like(w)], axis=0).astype(BF16)
    uv_t = jnp.concatenate([x, v], axis=0).T + lax.dot_general(s0b, w_pad, _NT, preferred_element_type=F32)
    yield None
    qhat = rt + qy[:, :w2]
    y = lax.dot_general(qhat.astype(BF16), s0b, _NT, preferred_element_type=F32) + qy[:, w2:]
    upd = _mmb(uv_t, jnp.concatenate([bd * g_rem, kd * g_rem], axis=0))
    yield s0 * jnp.exp(ctot) + jnp.where(same_head, upd, 0.0), y


def _rwkv_chunk_kernel(rf_ref, vf_ref, kkf_ref, lwf_ref, kdf_ref, bdf_ref,
                       rb_ref, vb_ref, kkb_ref, lwb_ref, kdb_ref, bdb_ref, *rest, has_init):
    if has_init:
        h0_ref, yf_ref, yb_ref, hfin_ref, h_scr, c_scr = rest
    else:
        yf_ref, yb_ref, hfin_ref, h_scr, c_scr = rest

    @pl.when(pl.program_id(1) == 0)
    def _():
        h_scr[...] = h0_ref[0] if has_init else jnp.zeros(h_scr.shape, F32)

    ti = lax.broadcasted_iota(jnp.int32, (CHUNK, CHUNK), 0)
    si = lax.broadcasted_iota(jnp.int32, (CHUNK, CHUNK), 1)

    def cumsum(tri, x):
        hi = x.astype(BF16)
        r1 = x - hi.astype(F32)
        mid = r1.astype(BF16)
        lo = (r1 - mid.astype(F32)).astype(BF16)
        return _mm(tri, hi) + _mm(tri, mid) + _mm(tri, lo)

    c_scr[0] = cumsum((si <= ti).astype(BF16), lwf_ref[...])
    c_scr[1] = cumsum((si >= ti).astype(BF16), lwb_ref[...])

    w2 = UNIT_W
    tpos = lax.broadcasted_iota(jnp.int32, (CHUNK, w2), 0)
    lane = lax.broadcasted_iota(jnp.int32, (CHUNK, w2), 1)
    spos = lane % CHUNK
    eye2 = (spos == tpos).astype(F32)
    head_masks = [lane // N_C == hh for hh in range(HEADS_PER_UNIT)]
    same_head = (lax.broadcasted_iota(jnp.int32, (w2, w2), 0) // N_C
                 == lax.broadcasted_iota(jnp.int32, (w2, w2), 1) // N_C)
    dirs = ((rf_ref, vf_ref, kkf_ref, lwf_ref, kdf_ref, bdf_ref, yf_ref, spos < tpos, spos <= tpos),
            (rb_ref, vb_ref, kkb_ref, lwb_ref, kdb_ref, bdb_ref, yb_ref, spos > tpos, spos >= tpos))

    def pair_group(pg, carry):
        units = []
        for pi in range(UNITS_PER_BODY):
            p = pg * UNITS_PER_BODY + pi
            ls = pl.ds(pl.multiple_of(p * w2, w2), w2)
            for d, (r_ref, v_ref, kk_ref, lw_ref, kd_ref, bd_ref, y_ref, strict, incl) in enumerate(dirs):
                args = (r_ref[:, ls].astype(F32), v_ref[:, ls].astype(F32), kk_ref[:, ls], lw_ref[:, ls], c_scr[d, :, ls],
                        kd_ref[:, ls], bd_ref[:, ls], h_scr[d, p])
                units.append((d, p, ls, y_ref, args, strict, incl))
        results = _run_lockstep([_chunk_unit(*args, d == 1, head_masks, eye2, strict, incl, same_head)
                                 for d, p, ls, y_ref, args, strict, incl in units])
        for (d, p, ls, y_ref, _, _, _), (h_new, y) in zip(units, results):
            h_scr[d, p] = h_new
            y_ref[:, ls] = y
        return carry

    lax.fori_loop(0, rf_ref.shape[1] // (w2 * UNITS_PER_BODY), pair_group, 0)

    @pl.when(pl.program_id(1) == pl.num_programs(1) - 1)
    def _():
        for d in range(2):
            for u in range(h_scr.shape[1]):
                for a in range(HEADS_PER_UNIT):
                    blk = slice(a * N_C, (a + 1) * N_C)
                    hfin_ref[0, d, u * HEADS_PER_UNIT + a] = h_scr[d, u, blk, blk]


def _rwkv_chunks(p, kk, lw, kd, bd, h0, row0, nseq, l, rc_col0, vc_col0):
    cc = kk.shape[1]
    nc = l // CHUNK
    rb0 = row0 // CHUNK
    nunit = cc // UNIT_W

    def fwd(s, g):
        return rb0 + s * nc + g

    def bwd(s, g):
        return rb0 + s * nc + nc - 1 - g

    def specs(rowf, d):
        return [pl.BlockSpec((CHUNK, cc), lambda s, g: (rowf(s, g), rc_col0 // cc)),
                pl.BlockSpec((CHUNK, cc), lambda s, g: (rowf(s, g), vc_col0 // cc)),
                pl.BlockSpec((CHUNK, cc), lambda s, g: (rowf(s, g), 0)),
                pl.BlockSpec((None, CHUNK, cc), lambda s, g: (d, rowf(s, g), 0)),
                pl.BlockSpec((None, CHUNK, cc), lambda s, g: (d, rowf(s, g), 0)),
                pl.BlockSpec((None, CHUNK, cc), lambda s, g: (d, rowf(s, g), 0))]

    hshape = (nseq, 2, nunit, UNIT_W, UNIT_W)
    fshape = (nseq, 2, cc // N_C, N_C, N_C)
    hspec = pl.BlockSpec((1,) + hshape[1:], lambda s, g: (s, 0, 0, 0, 0))
    yshape = jax.ShapeDtypeStruct((nseq * l, cc), F32)
    init = [] if h0 is None else [h0]
    return pl.pallas_call(
        functools.partial(_rwkv_chunk_kernel, has_init=h0 is not None),
        grid=(nseq, nc),
        in_specs=specs(fwd, 0) + specs(bwd, 1) + [hspec] * len(init),
        out_specs=[pl.BlockSpec((CHUNK, cc), lambda s, g: (s * nc + g, 0)),
                   pl.BlockSpec((CHUNK, cc), lambda s, g: (s * nc + nc - 1 - g, 0)),
                   pl.BlockSpec((1,) + fshape[1:], lambda s, g: (s, 0, 0, 0, 0))],
        out_shape=[yshape, yshape, jax.ShapeDtypeStruct(fshape, F32)],
        scratch_shapes=[pltpu.VMEM(hshape[1:], F32), pltpu.VMEM((2, CHUNK, cc), F32)],
        compiler_params=_cparams(("parallel", "arbitrary")),
        name="rwkv_chunks",
    )(p, p, kk, lw, kd, bd, p, p, kk, lw, kd, bd, *init)


def _state_to_wide(s):
    nseq, _, nh, n, _ = s.shape
    g = HEADS_PER_UNIT
    ht = s.reshape(nseq, 2, nh // g, g, n, n)
    z = jnp.zeros_like(ht[:, :, :, 0])
    rows = [jnp.concatenate([ht[:, :, :, a] if a == b else z for b in range(g)], axis=-1) for a in range(g)]
    return jnp.concatenate(rows, axis=-2)


def _rwkv_post_kernel(yf_ref, yb_ref, r_ref, v_ref, kd_ref, g_ref, rk_ref, gw_ref, gb_ref, e2_ref, o_ref):
    e2 = e2_ref[...]
    y = yf_ref[...] + yb_ref[...]
    inv_n = 1.0 / N_C
    d = y - _head_sums(y, e2) * inv_n
    var = _head_sums(d * d, e2) * inv_n
    yn = d * lax.rsqrt(var + GN_EPS) * gw_ref[...] + gb_ref[...]
    bonus = (_head_sums(r_ref[...].astype(F32) * (kd_ref[0] + kd_ref[1]) * rk_ref[...], e2)
             * v_ref[...].astype(F32))
    o_ref[...] = ((yn + bonus) * _silu(g_ref[...].astype(F32))).astype(BF16)


def _rwkv_post(yf, yb, p, kd, row0, rc_col0, vc_col0, gc_col0, r_k, gn_w, gn_b, e2, tm):
    t, cc = yf.shape
    rb0 = row0 // tm
    vec = pl.BlockSpec((1, cc), lambda i: (0, 0))
    own = pl.BlockSpec((tm, cc), lambda i: (i, 0))
    return pl.pallas_call(
        _rwkv_post_kernel,
        grid=(t // tm,),
        in_specs=[own, own,
                  pl.BlockSpec((tm, cc), lambda i: (rb0 + i, rc_col0 // cc)),
                  pl.BlockSpec((tm, cc), lambda i: (rb0 + i, vc_col0 // cc)),
                  pl.BlockSpec((2, tm, cc), lambda i: (0, rb0 + i, 0)),
                  pl.BlockSpec((tm, cc), lambda i: (rb0 + i, gc_col0 // cc)),
                  vec, vec, vec,
                  pl.BlockSpec((LANES, LANES), lambda i: (0, 0))],
        out_specs=own,
        out_shape=jax.ShapeDtypeStruct((t, cc), BF16),
        compiler_params=_cparams(("parallel",)),
        name="rwkv_post",
    )(yf, yb, p, p, kd, p, r_k, gn_w, gn_b, e2)


def _merge_kernel(bap_ref, bas_ref, bb_ref, bcp_ref, bcs_ref, m0_ref, m1_ref, m2_ref, wa_ref, wb_ref, wc_ref, o_ref,
                  *, n_prompt_tiles):
    in_prompt = pl.program_id(0) < n_prompt_tiles
    ba = jnp.where(in_prompt, bap_ref[...], bas_ref[...])
    bc = jnp.where(in_prompt, bcp_ref[...], bcs_ref[...])
    o_ref[...] = (jax.nn.sigmoid(m0_ref[...].astype(F32)) * _mm(ba, wa_ref[...])
                  + jax.nn.sigmoid(m1_ref[...].astype(F32)) * _mm(bb_ref[...], wb_ref[...])
                  + jax.nn.sigmoid(m2_ref[...].astype(F32)) * _mm(bc, wc_ref[...])).astype(BF16)


def _merge(ba_p, ba_s, bb, bc_p, bc_s, p, mg_col0, wa, wb, wc, tm, tn):
    t, wbr = bb.shape
    d = wa.shape[1]
    nj = d // tn
    j0 = mg_col0 // tn
    npt = ba_p.shape[0] // tm
    prm = pl.BlockSpec((tm, wbr), lambda i, j: (jnp.minimum(i, npt - 1), 0))
    smp = pl.BlockSpec((tm, wbr), lambda i, j: (jnp.maximum(i - npt, 0), 0))
    wspec = pl.BlockSpec((wbr, tn), lambda i, j: (0, j))
    return pl.pallas_call(
        functools.partial(_merge_kernel, n_prompt_tiles=npt),
        grid=(t // tm, nj),
        in_specs=[prm, smp, pl.BlockSpec((tm, wbr), lambda i, j: (i, 0)), prm, smp,
                  pl.BlockSpec((tm, tn), lambda i, j: (i, j0 + j)),
                  pl.BlockSpec((tm, tn), lambda i, j: (i, j0 + nj + j)),
                  pl.BlockSpec((tm, tn), lambda i, j: (i, j0 + 2 * nj + j)),
                  wspec, wspec, wspec],
        out_specs=pl.BlockSpec((tm, tn), lambda i, j: (i, j)),
        out_shape=jax.ShapeDtypeStruct((t, d), BF16),
        compiler_params=_cparams(("parallel", "arbitrary")),
        name="branch_merge",
    )(ba_p, ba_s, bb, bc_p, bc_s, p, p, p, wa, wb, wc)


def _resid_kernel(m_ref, wo_ref, x_ref, mod_ref, fw_ref, *o_refs, n_prompt_tiles):
    x = x_ref[...] + mod_ref[0, 2:3, :] * _mm(m_ref[...], wo_ref[...])
    if len(o_refs) == 1:
        o_refs[0][...] = x
        return
    x = x * lax.rsqrt(jnp.mean(x * x, axis=-1, keepdims=True) + EPS) * fw_ref[...]
    in_prompt = pl.program_id(0) < n_prompt_tiles

    @pl.when(in_prompt)
    def _():
        o_refs[0][...] = x

    @pl.when(jnp.logical_not(in_prompt))
    def _():
        o_refs[1][...] = x


def _out_proj(merged, wo, x, mod, final_w, cond_of_tile, tm, tp, final):
    t, d = x.shape
    npt = tp // tm
    row = pl.BlockSpec((tm, d), lambda i: (i, 0))
    if final:
        out_specs = [pl.BlockSpec((tm, d), lambda i: (jnp.minimum(i, npt - 1), 0)),
                     pl.BlockSpec((tm, d), lambda i: (jnp.maximum(i - npt, 0), 0))]
        out_shape = [jax.ShapeDtypeStruct((tp, d), F32), jax.ShapeDtypeStruct((t - tp, d), F32)]
    else:
        out_specs, out_shape = row, jax.ShapeDtypeStruct((t, d), F32)
    return pl.pallas_call(
        functools.partial(_resid_kernel, n_prompt_tiles=npt),
        grid=(t // tm,),
        in_specs=[row,
                  pl.BlockSpec((d, d), lambda i: (0, 0)),
                  row,
                  pl.BlockSpec((1, 3, d), lambda i: (cond_of_tile(i), 0, 0)),
                  pl.BlockSpec((1, d), lambda i: (0, 0))],
        out_specs=out_specs,
        out_shape=out_shape,
        compiler_params=_cparams(("arbitrary",)),
        name="out_proj",
    )(merged, wo, x, mod, final_w)


def _largest_tile(n, cap):
    t = cap
    while n % t:
        t //= 2
    return t


def kernel(x_prompt, x_sample, cache_k, cache_v, state_rwkv, c, c_ctx, w_ada, b_ada, norm_w, w_in,
           lambda_q1, lambda_k1, lambda_q2, lambda_k2, subln_w, conv_w, conv_b, conv_ln_w, conv_ln_b,
           rwkv_w0, rwkv_w_up, rwkv_a0, rwkv_a_up, rwkv_k_k, rwkv_k_a, rwkv_r_k, rwkv_gn_w, rwkv_gn_b,
           w_br_a, w_br_b, w_br_c, w_out, final_norm_w):
    bp, lp_, d = x_prompt.shape
    bs, ls, _ = x_sample.shape
    depth = w_ada.shape[0]
    past = cache_k.shape[2]
    tp, ts = bp * lp_, bs * ls
    n_ha = d // 256
    w_a = n_ha * HEAD_A
    c_b = d // 2
    n_hc = d // (2 * N_C)
    c_c = n_hc * N_C
    n_mg = 3 * d
    assert n_ha * 2 * DH_A == w_a and w_a == c_b == c_c
    assert bs + 1 <= SUBLANES

    col = {}
    off = 0
    for name, width in (("mg", n_mg), ("q", w_a), ("k", w_a), ("v", w_a), ("g_a", w_a), ("glu", 2 * c_b),
                        ("g_b", c_b), ("r_c", c_c), ("k_c", c_c), ("v_c", c_c), ("g_c", c_c),
                        ("xw", 2 * LORA), ("xa", 2 * LORA)):
        col[name] = off
        off += width
    in_cols = off
    n_lead = in_cols - n_mg

    tm_in = _largest_tile(math.gcd(tp, ls), 1024)
    tm_out = _largest_tile(math.gcd(tp, ls), 512)
    tn_in = 768 if in_cols % 768 == 0 else 256
    tm_mg = _largest_tile(tp + ts, 1024)
    tn_mg = math.gcd(col["mg"], 512)
    tt = _largest_tile(math.gcd(lp_, ls), 256)
    tm_rw = _largest_tile(math.gcd(tp, ts), 256)

    def cond_of(tm):
        npt = tp // tm
        return lambda i: jnp.where(i < npt, 0, 1 + ((i - npt) * tm) // ls)

    x = jnp.concatenate([x_prompt.reshape(tp, d), x_sample.reshape(ts, d)], axis=0)
    cond8 = jnp.zeros((SUBLANES, d), F32).at[0].set(c_ctx).at[1:1 + bs].set(c)
    rope_tabs = _rope_tables(ls)
    blk = jnp.arange(LANES) // N_C
    e2 = (blk[:, None] == blk[None, :]).astype(BF16)
    zpad = jnp.zeros((LORA, c_c), F32)

    ks, vs, ss = [], [], []
    for l in range(depth):
        lam_init = 0.8 - 0.6 * math.exp(-0.3 * l)
        mod = _ada(cond8, w_ada, b_ada[:, None, :], l).reshape(SUBLANES, 3, d)
        w_rot = jnp.concatenate([w_in[l][:, n_lead:], w_in[l][:, :n_lead]], axis=1).astype(BF16)
        p = _inproj(x, mod, norm_w[l][None, :], w_rot, cond_of(tm_in), tm_in, tn_in)

        lam_p = jnp.stack([lambda_q1[l], lambda_k1[l], lambda_q2[l], lambda_k2[l]], axis=0)
        sw = subln_w[l][None, :]
        cq, ck, cv, cg = (col[n] // w_a for n in ("q", "k", "v", "g_a"))
        tq_p = _largest_tile(lp_, 256)
        npq = lp_ // tq_p
        ba_p = _attention(
            lam_p, sw,
            p, lambda b, h, i: (b * npq + i, cq),
            p, lambda b, h, i: (b, ck),
            p, lambda b, h, i: (b, cv),
            p, lambda b, h, i: (b * npq + i, cg),
            None, bp, n_ha, n_ha, lp_, lp_, tq_p, lam_init, Q_SCALE)
        qk_rot = _rope_qk(p, rope_tabs, tp, ts, ls, col["q"], n_ha)
        cache = (cache_k[:, l].reshape(bs * past, w_a).astype(BF16),
                 cache_v[:, l].reshape(bs * past, w_a).astype(BF16), past)
        cv1, cg1 = col["v"] // HEAD_A, col["g_a"] // HEAD_A
        tq_s = _largest_tile(ls, 1024)
        nsq = ls // tq_s
        ba_s = _attention(
            lam_p, sw,
            qk_rot, lambda b, h, i: (b * nsq + i, h),
            qk_rot, lambda b, h, i: (b, n_ha + h),
            p, lambda b, h, i: (tp // ls + b, cv1 + h),
            p, lambda b, h, i: (tp // tq_s + b * nsq + i, cg1 + h),
            cache, bs, n_ha, 1, ls, ls, tq_s, lam_init, 1.0)

        bb = _conv(p, conv_w[l], conv_b[l][None, :], conv_ln_w[l][None, :], conv_ln_b[l][None, :],
                   col["glu"], col["g_b"], tt, tp // tt, lp_ // tt, ls // tt)

        wup_pad = jnp.stack([jnp.concatenate([rwkv_w_up[l, 0], zpad], 0), jnp.concatenate([zpad, rwkv_w_up[l, 1]], 0)])
        aup_pad = jnp.stack([jnp.concatenate([rwkv_a_up[l, 0], zpad], 0), jnp.concatenate([zpad, rwkv_a_up[l, 1]], 0)])
        kk, lw, kd, bd = _rwkv_prep(p, col["k_c"], col["xw"], col["xa"], rwkv_w0[l], wup_pad, rwkv_a0[l], aup_pad,
                                    rwkv_k_k[l][None, :], rwkv_k_a[l][None, :], e2, tm_rw)
        yf_p, yb_p, hfin_p = _rwkv_chunks(p, kk, lw, kd, bd, None, 0, bp, lp_, col["r_c"], col["v_c"])
        ss.append(hfin_p)
        yf_s, yb_s, _ = _rwkv_chunks(p, kk, lw, kd, bd, _state_to_wide(state_rwkv[:, l]), tp, bs, ls,
                                     col["r_c"], col["v_c"])
        post_args = (col["r_c"], col["v_c"], col["g_c"], rwkv_r_k[l].reshape(1, c_c),
                     rwkv_gn_w[l][None, :], rwkv_gn_b[l][None, :], e2, tm_rw)
        bc_p = _rwkv_post(yf_p, yb_p, p, kd, 0, *post_args)
        bc_s = _rwkv_post(yf_s, yb_s, p, kd, tp, *post_args)

        merged = _merge(ba_p, ba_s, bb, bc_p, bc_s, p, col["mg"], w_br_a[l].astype(BF16), w_br_b[l].astype(BF16), w_br_c[l].astype(BF16),
                        tm_mg, tn_mg)
        x = _out_proj(merged, w_out[l].astype(BF16), x, mod, final_norm_w[None, :], cond_of(tm_out), tm_out, tp,
                      final=(l == depth - 1))

        ks.append(p[:tp, col["k"]:col["k"] + w_a].astype(F32).reshape(bp, lp_, n_ha, 2, DH_A))
        vs.append(p[:tp, col["v"]:col["v"] + w_a].astype(F32).reshape(bp, lp_, n_ha, HEAD_A))

    y_prompt = x[0].reshape(bp, lp_, d)
    y_sample = x[1].reshape(bs, ls, d)
    return (y_prompt, y_sample, jnp.stack(ks, axis=1), jnp.stack(vs, axis=1), jnp.stack(ss, axis=1))
```

```python
import functools
import math

import jax
import jax.numpy as jnp
from jax import lax
from jax.experimental import pallas as pl
from jax.experimental.pallas import tpu as pltpu

F32 = jnp.float32
BF16 = jnp.bfloat16

HEAD_A = 128
DH_A = 64
CONV_W = 31
CONV_HALO = 16
N_C = 64
LORA = 64
GRID_W = 64
ROPE_BASE = 10000.0
EPS = 1e-6
LN_EPS = 1e-5
GN_EPS = 64e-5
Q_SCALE = DH_A ** -0.5 * math.log2(math.e)

LANES = 128
SUBLANES = 8
MIB = 1024 * 1024


def _cparams(semantics, vmem_mib=48):
    return pltpu.CompilerParams(dimension_semantics=semantics, vmem_limit_bytes=vmem_mib * MIB)


def _split_bf16(x):
    hi = x.astype(BF16)
    lo = (x - hi.astype(F32)).astype(BF16)
    return hi, lo


_NT = (((1,), (1,)), ((), ()))


def _mm(a, b):
    return jnp.dot(a, b, preferred_element_type=F32)


def _dot3(a, b):
    ah, al = _split_bf16(a)
    bh, bl = _split_bf16(b)
    return _mm(ah, bh) + _mm(ah, bl) + _mm(al, bh)


def _silu(x):
    return x * jax.nn.sigmoid(x)


def _run_lockstep(gens):
    last = [None] * len(gens)
    live = list(range(len(gens)))
    while live:
        nxt = []
        for i in live:
            try:
                last[i] = next(gens[i])
                nxt.append(i)
            except StopIteration:
                pass
        live = nxt
    return last


def _head_sums(x, e2):
    xb = x.astype(BF16)
    parts = [_mm(xb[:, c * LANES:(c + 1) * LANES], e2) for c in range(x.shape[1] // LANES)]
    return jnp.concatenate(parts, axis=1)


def _ada_kernel(c_ref, w_ref, b_ref, o_ref):
    o_ref[...] = _dot3(_silu(c_ref[...]), w_ref[...]) + b_ref[...]


def _ada(cond8, w, b, layer):
    _, d, n = w.shape
    tn = 512
    return pl.pallas_call(
        _ada_kernel,
        grid=(n // tn,),
        in_specs=[pl.BlockSpec((SUBLANES, d), lambda j: (0, 0)),
                  pl.BlockSpec((None, d, tn), lambda j: (layer, 0, j)),
                  pl.BlockSpec((None, 1, tn), lambda j: (layer, 0, j))],
        out_specs=pl.BlockSpec((SUBLANES, tn), lambda j: (0, j)),
        out_shape=jax.ShapeDtypeStruct((SUBLANES, n), F32),
        compiler_params=_cparams(("arbitrary",)),
        name="ada_mod",
    )(cond8, w, b)


def _inproj_kernel(x_ref, mod_ref, nw_ref, w_ref, o_ref, h_scr):
    @pl.when(pl.program_id(1) == 0)
    def _():
        x = x_ref[...]
        y = x * lax.rsqrt(jnp.mean(x * x, axis=-1, keepdims=True) + EPS) * nw_ref[...]
        h = y * (1.0 + mod_ref[0, 1:2, :]) + mod_ref[0, 0:1, :]
        h_scr[...] = h.astype(BF16)

    o_ref[...] = _mm(h_scr[...], w_ref[...]).astype(o_ref.dtype)


def _inproj(x, mod, norm_w, w_bf16, cond_of_tile, tm, tn):
    t, d = x.shape
    n = w_bf16.shape[1]
    return pl.pallas_call(
        _inproj_kernel,
        grid=(t // tm, n // tn),
        in_specs=[pl.BlockSpec((tm, d), lambda i, j: (i, 0)),
                  pl.BlockSpec((1, 3, d), lambda i, j: (cond_of_tile(i), 0, 0)),
                  pl.BlockSpec((1, d), lambda i, j: (0, 0)),
                  pl.BlockSpec((d, tn), lambda i, j: (0, j))],
        out_specs=pl.BlockSpec((tm, tn), lambda i, j: (i, j)),
        out_shape=jax.ShapeDtypeStruct((t, n), BF16),
        scratch_shapes=[pltpu.VMEM((tm, d), BF16)],
        compiler_params=_cparams(("parallel", "arbitrary")),
        name="inproj",
    )(x, mod, norm_w, w_bf16)


def _rope_tab_kernel(inv_ref, c_ref, s1_ref, s2_ref, *, tr):
    t = lax.broadcasted_iota(jnp.int32, (tr, LANES), 0) + pl.program_id(0) * tr
    lane = lax.broadcasted_iota(jnp.int32, (tr, LANES), 1)
    row = (t // GRID_W).astype(F32)
    col = (t % GRID_W).astype(F32)
    pos = jnp.where((lane % DH_A) >= DH_A // 2, col, row)
    ang = pos * inv_ref[...]
    c = jnp.cos(ang)
    s = jnp.sin(ang)
    first = (lane % (DH_A // 2)) < DH_A // 4
    c_ref[...] = c
    s1_ref[...] = jnp.where(first, -s, 0.0)
    s2_ref[...] = jnp.where(first, 0.0, s)


def _rope_tables(ls):
    tr = min(ls, 512)
    quarter = DH_A // 4
    inv = ROPE_BASE ** (-jnp.arange(quarter, dtype=F32) / quarter)
    inv = jnp.tile(inv, LANES // quarter)[None, :]
    shp = jax.ShapeDtypeStruct((ls, LANES), F32)
    spec = pl.BlockSpec((tr, LANES), lambda i: (i, 0))
    return pl.pallas_call(
        functools.partial(_rope_tab_kernel, tr=tr),
        grid=(ls // tr,),
        in_specs=[pl.BlockSpec((1, LANES), lambda i: (0, 0))],
        out_specs=[spec, spec, spec],
        out_shape=[shp, shp, shp],
        compiler_params=_cparams(("arbitrary",)),
        name="rope_tables",
    )(inv)


def _rope_kernel(x_ref, c_ref, s1_ref, s2_ref, o_ref, *, n_q):
    quarter = DH_A // 4
    c, s1, s2 = c_ref[...], s1_ref[...], s2_ref[...]
    for j in range(x_ref.shape[1] // LANES):
        x = x_ref[:, j * LANES:(j + 1) * LANES].astype(F32)
        y = x * c + pltpu.roll(x, LANES - quarter, 1) * s1 + pltpu.roll(x, quarter, 1) * s2
        if j < n_q:
            y = y * Q_SCALE
        o_ref[:, j * LANES:(j + 1) * LANES] = y.astype(BF16)


def _rope_qk(p, tabs, row0, ts, ls, col0, n_heads):
    tr = min(ls, 512)
    width = 2 * n_heads * LANES
    tab_spec = pl.BlockSpec((tr, LANES), lambda i: (i % (ls // tr), 0))
    return pl.pallas_call(
        functools.partial(_rope_kernel, n_q=n_heads),
        grid=(ts // tr,),
        in_specs=[pl.BlockSpec((tr, width), lambda i: (row0 // tr + i, col0 // width)),
                  tab_spec, tab_spec, tab_spec],
        out_specs=pl.BlockSpec((tr, width), lambda i: (i, 0)),
        out_shape=jax.ShapeDtypeStruct((ts, width), BF16),
        compiler_params=_cparams(("parallel",)),
        name="rope_qk",
    )(p, *tabs)


ATTN_SUB = 128
ATTN_CHAINS = 4


def _attn_kernel(lam_ref, sw_ref, q_ref, k_ref, v_ref, g_ref, *rest, lam_init, q_scale, has_cache):
    if has_cache:
        kc_ref, vc_ref, o_ref = rest
    else:
        (o_ref,) = rest
    lp = lam_ref[...]
    lam = (jnp.exp(jnp.sum(lp[0:1, :] * lp[1:2, :], axis=1, keepdims=True))
           - jnp.exp(jnp.sum(lp[2:3, :] * lp[3:4, :], axis=1, keepdims=True)) + lam_init)
    tq = q_ref.shape[0]
    rs = min(ATTN_SUB, tq)
    lane = lax.broadcasted_iota(jnp.int32, (rs, HEAD_A), 1)
    zero = jnp.zeros((rs, HEAD_A), BF16)

    def head_kv(hh):
        hs = slice(hh * HEAD_A, (hh + 1) * HEAD_A)
        k = k_ref[:, hs].astype(BF16)
        v = v_ref[:, hs].astype(BF16)
        if has_cache:
            k = jnp.concatenate([kc_ref[:, hs], k], axis=0)
            v = jnp.concatenate([vc_ref[:, hs], v], axis=0)
        return k, jnp.concatenate([v, jnp.ones_like(v)], axis=1)

    def logits(hh, i, k):
        q = q_ref[i * rs:(i + 1) * rs, hh * HEAD_A:(hh + 1) * HEAD_A]
        if q_scale != 1.0:
            q = q.astype(F32) * q_scale
        q = q.astype(BF16)
        lhs = jnp.concatenate([jnp.where(lane < DH_A, q, zero), jnp.where(lane < DH_A, zero, q)], axis=0)
        return lax.dot_general(lhs, k, _NT, preferred_element_type=F32)

    kv = {}

    def kv_of(hh):
        if hh not in kv:
            kv[hh] = head_kv(hh)
        return kv[hh]

    def chain(items):
        s = logits(*items[0], kv_of(items[0][0])[0])
        yield
        for idx, (hh, i) in enumerate(items):
            if idx + 1 < len(items):
                s_next = logits(*items[idx + 1], kv_of(items[idx + 1][0])[0])
                yield
            e = jnp.exp2(s - jnp.max(s, axis=-1, keepdims=True)).astype(BF16)
            oa = _mm(e, kv_of(hh)[1])
            yield
            on = oa[:, :HEAD_A] * (1.0 / oa[:, HEAD_A:HEAD_A + 1])
            o = on[:rs] - lam * on[rs:]
            o = o * lax.rsqrt(jnp.mean(o * o, axis=-1, keepdims=True) + EPS) * sw_ref[...] * (1.0 - lam_init)
            rows, hs = slice(i * rs, (i + 1) * rs), slice(hh * HEAD_A, (hh + 1) * HEAD_A)
            o_ref[rows, hs] = (o * _silu(g_ref[rows, hs].astype(F32))).astype(BF16)
            if idx + 1 < len(items):
                s = s_next

    n_heads_blk = q_ref.shape[1] // HEAD_A
    n_chains = min(n_heads_blk, ATTN_CHAINS)
    items = [(hh, i) for hh in range(n_heads_blk) for i in range(tq // rs)]
    per = len(items) // n_chains
    _run_lockstep([chain(items[c * per:(c + 1) * per]) for c in range(n_chains)])


def _attention(lam_p, subln_w, q_arr, q_map, k_arr, k_map, v_arr, v_map, g_arr, g_map, cache,
               n_b, n_heads, hps, lq, lk_new, tq, lam_init, q_scale):
    nq = lq // tq
    wide = hps * HEAD_A
    in_specs = [pl.BlockSpec(lam_p.shape, lambda b, h, i: (0, 0)),
                pl.BlockSpec((1, HEAD_A), lambda b, h, i: (0, 0)),
                pl.BlockSpec((tq, wide), q_map),
                pl.BlockSpec((lk_new, wide), k_map),
                pl.BlockSpec((lk_new, wide), v_map),
                pl.BlockSpec((tq, wide), g_map)]
    args = [lam_p, subln_w, q_arr, k_arr, v_arr, g_arr]
    if cache is not None:
        kc, vc, past = cache
        cspec = pl.BlockSpec((past, wide), lambda b, h, i: (b, h))
        in_specs += [cspec, cspec]
        args += [kc, vc]
    return pl.pallas_call(
        functools.partial(_attn_kernel, lam_init=lam_init, q_scale=q_scale, has_cache=cache is not None),
        grid=(n_b, n_heads // hps, nq),
        in_specs=in_specs,
        out_specs=pl.BlockSpec((tq, wide), lambda b, h, i: (b * nq + i, h)),
        out_shape=jax.ShapeDtypeStruct((n_b * lq, n_heads * HEAD_A), BF16),
        compiler_params=_cparams(("parallel", "parallel", "arbitrary")),
        name="diff_attn",
    )(*args)


def _conv_kernel(u_ref, up_ref, un_ref, g_ref, cw_ref, cb_ref, lw_ref, lb_ref, o_ref, zp_scr, acc_scr, zs_scr,
                 *, tt, n_prompt_blocks, blocks_per_prompt_seq, blocks_per_sample_seq):
    cb_ = u_ref.shape[1] // 2
    blk = pl.program_id(0)
    in_prompt = blk < n_prompt_blocks
    pos = jnp.where(in_prompt, blk % blocks_per_prompt_seq, (blk - n_prompt_blocks) % blocks_per_sample_seq)
    per_seq = jnp.where(in_prompt, blocks_per_prompt_seq, blocks_per_sample_seq)
    keep_prev = (pos > 0).astype(F32)
    keep_next = (pos < per_seq - 1).astype(F32)

    def glu(ref):
        u = ref[...].astype(F32)
        return u[:, :cb_] * jax.nn.sigmoid(u[:, cb_:])

    h = CONV_HALO
    zp_scr[pl.ds(0, h), :] = glu(up_ref) * keep_prev
    zp_scr[pl.ds(h, tt), :] = glu(u_ref)
    zp_scr[pl.ds(h + tt, h), :] = glu(un_ref) * keep_next

    rc = 64
    off = h - CONV_W // 2
    n_sh = zs_scr.shape[1]

    def lane_chunk(c, carry):
        ls = pl.ds(pl.multiple_of(c * LANES, LANES), LANES)
        w = cw_ref[:, ls]
        for s in range(SUBLANES):
            zs_scr[s] = zp_scr[pl.ds(s, n_sh), ls]
        for r0 in range(0, tt, rc):
            acc = jnp.zeros((rc, LANES), F32)
            for tau in range(CONV_W):
                o = tau + off
                acc = acc + zs_scr[o % SUBLANES, pl.ds(r0 + o - o % SUBLANES, rc), :] * w[tau:tau + 1, :]
            acc_scr[pl.ds(r0, rc), ls] = acc
        return carry

    lax.fori_loop(0, cb_ // LANES, lane_chunk, 0)

    z = acc_scr[...] + cb_ref[...]
    mu = jnp.mean(z, axis=-1, keepdims=True)
    d = z - mu
    var = jnp.mean(d * d, axis=-1, keepdims=True)
    zn = d * lax.rsqrt(var + LN_EPS) * lw_ref[...] + lb_ref[...]
    o_ref[...] = (_silu(zn) * _silu(g_ref[...].astype(F32))).astype(BF16)


def _conv(p, conv_w, conv_b, ln_w, ln_b, glu_col0, gb_col0, tt, n_prompt_blocks,
          blocks_per_prompt_seq, blocks_per_sample_seq):
    t = p.shape[0]
    cb_ = conv_w.shape[1]
    nblk = t // tt
    hb = tt // CONV_HALO
    n_hblk = t // CONV_HALO
    vec = pl.BlockSpec((1, cb_), lambda i: (0, 0))
    return pl.pallas_call(
        functools.partial(_conv_kernel, tt=tt, n_prompt_blocks=n_prompt_blocks,
                          blocks_per_prompt_seq=blocks_per_prompt_seq,
                          blocks_per_sample_seq=blocks_per_sample_seq),
        grid=(nblk,),
        in_specs=[pl.BlockSpec((tt, 2 * cb_), lambda i: (i, glu_col0 // (2 * cb_))),
                  pl.BlockSpec((CONV_HALO, 2 * cb_), lambda i: (jnp.maximum(i * hb - 1, 0), glu_col0 // (2 * cb_))),
                  pl.BlockSpec((CONV_HALO, 2 * cb_),
                               lambda i: (jnp.minimum((i + 1) * hb, n_hblk - 1), glu_col0 // (2 * cb_))),
                  pl.BlockSpec((tt, cb_), lambda i: (i, gb_col0 // cb_)),
                  pl.BlockSpec(conv_w.shape, lambda i: (0, 0)),
                  vec, vec, vec],
        out_specs=pl.BlockSpec((tt, cb_), lambda i: (i, 0)),
        out_shape=jax.ShapeDtypeStruct((t, cb_), BF16),
        scratch_shapes=[pltpu.VMEM((tt + 2 * CONV_HALO, cb_), F32), pltpu.VMEM((tt, cb_), F32),
                        pltpu.VMEM((SUBLANES, tt + 2 * CONV_HALO - SUBLANES, LANES), F32)],
        compiler_params=_cparams(("parallel",)),
        name="conformer_conv",
    )(p, p, p, p, conv_w, conv_b, ln_w, ln_b)


def _rwkv_prep_kernel(k_ref, xw_ref, xa_ref, w0_ref, wup_ref, a0_ref, aup_ref, kk_ref, ka_ref, e2_ref,
                      okk_ref, olw_ref, okd_ref, obd_ref):
    k = k_ref[...].astype(F32)
    e2 = e2_ref[...]
    kkr = k * kk_ref[...]
    kk = kkr / jnp.maximum(jnp.sqrt(_head_sums(kkr * kkr, e2)), 1e-12)
    okk_ref[...] = kk
    txw = jnp.tanh(xw_ref[...].astype(F32))
    xa = xa_ref[...].astype(F32)
    for d in range(2):
        olw_ref[d] = -math.exp(-0.5) * jax.nn.sigmoid(w0_ref[d:d + 1, :] + _dot3(txw, wup_ref[d]))
        a = jax.nn.sigmoid(a0_ref[d:d + 1, :] + _dot3(xa, aup_ref[d]))
        okd_ref[d] = k * (1.0 + (a - 1.0) * ka_ref[...])
        obd_ref[d] = kk * a


def _rwkv_prep(p, kc_col0, xw_col0, xa_col0, w0, wup_pad, a0, aup_pad, k_k, k_a, e2, tm):
    t = p.shape[0]
    cc = w0.shape[1]
    vec = pl.BlockSpec((1, cc), lambda i: (0, 0))
    vec2 = pl.BlockSpec((2, cc), lambda i: (0, 0))
    mat = pl.BlockSpec((2, LANES, cc), lambda i: (0, 0, 0))
    one = jax.ShapeDtypeStruct((t, cc), F32)
    two = jax.ShapeDtypeStruct((2, t, cc), F32)
    two_spec = pl.BlockSpec((2, tm, cc), lambda i: (0, i, 0))
    return pl.pallas_call(
        _rwkv_prep_kernel,
        grid=(t // tm,),
        in_specs=[pl.BlockSpec((tm, cc), lambda i: (i, kc_col0 // cc)),
                  pl.BlockSpec((tm, LANES), lambda i: (i, xw_col0 // LANES)),
                  pl.BlockSpec((tm, LANES), lambda i: (i, xa_col0 // LANES)),
                  vec2, mat, vec2, mat, vec, vec,
                  pl.BlockSpec((LANES, LANES), lambda i: (0, 0))],
        out_specs=[pl.BlockSpec((tm, cc), lambda i: (i, 0)), two_spec, two_spec, two_spec],
        out_shape=[one, two, two, two],
        compiler_params=_cparams(("parallel",)),
        name="rwkv_prep",
    )(p, p, p, w0, wup_pad, a0, aup_pad, k_k, k_a, e2)


CHUNK = 64
HEADS_PER_UNIT = 2
UNIT_W = HEADS_PER_UNIT * N_C
UNITS_PER_BODY = 8


def _mmb(a, b):
    return _mm(a.astype(BF16), b.astype(BF16))


def _chunk_unit(r, v, kk, lw, c, kd, bd, s0, reverse, head_masks, eye2, strict, incl, same_head):
    w2 = UNIT_W
    ctot = c[0:1, :] if reverse else c[CHUNK - 1:CHUNK, :]
    g_inv = jnp.exp(-c)
    g_rem = jnp.exp(ctot - c)
    rt = r * jnp.exp(c)
    bt = -(kk * jnp.exp(c - lw))

    def stack(x):
        xb = x.astype(BF16)
        z = jnp.zeros_like(xb)
        return jnp.concatenate([jnp.where(m, xb, z) for m in head_masks], axis=0)

    v_s = stack(v)
    s0b = s0.astype(BF16)
    lhs = jnp.concatenate([bt, rt], axis=0).astype(BF16)
    rhs = jnp.concatenate([stack(bd * g_inv), stack(kd * g_inv)], axis=0)
    nn = lax.dot_general(lhs, rhs, _NT, preferred_element_type=F32)
    yield None
    a_ab = jnp.where(strict, nn[:CHUNK, :w2], 0.0)
    a_bk = jnp.where(strict, nn[:CHUNK, w2:], 0.0)
    a_r = jnp.concatenate([jnp.where(incl, nn[CHUNK:, :w2], 0.0), jnp.where(incl, nn[CHUNK:, w2:], 0.0)], axis=1)
    p1 = _mm(a_bk.astype(BF16), v_s)
    t = eye2 + a_ab
    n = _mm(a_ab.astype(BF16), stack(a_ab))
    yield None
    for _ in range(int(math.log2(CHUNK)) - 2):
        prod = _mm(jnp.concatenate([n, t], axis=0).astype(BF16), stack(n))
        yield None
        t = t + prod[CHUNK:]
        n = prod[:CHUNK]
    t = t + _mm(t.astype(BF16), stack(n))
    yield None
    wx = _mm(t.astype(BF16), jnp.concatenate([stack(bt), stack(p1)], axis=1))
    yield None
    w, x = wx[:, :w2], wx[:, w2:]
    rhs_big = jnp.concatenate(
        [jnp.concatenate([stack(w), stack(x)], axis=1),
         jnp.concatenate([jnp.zeros((w2, w2), BF16), v_s], axis=1)], axis=0)
    qy = _mm(a_r.astype(BF16), rhs_big)
    w_pad = jnp.concatenate([w, jnp.zeros_like(w)], axis=0).astype(BF16)
    uv_t = jnp.concatenate([x, v], axis=0).T + lax.dot_general(s0b, w_pad, _NT, preferred_element_type=F32)
    yield None
    qhat = rt + qy[:, :w2]
    y = lax.dot_general(qhat.astype(BF16), s0b, _NT, preferred_element_type=F32) + qy[:, w2:]
    upd = _mmb(uv_t, jnp.concatenate([bd * g_rem, kd * g_rem], axis=0))
    yield s0 * jnp.exp(ctot) + jnp.where(same_head, upd, 0.0), y


def _rwkv_chunk_kernel(rf_ref, vf_ref, kkf_ref, lwf_ref, kdf_ref, bdf_ref,
                       rb_ref, vb_ref, kkb_ref, lwb_ref, kdb_ref, bdb_ref, *rest, has_init):
    if has_init:
        h0_ref, yf_ref, yb_ref, hfin_ref, h_scr, c_scr = rest
    else:
        yf_ref, yb_ref, hfin_ref, h_scr, c_scr = rest

    @pl.when(pl.program_id(1) == 0)
    def _():
        h_scr[...] = h0_ref[0] if has_init else jnp.zeros(h_scr.shape, F32)

    ti = lax.broadcasted_iota(jnp.int32, (CHUNK, CHUNK), 0)
    si = lax.broadcasted_iota(jnp.int32, (CHUNK, CHUNK), 1)

    def cumsum(tri, x):
        hi = x.astype(BF16)
        r1 = x - hi.astype(F32)
        mid = r1.astype(BF16)
        lo = (r1 - mid.astype(F32)).astype(BF16)
        return _mm(tri, hi) + _mm(tri, mid) + _mm(tri, lo)

    c_scr[0] = cumsum((si <= ti).astype(BF16), lwf_ref[...])
    c_scr[1] = cumsum((si >= ti).astype(BF16), lwb_ref[...])

    w2 = UNIT_W
    tpos = lax.broadcasted_iota(jnp.int32, (CHUNK, w2), 0)
    lane = lax.broadcasted_iota(jnp.int32, (CHUNK, w2), 1)
    spos = lane % CHUNK
    eye2 = (spos == tpos).astype(F32)
    head_masks = [lane // N_C == hh for hh in range(HEADS_PER_UNIT)]
    same_head = (lax.broadcasted_iota(jnp.int32, (w2, w2), 0) // N_C
                 == lax.broadcasted_iota(jnp.int32, (w2, w2), 1) // N_C)
    dirs = ((rf_ref, vf_ref, kkf_ref, lwf_ref, kdf_ref, bdf_ref, yf_ref, spos < tpos, spos <= tpos),
            (rb_ref, vb_ref, kkb_ref, lwb_ref, kdb_ref, bdb_ref, yb_ref, spos > tpos, spos >= tpos))

    def pair_group(pg, carry):
        units = []
        for pi in range(UNITS_PER_BODY):
            p = pg * UNITS_PER_BODY + pi
            ls = pl.ds(pl.multiple_of(p * w2, w2), w2)
            for d, (r_ref, v_ref, kk_ref, lw_ref, kd_ref, bd_ref, y_ref, strict, incl) in enumerate(dirs):
                args = (r_ref[:, ls].astype(F32), v_ref[:, ls].astype(F32), kk_ref[:, ls], lw_ref[:, ls], c_scr[d, :, ls],
                        kd_ref[:, ls], bd_ref[:, ls], h_scr[d, p])
                units.append((d, p, ls, y_ref, args, strict, incl))
        results = _run_lockstep([_chunk_unit(*args, d == 1, head_masks, eye2, strict, incl, same_head)
                                 for d, p, ls, y_ref, args, strict, incl in units])
        for (d, p, ls, y_ref, _, _, _), (h_new, y) in zip(units, results):
            h_scr[d, p] = h_new
            y_ref[:, ls] = y
        return carry

    lax.fori_loop(0, rf_ref.shape[1] // (w2 * UNITS_PER_BODY), pair_group, 0)

    @pl.when(pl.program_id(1) == pl.num_programs(1) - 1)
    def _():
        for d in range(2):
            for u in range(h_scr.shape[1]):
                for a in range(HEADS_PER_UNIT):
                    blk = slice(a * N_C, (a + 1) * N_C)
                    hfin_ref[0, d, u * HEADS_PER_UNIT + a] = h_scr[d, u, blk, blk]


def _rwkv_chunks(p, kk, lw, kd, bd, h0, row0, nseq, l, rc_col0, vc_col0):
    cc = kk.shape[1]
    nc = l // CHUNK
    rb0 = row0 // CHUNK
    nunit = cc // UNIT_W

    def fwd(s, g):
        return rb0 + s * nc + g

    def bwd(s, g):
        return rb0 + s * nc + nc - 1 - g

    def specs(rowf, d):
        return [pl.BlockSpec((CHUNK, cc), lambda s, g: (rowf(s, g), rc_col0 // cc)),
                pl.BlockSpec((CHUNK, cc), lambda s, g: (rowf(s, g), vc_col0 // cc)),
                pl.BlockSpec((CHUNK, cc), lambda s, g: (rowf(s, g), 0)),
                pl.BlockSpec((None, CHUNK, cc), lambda s, g: (d, rowf(s, g), 0)),
                pl.BlockSpec((None, CHUNK, cc), lambda s, g: (d, rowf(s, g), 0)),
                pl.BlockSpec((None, CHUNK, cc), lambda s, g: (d, rowf(s, g), 0))]

    hshape = (nseq, 2, nunit, UNIT_W, UNIT_W)
    fshape = (nseq, 2, cc // N_C, N_C, N_C)
    hspec = pl.BlockSpec((1,) + hshape[1:], lambda s, g: (s, 0, 0, 0, 0))
    yshape = jax.ShapeDtypeStruct((nseq * l, cc), F32)
    init = [] if h0 is None else [h0]
    return pl.pallas_call(
        functools.partial(_rwkv_chunk_kernel, has_init=h0 is not None),
        grid=(nseq, nc),
        in_specs=specs(fwd, 0) + specs(bwd, 1) + [hspec] * len(init),
        out_specs=[pl.BlockSpec((CHUNK, cc), lambda s, g: (s * nc + g, 0)),
                   pl.BlockSpec((CHUNK, cc), lambda s, g: (s * nc + nc - 1 - g, 0)),
                   pl.BlockSpec((1,) + fshape[1:], lambda s, g: (s, 0, 0, 0, 0))],
        out_shape=[yshape, yshape, jax.ShapeDtypeStruct(fshape, F32)],
        scratch_shapes=[pltpu.VMEM(hshape[1:], F32), pltpu.VMEM((2, CHUNK, cc), F32)],
        compiler_params=_cparams(("parallel", "arbitrary")),
        name="rwkv_chunks",
    )(p, p, kk, lw, kd, bd, p, p, kk, lw, kd, bd, *init)


def _state_to_wide(s):
    nseq, _, nh, n, _ = s.shape
    g = HEADS_PER_UNIT
    ht = s.reshape(nseq, 2, nh // g, g, n, n)
    z = jnp.zeros_like(ht[:, :, :, 0])
    rows = [jnp.concatenate([ht[:, :, :, a] if a == b else z for b in range(g)], axis=-1) for a in range(g)]
    return jnp.concatenate(rows, axis=-2)


def _rwkv_post_kernel(yf_ref, yb_ref, r_ref, v_ref, kd_ref, g_ref, rk_ref, gw_ref, gb_ref, e2_ref, o_ref):
    e2 = e2_ref[...]
    y = yf_ref[...] + yb_ref[...]
    inv_n = 1.0 / N_C
    d = y - _head_sums(y, e2) * inv_n
    var = _head_sums(d * d, e2) * inv_n
    yn = d * lax.rsqrt(var + GN_EPS) * gw_ref[...] + gb_ref[...]
    bonus = (_head_sums(r_ref[...].astype(F32) * (kd_ref[0] + kd_ref[1]) * rk_ref[...], e2)
             * v_ref[...].astype(F32))
    o_ref[...] = ((yn + bonus) * _silu(g_ref[...].astype(F32))).astype(BF16)


def _rwkv_post(yf, yb, p, kd, row0, rc_col0, vc_col0, gc_col0, r_k, gn_w, gn_b, e2, tm):
    t, cc = yf.shape
    rb0 = row0 // tm
    vec = pl.BlockSpec((1, cc), lambda i: (0, 0))
    own = pl.BlockSpec((tm, cc), lambda i: (i, 0))
    return pl.pallas_call(
        _rwkv_post_kernel,
        grid=(t // tm,),
        in_specs=[own, own,
                  pl.BlockSpec((tm, cc), lambda i: (rb0 + i, rc_col0 // cc)),
                  pl.BlockSpec((tm, cc), lambda i: (rb0 + i, vc_col0 // cc)),
                  pl.BlockSpec((2, tm, cc), lambda i: (0, rb0 + i, 0)),
                  pl.BlockSpec((tm, cc), lambda i: (rb0 + i, gc_col0 // cc)),
                  vec, vec, vec,
                  pl.BlockSpec((LANES, LANES), lambda i: (0, 0))],
        out_specs=own,
        out_shape=jax.ShapeDtypeStruct((t, cc), BF16),
        compiler_params=_cparams(("parallel",)),
        name="rwkv_post",
    )(yf, yb, p, p, kd, p, r_k, gn_w, gn_b, e2)


def _merge_kernel(bap_ref, bas_ref, bb_ref, bcp_ref, bcs_ref, m0_ref, m1_ref, m2_ref, wa_ref, wb_ref, wc_ref, o_ref,
                  *, n_prompt_tiles):
    in_prompt = pl.program_id(0) < n_prompt_tiles
    ba = jnp.where(in_prompt, bap_ref[...], bas_ref[...])
    bc = jnp.where(in_prompt, bcp_ref[...], bcs_ref[...])
    o_ref[...] = (jax.nn.sigmoid(m0_ref[...].astype(F32)) * _mm(ba, wa_ref[...])
                  + jax.nn.sigmoid(m1_ref[...].astype(F32)) * _mm(bb_ref[...], wb_ref[...])
                  + jax.nn.sigmoid(m2_ref[...].astype(F32)) * _mm(bc, wc_ref[...])).astype(BF16)


def _merge(ba_p, ba_s, bb, bc_p, bc_s, p, mg_col0, wa, wb, wc, tm, tn):
    t, wbr = bb.shape
    d = wa.shape[1]
    nj = d // tn
    j0 = mg_col0 // tn
    npt = ba_p.shape[0] // tm
    prm = pl.BlockSpec((tm, wbr), lambda i, j: (jnp.minimum(i, npt - 1), 0))
    smp = pl.BlockSpec((tm, wbr), lambda i, j: (jnp.maximum(i - npt, 0), 0))
    wspec = pl.BlockSpec((wbr, tn), lambda i, j: (0, j))
    return pl.pallas_call(
        functools.partial(_merge_kernel, n_prompt_tiles=npt),
        grid=(t // tm, nj),
        in_specs=[prm, smp, pl.BlockSpec((tm, wbr), lambda i, j: (i, 0)), prm, smp,
                  pl.BlockSpec((tm, tn), lambda i, j: (i, j0 + j)),
                  pl.BlockSpec((tm, tn), lambda i, j: (i, j0 + nj + j)),
                  pl.BlockSpec((tm, tn), lambda i, j: (i, j0 + 2 * nj + j)),
                  wspec, wspec, wspec],
        out_specs=pl.BlockSpec((tm, tn), lambda i, j: (i, j)),
        out_shape=jax.ShapeDtypeStruct((t, d), BF16),
        compiler_params=_cparams(("parallel", "arbitrary")),
        name="branch_merge",
    )(ba_p, ba_s, bb, bc_p, bc_s, p, p, p, wa, wb, wc)


def _resid_kernel(m_ref, wo_ref, x_ref, mod_ref, fw_ref, *o_refs, n_prompt_tiles):
    x = x_ref[...] + mod_ref[0, 2:3, :] * _mm(m_ref[...], wo_ref[...])
    if len(o_refs) == 1:
        o_refs[0][...] = x
        return
    x = x * lax.rsqrt(jnp.mean(x * x, axis=-1, keepdims=True) + EPS) * fw_ref[...]
    in_prompt = pl.program_id(0) < n_prompt_tiles

    @pl.when(in_prompt)
    def _():
        o_refs[0][...] = x

    @pl.when(jnp.logical_not(in_prompt))
    def _():
        o_refs[1][...] = x


def _out_proj(merged, wo, x, mod, final_w, cond_of_tile, tm, tp, final):
    t, d = x.shape
    npt = tp // tm
    row = pl.BlockSpec((tm, d), lambda i: (i, 0))
    if final:
        out_specs = [pl.BlockSpec((tm, d), lambda i: (jnp.minimum(i, npt - 1), 0)),
                     pl.BlockSpec((tm, d), lambda i: (jnp.maximum(i - npt, 0), 0))]
        out_shape = [jax.ShapeDtypeStruct((tp, d), F32), jax.ShapeDtypeStruct((t - tp, d), F32)]
    else:
        out_specs, out_shape = row, jax.ShapeDtypeStruct((t, d), F32)
    return pl.pallas_call(
        functools.partial(_resid_kernel, n_prompt_tiles=npt),
        grid=(t // tm,),
        in_specs=[row,
                  pl.BlockSpec((d, d), lambda i: (0, 0)),
                  row,
                  pl.BlockSpec((1, 3, d), lambda i: (cond_of_tile(i), 0, 0)),
                  pl.BlockSpec((1, d), lambda i: (0, 0))],
        out_specs=out_specs,
        out_shape=out_shape,
        compiler_params=_cparams(("arbitrary",)),
        name="out_proj",
    )(merged, wo, x, mod, final_w)


def _kv_out_kernel(k_ref, v_ref, *rest):
    ok_ref, ov_ref = rest[-2:]
    ok_ref[0] = k_ref[...].astype(F32)
    ov_ref[0] = v_ref[...].astype(F32)


def _kv_out(p, k_col0, v_col0, width, n_b, l, depth, layer, prev):
    shp = jax.ShapeDtypeStruct((n_b, depth, l, width), F32)
    ospec = pl.BlockSpec((1, None, l, width), lambda b: (b, layer, 0, 0))
    in_specs = [pl.BlockSpec((l, width), lambda b: (b, k_col0 // width)),
                pl.BlockSpec((l, width), lambda b: (b, v_col0 // width))]
    args, aliases = [p, p], {}
    if prev is not None:
        in_specs += [pl.BlockSpec(memory_space=pl.ANY)] * 2
        args += list(prev)
        aliases = {2: 0, 3: 1}
    return pl.pallas_call(
        _kv_out_kernel,
        grid=(n_b,),
        in_specs=in_specs,
        out_specs=[ospec, ospec],
        out_shape=[shp, shp],
        input_output_aliases=aliases,
        compiler_params=_cparams(("arbitrary",)),
        name="kv_out",
    )(*args)


def _largest_tile(n, cap):
    t = cap
    while n % t:
        t //= 2
    return t


def kernel(x_prompt, x_sample, cache_k, cache_v, state_rwkv, c, c_ctx, w_ada, b_ada, norm_w, w_in,
           lambda_q1, lambda_k1, lambda_q2, lambda_k2, subln_w, conv_w, conv_b, conv_ln_w, conv_ln_b,
           rwkv_w0, rwkv_w_up, rwkv_a0, rwkv_a_up, rwkv_k_k, rwkv_k_a, rwkv_r_k, rwkv_gn_w, rwkv_gn_b,
           w_br_a, w_br_b, w_br_c, w_out, final_norm_w):
    bp, lp_, d = x_prompt.shape
    bs, ls, _ = x_sample.shape
    depth = w_ada.shape[0]
    past = cache_k.shape[2]
    tp, ts = bp * lp_, bs * ls
    n_ha = d // 256
    w_a = n_ha * HEAD_A
    c_b = d // 2
    n_hc = d // (2 * N_C)
    c_c = n_hc * N_C
    n_mg = 3 * d
    assert n_ha * 2 * DH_A == w_a and w_a == c_b == c_c
    assert bs + 1 <= SUBLANES

    col = {}
    off = 0
    for name, width in (("mg", n_mg), ("q", w_a), ("k", w_a), ("v", w_a), ("g_a", w_a), ("glu", 2 * c_b),
                        ("g_b", c_b), ("r_c", c_c), ("k_c", c_c), ("v_c", c_c), ("g_c", c_c),
                        ("xw", 2 * LORA), ("xa", 2 * LORA)):
        col[name] = off
        off += width
    in_cols = off
    n_lead = in_cols - n_mg

    tm_in = _largest_tile(math.gcd(tp, ls), 1024)
    tm_out = _largest_tile(math.gcd(tp, ls), 512)
    tn_in = 768 if in_cols % 768 == 0 else 256
    tm_mg = _largest_tile(tp + ts, 1024)
    tn_mg = math.gcd(col["mg"], 512)
    tt = _largest_tile(math.gcd(lp_, ls), 256)
    tm_rw = _largest_tile(math.gcd(tp, ts), 256)

    def cond_of(tm):
        npt = tp // tm
        return lambda i: jnp.where(i < npt, 0, 1 + ((i - npt) * tm) // ls)

    x = jnp.concatenate([x_prompt.reshape(tp, d), x_sample.reshape(ts, d)], axis=0)
    cond8 = jnp.zeros((SUBLANES, d), F32).at[0].set(c_ctx).at[1:1 + bs].set(c)
    rope_tabs = _rope_tables(ls)
    blk = jnp.arange(LANES) // N_C
    e2 = (blk[:, None] == blk[None, :]).astype(BF16)
    zpad = jnp.zeros((LORA, c_c), F32)

    kv_new, ss = None, []
    for l in range(depth):
        lam_init = 0.8 - 0.6 * math.exp(-0.3 * l)
        mod = _ada(cond8, w_ada, b_ada[:, None, :], l).reshape(SUBLANES, 3, d)
        w_rot = jnp.concatenate([w_in[l][:, n_lead:], w_in[l][:, :n_lead]], axis=1).astype(BF16)
        p = _inproj(x, mod, norm_w[l][None, :], w_rot, cond_of(tm_in), tm_in, tn_in)

        lam_p = jnp.stack([lambda_q1[l], lambda_k1[l], lambda_q2[l], lambda_k2[l]], axis=0)
        sw = subln_w[l][None, :]
        cq, ck, cv, cg = (col[n] // w_a for n in ("q", "k", "v", "g_a"))
        tq_p = _largest_tile(lp_, 256)
        npq = lp_ // tq_p
        ba_p = _attention(
            lam_p, sw,
            p, lambda b, h, i: (b * npq + i, cq),
            p, lambda b, h, i: (b, ck),
            p, lambda b, h, i: (b, cv),
            p, lambda b, h, i: (b * npq + i, cg),
            None, bp, n_ha, n_ha, lp_, lp_, tq_p, lam_init, Q_SCALE)
        qk_rot = _rope_qk(p, rope_tabs, tp, ts, ls, col["q"], n_ha)
        cache = (cache_k[:, l].reshape(bs * past, w_a).astype(BF16),
                 cache_v[:, l].reshape(bs * past, w_a).astype(BF16), past)
        cv1, cg1 = col["v"] // HEAD_A, col["g_a"] // HEAD_A
        tq_s = _largest_tile(ls, 1024)
        nsq = ls // tq_s
        ba_s = _attention(
            lam_p, sw,
            qk_rot, lambda b, h, i: (b * nsq + i, h),
            qk_rot, lambda b, h, i: (b, n_ha + h),
            p, lambda b, h, i: (tp // ls + b, cv1 + h),
            p, lambda b, h, i: (tp // tq_s + b * nsq + i, cg1 + h),
            cache, bs, n_ha, 1, ls, ls, tq_s, lam_init, 1.0)

        bb = _conv(p, conv_w[l], conv_b[l][None, :], conv_ln_w[l][None, :], conv_ln_b[l][None, :],
                   col["glu"], col["g_b"], tt, tp // tt, lp_ // tt, ls // tt)

        wup_pad = jnp.stack([jnp.concatenate([rwkv_w_up[l, 0], zpad], 0), jnp.concatenate([zpad, rwkv_w_up[l, 1]], 0)])
        aup_pad = jnp.stack([jnp.concatenate([rwkv_a_up[l, 0], zpad], 0), jnp.concatenate([zpad, rwkv_a_up[l, 1]], 0)])
        kk, lw, kd, bd = _rwkv_prep(p, col["k_c"], col["xw"], col["xa"], rwkv_w0[l], wup_pad, rwkv_a0[l], aup_pad,
                                    rwkv_k_k[l][None, :], rwkv_k_a[l][None, :], e2, tm_rw)
        yf_p, yb_p, hfin_p = _rwkv_chunks(p, kk, lw, kd, bd, None, 0, bp, lp_, col["r_c"], col["v_c"])
        ss.append(hfin_p)
        yf_s, yb_s, _ = _rwkv_chunks(p, kk, lw, kd, bd, _state_to_wide(state_rwkv[:, l]), tp, bs, ls,
                                     col["r_c"], col["v_c"])
        post_args = (col["r_c"], col["v_c"], col["g_c"], rwkv_r_k[l].reshape(1, c_c),
                     rwkv_gn_w[l][None, :], rwkv_gn_b[l][None, :], e2, tm_rw)
        bc_p = _rwkv_post(yf_p, yb_p, p, kd, 0, *post_args)
        bc_s = _rwkv_post(yf_s, yb_s, p, kd, tp, *post_args)

        merged = _merge(ba_p, ba_s, bb, bc_p, bc_s, p, col["mg"], w_br_a[l].astype(BF16), w_br_b[l].astype(BF16), w_br_c[l].astype(BF16),
                        tm_mg, tn_mg)
        x = _out_proj(merged, w_out[l].astype(BF16), x, mod, final_norm_w[None, :], cond_of(tm_out), tm_out, tp,
                      final=(l == depth - 1))

        kv_new = _kv_out(p, col["k"], col["v"], w_a, bp, lp_, depth, l, kv_new)

    y_prompt = x[0].reshape(bp, lp_, d)
    y_sample = x[1].reshape(bs, ls, d)
    new_cache_k = kv_new[0].reshape(bp, depth, lp_, n_ha, 2, DH_A)
    new_cache_v = kv_new[1].reshape(bp, depth, lp_, n_ha, HEAD_A)
    return (y_prompt, y_sample, new_cache_k, new_cache_v, jnp.stack(ss, axis=1))
```

```python
import functools
import math

import jax
import jax.numpy as jnp
from jax import lax
from jax.experimental import pallas as pl
from jax.experimental.pallas import tpu as pltpu

F32 = jnp.float32
BF16 = jnp.bfloat16

HEAD_A = 128
DH_A = 64
CONV_W = 31
CONV_HALO = 16
N_C = 64
LORA = 64
GRID_W = 64
ROPE_BASE = 10000.0
EPS = 1e-6
LN_EPS = 1e-5
GN_EPS = 64e-5
Q_SCALE = DH_A ** -0.5 * math.log2(math.e)

LANES = 128
SUBLANES = 8
MIB = 1024 * 1024


def _cparams(semantics, vmem_mib=48):
    return pltpu.CompilerParams(dimension_semantics=semantics, vmem_limit_bytes=vmem_mib * MIB)


def _split_bf16(x):
    hi = x.astype(BF16)
    lo = (x - hi.astype(F32)).astype(BF16)
    return hi, lo


_NT = (((1,), (1,)), ((), ()))


def _mm(a, b):
    return jnp.dot(a, b, preferred_element_type=F32)


def _dot3(a, b):
    ah, al = _split_bf16(a)
    bh, bl = _split_bf16(b)
    return _mm(ah, bh) + _mm(ah, bl) + _mm(al, bh)


def _silu(x):
    return x * jax.nn.sigmoid(x)


def _run_lockstep(gens):
    last = [None] * len(gens)
    live = list(range(len(gens)))
    while live:
        nxt = []
        for i in live:
            try:
                last[i] = next(gens[i])
                nxt.append(i)
            except StopIteration:
                pass
        live = nxt
    return last


def _head_sums(x, e2):
    xb = x.astype(BF16)
    parts = [_mm(xb[:, c * LANES:(c + 1) * LANES], e2) for c in range(x.shape[1] // LANES)]
    return jnp.concatenate(parts, axis=1)


def _ada_kernel(c_ref, w_ref, b_ref, o_ref):
    o_ref[...] = _dot3(_silu(c_ref[...]), w_ref[...]) + b_ref[...]


def _ada(cond8, w, b, layer):
    _, d, n = w.shape
    tn = 512
    return pl.pallas_call(
        _ada_kernel,
        grid=(n // tn,),
        in_specs=[pl.BlockSpec((SUBLANES, d), lambda j: (0, 0)),
                  pl.BlockSpec((None, d, tn), lambda j: (layer, 0, j)),
                  pl.BlockSpec((None, 1, tn), lambda j: (layer, 0, j))],
        out_specs=pl.BlockSpec((SUBLANES, tn), lambda j: (0, j)),
        out_shape=jax.ShapeDtypeStruct((SUBLANES, n), F32),
        compiler_params=_cparams(("arbitrary",)),
        name="ada_mod",
    )(cond8, w, b)


def _rows_of(x_refs, n_prompt_tiles):
    if len(x_refs) == 1:
        return x_refs[0][...]
    return jnp.where(pl.program_id(0) < n_prompt_tiles, x_refs[0][...], x_refs[1][...])


def _row_specs(xs, tm, d, npt, nidx):
    if len(xs) == 1:
        return [pl.BlockSpec((tm, d), (lambda i, j: (i, 0)) if nidx == 2 else (lambda i: (i, 0)))]
    if nidx == 2:
        one = pl.Buffered(1)
        return [pl.BlockSpec((tm, d), lambda i, j: (jnp.minimum(i, npt - 1), 0), pipeline_mode=one),
                pl.BlockSpec((tm, d), lambda i, j: (jnp.maximum(i - npt, 0), 0), pipeline_mode=one)]
    return [pl.BlockSpec((tm, d), lambda i: (jnp.minimum(i, npt - 1), 0)),
            pl.BlockSpec((tm, d), lambda i: (jnp.maximum(i - npt, 0), 0))]


def _inproj_kernel(*refs, n_prompt_tiles):
    *x_refs, mod_ref, nw_ref, w_ref, o_ref, h_scr = refs

    @pl.when(pl.program_id(1) == 0)
    def _():
        x = _rows_of(x_refs, n_prompt_tiles)
        y = x * lax.rsqrt(jnp.mean(x * x, axis=-1, keepdims=True) + EPS) * nw_ref[...]
        h = y * (1.0 + mod_ref[0, 1:2, :]) + mod_ref[0, 0:1, :]
        h_scr[...] = h.astype(BF16)

    o_ref[...] = _mm(h_scr[...], w_ref[...]).astype(o_ref.dtype)


def _inproj(xs, mod, norm_w, w_bf16, cond_of_tile, tm, tn):
    t = sum(a.shape[0] for a in xs)
    d = xs[0].shape[1]
    n = w_bf16.shape[1]
    npt = xs[0].shape[0] // tm
    return pl.pallas_call(
        functools.partial(_inproj_kernel, n_prompt_tiles=npt),
        grid=(t // tm, n // tn),
        in_specs=_row_specs(xs, tm, d, npt, 2) + [
            pl.BlockSpec((1, 3, d), lambda i, j: (cond_of_tile(i), 0, 0)),
            pl.BlockSpec((1, d), lambda i, j: (0, 0)),
            pl.BlockSpec((d, tn), lambda i, j: (0, j))],
        out_specs=pl.BlockSpec((tm, tn), lambda i, j: (i, j)),
        out_shape=jax.ShapeDtypeStruct((t, n), BF16),
        scratch_shapes=[pltpu.VMEM((tm, d), BF16)],
        compiler_params=_cparams(("parallel", "arbitrary")),
        name="inproj",
    )(*xs, mod, norm_w, w_bf16)


def _rope_tab_kernel(inv_ref, c_ref, s1_ref, s2_ref, *, tr):
    t = lax.broadcasted_iota(jnp.int32, (tr, LANES), 0) + pl.program_id(0) * tr
    lane = lax.broadcasted_iota(jnp.int32, (tr, LANES), 1)
    row = (t // GRID_W).astype(F32)
    col = (t % GRID_W).astype(F32)
    pos = jnp.where((lane % DH_A) >= DH_A // 2, col, row)
    ang = pos * inv_ref[...]
    c = jnp.cos(ang)
    s = jnp.sin(ang)
    first = (lane % (DH_A // 2)) < DH_A // 4
    c_ref[...] = c
    s1_ref[...] = jnp.where(first, -s, 0.0)
    s2_ref[...] = jnp.where(first, 0.0, s)


def _rope_tables(ls):
    tr = min(ls, 512)
    quarter = DH_A // 4
    inv = ROPE_BASE ** (-jnp.arange(quarter, dtype=F32) / quarter)
    inv = jnp.tile(inv, LANES // quarter)[None, :]
    shp = jax.ShapeDtypeStruct((ls, LANES), F32)
    spec = pl.BlockSpec((tr, LANES), lambda i: (i, 0))
    return pl.pallas_call(
        functools.partial(_rope_tab_kernel, tr=tr),
        grid=(ls // tr,),
        in_specs=[pl.BlockSpec((1, LANES), lambda i: (0, 0))],
        out_specs=[spec, spec, spec],
        out_shape=[shp, shp, shp],
        compiler_params=_cparams(("arbitrary",)),
        name="rope_tables",
    )(inv)


def _rope_kernel(x_ref, c_ref, s1_ref, s2_ref, o_ref, *, n_q):
    quarter = DH_A // 4
    c, s1, s2 = c_ref[...], s1_ref[...], s2_ref[...]
    for j in range(x_ref.shape[1] // LANES):
        x = x_ref[:, j * LANES:(j + 1) * LANES].astype(F32)
        y = x * c + pltpu.roll(x, LANES - quarter, 1) * s1 + pltpu.roll(x, quarter, 1) * s2
        if j < n_q:
            y = y * Q_SCALE
        o_ref[:, j * LANES:(j + 1) * LANES] = y.astype(BF16)


def _rope_qk(p, tabs, row0, ts, ls, col0, n_heads):
    tr = min(ls, 512)
    width = 2 * n_heads * LANES
    tab_spec = pl.BlockSpec((tr, LANES), lambda i: (i % (ls // tr), 0))
    return pl.pallas_call(
        functools.partial(_rope_kernel, n_q=n_heads),
        grid=(ts // tr,),
        in_specs=[pl.BlockSpec((tr, width), lambda i: (row0 // tr + i, col0 // width)),
                  tab_spec, tab_spec, tab_spec],
        out_specs=pl.BlockSpec((tr, width), lambda i: (i, 0)),
        out_shape=jax.ShapeDtypeStruct((ts, width), BF16),
        compiler_params=_cparams(("parallel",)),
        name="rope_qk",
    )(p, *tabs)


ATTN_SUB = 128
ATTN_CHAINS = 4


def _attn_kernel(lam_ref, sw_ref, q_ref, k_ref, v_ref, g_ref, *rest, lam_init, q_scale, has_cache):
    if has_cache:
        kc_ref, vc_ref, o_ref = rest
    else:
        (o_ref,) = rest
    lp = lam_ref[...]
    lam = (jnp.exp(jnp.sum(lp[0:1, :] * lp[1:2, :], axis=1, keepdims=True))
           - jnp.exp(jnp.sum(lp[2:3, :] * lp[3:4, :], axis=1, keepdims=True)) + lam_init)
    tq = q_ref.shape[0]
    rs = min(ATTN_SUB, tq)
    lane = lax.broadcasted_iota(jnp.int32, (rs, HEAD_A), 1)
    zero = jnp.zeros((rs, HEAD_A), BF16)

    def head_kv(hh):
        hs = slice(hh * HEAD_A, (hh + 1) * HEAD_A)
        k = k_ref[:, hs].astype(BF16)
        v = v_ref[:, hs].astype(BF16)
        if has_cache:
            k = jnp.concatenate([kc_ref[:, hs], k], axis=0)
            v = jnp.concatenate([vc_ref[:, hs], v], axis=0)
        return k, jnp.concatenate([v, jnp.ones_like(v)], axis=1)

    def logits(hh, i, k):
        q = q_ref[i * rs:(i + 1) * rs, hh * HEAD_A:(hh + 1) * HEAD_A]
        if q_scale != 1.0:
            q = q.astype(F32) * q_scale
        q = q.astype(BF16)
        lhs = jnp.concatenate([jnp.where(lane < DH_A, q, zero), jnp.where(lane < DH_A, zero, q)], axis=0)
        return lax.dot_general(lhs, k, _NT, preferred_element_type=F32)

    kv = {}

    def kv_of(hh):
        if hh not in kv:
            kv[hh] = head_kv(hh)
        return kv[hh]

    def chain(items):
        s = logits(*items[0], kv_of(items[0][0])[0])
        yield
        for idx, (hh, i) in enumerate(items):
            if idx + 1 < len(items):
                s_next = logits(*items[idx + 1], kv_of(items[idx + 1][0])[0])
                yield
            e = jnp.exp2(s - jnp.max(s, axis=-1, keepdims=True)).astype(BF16)
            oa = _mm(e, kv_of(hh)[1])
            yield
            on = oa[:, :HEAD_A] * (1.0 / oa[:, HEAD_A:HEAD_A + 1])
            o = on[:rs] - lam * on[rs:]
            o = o * lax.rsqrt(jnp.mean(o * o, axis=-1, keepdims=True) + EPS) * sw_ref[...] * (1.0 - lam_init)
            rows, hs = slice(i * rs, (i + 1) * rs), slice(hh * HEAD_A, (hh + 1) * HEAD_A)
            o_ref[rows, hs] = (o * _silu(g_ref[rows, hs].astype(F32))).astype(BF16)
            if idx + 1 < len(items):
                s = s_next

    n_heads_blk = q_ref.shape[1] // HEAD_A
    n_chains = min(n_heads_blk, ATTN_CHAINS)
    items = [(hh, i) for hh in range(n_heads_blk) for i in range(tq // rs)]
    per = len(items) // n_chains
    _run_lockstep([chain(items[c * per:(c + 1) * per]) for c in range(n_chains)])


def _attention(lam_p, subln_w, q_arr, q_map, k_arr, k_map, v_arr, v_map, g_arr, g_map, cache,
               n_b, n_heads, hps, lq, lk_new, tq, lam_init, q_scale):
    nq = lq // tq
    wide = hps * HEAD_A
    in_specs = [pl.BlockSpec(lam_p.shape, lambda b, h, i: (0, 0)),
                pl.BlockSpec((1, HEAD_A), lambda b, h, i: (0, 0)),
                pl.BlockSpec((tq, wide), q_map),
                pl.BlockSpec((lk_new, wide), k_map),
                pl.BlockSpec((lk_new, wide), v_map),
                pl.BlockSpec((tq, wide), g_map)]
    args = [lam_p, subln_w, q_arr, k_arr, v_arr, g_arr]
    if cache is not None:
        kc, vc, past = cache
        cspec = pl.BlockSpec((past, wide), lambda b, h, i: (b, h))
        in_specs += [cspec, cspec]
        args += [kc, vc]
    return pl.pallas_call(
        functools.partial(_attn_kernel, lam_init=lam_init, q_scale=q_scale, has_cache=cache is not None),
        grid=(n_b, n_heads // hps, nq),
        in_specs=in_specs,
        out_specs=pl.BlockSpec((tq, wide), lambda b, h, i: (b * nq + i, h)),
        out_shape=jax.ShapeDtypeStruct((n_b * lq, n_heads * HEAD_A), BF16),
        compiler_params=_cparams(("parallel", "parallel", "arbitrary")),
        name="diff_attn",
    )(*args)


def _conv_kernel(u_ref, up_ref, un_ref, g_ref, cw_ref, cb_ref, lw_ref, lb_ref, o_ref, zp_scr, acc_scr, zs_scr,
                 *, tt, n_prompt_blocks, blocks_per_prompt_seq, blocks_per_sample_seq):
    cb_ = u_ref.shape[1] // 2
    blk = pl.program_id(0)
    in_prompt = blk < n_prompt_blocks
    pos = jnp.where(in_prompt, blk % blocks_per_prompt_seq, (blk - n_prompt_blocks) % blocks_per_sample_seq)
    per_seq = jnp.where(in_prompt, blocks_per_prompt_seq, blocks_per_sample_seq)
    keep_prev = (pos > 0).astype(F32)
    keep_next = (pos < per_seq - 1).astype(F32)

    def glu(ref):
        u = ref[...].astype(F32)
        return u[:, :cb_] * jax.nn.sigmoid(u[:, cb_:])

    h = CONV_HALO
    zp_scr[pl.ds(0, h), :] = glu(up_ref) * keep_prev
    zp_scr[pl.ds(h, tt), :] = glu(u_ref)
    zp_scr[pl.ds(h + tt, h), :] = glu(un_ref) * keep_next

    rc = 64
    off = h - CONV_W // 2
    n_sh = zs_scr.shape[1]

    def lane_chunk(c, carry):
        ls = pl.ds(pl.multiple_of(c * LANES, LANES), LANES)
        w = cw_ref[:, ls]
        for s in range(SUBLANES):
            zs_scr[s] = zp_scr[pl.ds(s, n_sh), ls]
        for r0 in range(0, tt, rc):
            acc = jnp.zeros((rc, LANES), F32)
            for tau in range(CONV_W):
                o = tau + off
                acc = acc + zs_scr[o % SUBLANES, pl.ds(r0 + o - o % SUBLANES, rc), :] * w[tau:tau + 1, :]
            acc_scr[pl.ds(r0, rc), ls] = acc
        return carry

    lax.fori_loop(0, cb_ // LANES, lane_chunk, 0)

    z = acc_scr[...] + cb_ref[...]
    mu = jnp.mean(z, axis=-1, keepdims=True)
    d = z - mu
    var = jnp.mean(d * d, axis=-1, keepdims=True)
    zn = d * lax.rsqrt(var + LN_EPS) * lw_ref[...] + lb_ref[...]
    o_ref[...] = (_silu(zn) * _silu(g_ref[...].astype(F32))).astype(BF16)


def _conv(p, conv_w, conv_b, ln_w, ln_b, glu_col0, gb_col0, tt, n_prompt_blocks,
          blocks_per_prompt_seq, blocks_per_sample_seq):
    t = p.shape[0]
    cb_ = conv_w.shape[1]
    nblk = t // tt
    hb = tt // CONV_HALO
    n_hblk = t // CONV_HALO
    vec = pl.BlockSpec((1, cb_), lambda i: (0, 0))
    return pl.pallas_call(
        functools.partial(_conv_kernel, tt=tt, n_prompt_blocks=n_prompt_blocks,
                          blocks_per_prompt_seq=blocks_per_prompt_seq,
                          blocks_per_sample_seq=blocks_per_sample_seq),
        grid=(nblk,),
        in_specs=[pl.BlockSpec((tt, 2 * cb_), lambda i: (i, glu_col0 // (2 * cb_))),
                  pl.BlockSpec((CONV_HALO, 2 * cb_), lambda i: (jnp.maximum(i * hb - 1, 0), glu_col0 // (2 * cb_))),
                  pl.BlockSpec((CONV_HALO, 2 * cb_),
                               lambda i: (jnp.minimum((i + 1) * hb, n_hblk - 1), glu_col0 // (2 * cb_))),
                  pl.BlockSpec((tt, cb_), lambda i: (i, gb_col0 // cb_)),
                  pl.BlockSpec(conv_w.shape, lambda i: (0, 0)),
                  vec, vec, vec],
        out_specs=pl.BlockSpec((tt, cb_), lambda i: (i, 0)),
        out_shape=jax.ShapeDtypeStruct((t, cb_), BF16),
        scratch_shapes=[pltpu.VMEM((tt + 2 * CONV_HALO, cb_), F32), pltpu.VMEM((tt, cb_), F32),
                        pltpu.VMEM((SUBLANES, tt + 2 * CONV_HALO - SUBLANES, LANES), F32)],
        compiler_params=_cparams(("parallel",)),
        name="conformer_conv",
    )(p, p, p, p, conv_w, conv_b, ln_w, ln_b)


def _rwkv_prep_kernel(k_ref, xw_ref, xa_ref, w0_ref, wup_ref, a0_ref, aup_ref, kk_ref, ka_ref, e2_ref,
                      okk_ref, olw_ref, okd_ref, obd_ref):
    k = k_ref[...].astype(F32)
    e2 = e2_ref[...]
    kkr = k * kk_ref[...]
    kk = kkr / jnp.maximum(jnp.sqrt(_head_sums(kkr * kkr, e2)), 1e-12)
    okk_ref[...] = kk
    txw = jnp.tanh(xw_ref[...].astype(F32))
    xa = xa_ref[...].astype(F32)
    for d in range(2):
        olw_ref[d] = -math.exp(-0.5) * jax.nn.sigmoid(w0_ref[d:d + 1, :] + _dot3(txw, wup_ref[d]))
        a = jax.nn.sigmoid(a0_ref[d:d + 1, :] + _dot3(xa, aup_ref[d]))
        okd_ref[d] = k * (1.0 + (a - 1.0) * ka_ref[...])
        obd_ref[d] = kk * a


def _rwkv_prep(p, kc_col0, xw_col0, xa_col0, w0, wup_pad, a0, aup_pad, k_k, k_a, e2, tm):
    t = p.shape[0]
    cc = w0.shape[1]
    vec = pl.BlockSpec((1, cc), lambda i: (0, 0))
    vec2 = pl.BlockSpec((2, cc), lambda i: (0, 0))
    mat = pl.BlockSpec((2, LANES, cc), lambda i: (0, 0, 0))
    one = jax.ShapeDtypeStruct((t, cc), F32)
    two = jax.ShapeDtypeStruct((2, t, cc), F32)
    two_spec = pl.BlockSpec((2, tm, cc), lambda i: (0, i, 0))
    return pl.pallas_call(
        _rwkv_prep_kernel,
        grid=(t // tm,),
        in_specs=[pl.BlockSpec((tm, cc), lambda i: (i, kc_col0 // cc)),
                  pl.BlockSpec((tm, LANES), lambda i: (i, xw_col0 // LANES)),
                  pl.BlockSpec((tm, LANES), lambda i: (i, xa_col0 // LANES)),
                  vec2, mat, vec2, mat, vec, vec,
                  pl.BlockSpec((LANES, LANES), lambda i: (0, 0))],
        out_specs=[pl.BlockSpec((tm, cc), lambda i: (i, 0)), two_spec, two_spec, two_spec],
        out_shape=[one, two, two, two],
        compiler_params=_cparams(("parallel",)),
        name="rwkv_prep",
    )(p, p, p, w0, wup_pad, a0, aup_pad, k_k, k_a, e2)


CHUNK = 64
HEADS_PER_UNIT = 2
UNIT_W = HEADS_PER_UNIT * N_C
UNITS_PER_BODY = 8


def _mmb(a, b):
    return _mm(a.astype(BF16), b.astype(BF16))


def _chunk_unit(r, v, kk, lw, c, kd, bd, s0, reverse, head_masks, eye2, strict, incl, same_head):
    w2 = UNIT_W
    ctot = c[0:1, :] if reverse else c[CHUNK - 1:CHUNK, :]
    g_inv = jnp.exp(-c)
    g_rem = jnp.exp(ctot - c)
    rt = r * jnp.exp(c)
    bt = -(kk * jnp.exp(c - lw))

    def stack(x):
        xb = x.astype(BF16)
        z = jnp.zeros_like(xb)
        return jnp.concatenate([jnp.where(m, xb, z) for m in head_masks], axis=0)

    v_s = stack(v)
    s0b = s0.astype(BF16)
    lhs = jnp.concatenate([bt, rt], axis=0).astype(BF16)
    rhs = jnp.concatenate([stack(bd * g_inv), stack(kd * g_inv)], axis=0)
    nn = lax.dot_general(lhs, rhs, _NT, preferred_element_type=F32)
    yield None
    a_ab = jnp.where(strict, nn[:CHUNK, :w2], 0.0)
    a_bk = jnp.where(strict, nn[:CHUNK, w2:], 0.0)
    a_r = jnp.concatenate([jnp.where(incl, nn[CHUNK:, :w2], 0.0), jnp.where(incl, nn[CHUNK:, w2:], 0.0)], axis=1)
    p1 = _mm(a_bk.astype(BF16), v_s)
    t = eye2 + a_ab
    n = _mm(a_ab.astype(BF16), stack(a_ab))
    yield None
    for _ in range(int(math.log2(CHUNK)) - 2):
        prod = _mm(jnp.concatenate([n, t], axis=0).astype(BF16), stack(n))
        yield None
        t = t + prod[CHUNK:]
        n = prod[:CHUNK]
    t = t + _mm(t.astype(BF16), stack(n))
    yield None
    wx = _mm(t.astype(BF16), jnp.concatenate([stack(bt), stack(p1)], axis=1))
    yield None
    w, x = wx[:, :w2], wx[:, w2:]
    rhs_big = jnp.concatenate(
        [jnp.concatenate([stack(w), stack(x)], axis=1),
         jnp.concatenate([jnp.zeros((w2, w2), BF16), v_s], axis=1)], axis=0)
    qy = _mm(a_r.astype(BF16), rhs_big)
    w_pad = jnp.concatenate([w, jnp.zeros_like(w)], axis=0).astype(BF16)
    uv_t = jnp.concatenate([x, v], axis=0).T + lax.dot_general(s0b, w_pad, _NT, preferred_element_type=F32)
    yield None
    qhat = rt + qy[:, :w2]
    y = lax.dot_general(qhat.astype(BF16), s0b, _NT, preferred_element_type=F32) + qy[:, w2:]
    upd = _mmb(uv_t, jnp.concatenate([bd * g_rem, kd * g_rem], axis=0))
    yield s0 * jnp.exp(ctot) + jnp.where(same_head, upd, 0.0), y


def _rwkv_chunk_kernel(rf_ref, vf_ref, kkf_ref, lwf_ref, kdf_ref, bdf_ref,
                       rb_ref, vb_ref, kkb_ref, lwb_ref, kdb_ref, bdb_ref, *rest, has_init):
    if has_init:
        h0_ref, yf_ref, yb_ref, hfin_ref, h_scr, c_scr = rest
    else:
        yf_ref, yb_ref, hfin_ref, h_scr, c_scr = rest

    @pl.when(pl.program_id(1) == 0)
    def _():
        h_scr[...] = h0_ref[0] if has_init else jnp.zeros(h_scr.shape, F32)

    ti = lax.broadcasted_iota(jnp.int32, (CHUNK, CHUNK), 0)
    si = lax.broadcasted_iota(jnp.int32, (CHUNK, CHUNK), 1)

    def cumsum(tri, x):
        hi = x.astype(BF16)
        r1 = x - hi.astype(F32)
        mid = r1.astype(BF16)
        lo = (r1 - mid.astype(F32)).astype(BF16)
        return _mm(tri, hi) + _mm(tri, mid) + _mm(tri, lo)

    c_scr[0] = cumsum((si <= ti).astype(BF16), lwf_ref[...])
    c_scr[1] = cumsum((si >= ti).astype(BF16), lwb_ref[...])

    w2 = UNIT_W
    tpos = lax.broadcasted_iota(jnp.int32, (CHUNK, w2), 0)
    lane = lax.broadcasted_iota(jnp.int32, (CHUNK, w2), 1)
    spos = lane % CHUNK
    eye2 = (spos == tpos).astype(F32)
    head_masks = [lane // N_C == hh for hh in range(HEADS_PER_UNIT)]
    same_head = (lax.broadcasted_iota(jnp.int32, (w2, w2), 0) // N_C
                 == lax.broadcasted_iota(jnp.int32, (w2, w2), 1) // N_C)
    dirs = ((rf_ref, vf_ref, kkf_ref, lwf_ref, kdf_ref, bdf_ref, yf_ref, spos < tpos, spos <= tpos),
            (rb_ref, vb_ref, kkb_ref, lwb_ref, kdb_ref, bdb_ref, yb_ref, spos > tpos, spos >= tpos))

    def pair_group(pg, carry):
        units = []
        for pi in range(UNITS_PER_BODY):
            p = pg * UNITS_PER_BODY + pi
            ls = pl.ds(pl.multiple_of(p * w2, w2), w2)
            for d, (r_ref, v_ref, kk_ref, lw_ref, kd_ref, bd_ref, y_ref, strict, incl) in enumerate(dirs):
                args = (r_ref[:, ls].astype(F32), v_ref[:, ls].astype(F32), kk_ref[:, ls], lw_ref[:, ls], c_scr[d, :, ls],
                        kd_ref[:, ls], bd_ref[:, ls], h_scr[d, p])
                units.append((d, p, ls, y_ref, args, strict, incl))
        results = _run_lockstep([_chunk_unit(*args, d == 1, head_masks, eye2, strict, incl, same_head)
                                 for d, p, ls, y_ref, args, strict, incl in units])
        for (d, p, ls, y_ref, _, _, _), (h_new, y) in zip(units, results):
            h_scr[d, p] = h_new
            y_ref[:, ls] = y
        return carry

    lax.fori_loop(0, rf_ref.shape[1] // (w2 * UNITS_PER_BODY), pair_group, 0)

    @pl.when(pl.program_id(1) == pl.num_programs(1) - 1)
    def _():
        for d in range(2):
            for u in range(h_scr.shape[1]):
                for a in range(HEADS_PER_UNIT):
                    blk = slice(a * N_C, (a + 1) * N_C)
                    hfin_ref[0, d, u * HEADS_PER_UNIT + a] = h_scr[d, u, blk, blk]


def _rwkv_chunks(p, kk, lw, kd, bd, h0, row0, nseq, l, rc_col0, vc_col0):
    cc = kk.shape[1]
    nc = l // CHUNK
    rb0 = row0 // CHUNK
    nunit = cc // UNIT_W

    def fwd(s, g):
        return rb0 + s * nc + g

    def bwd(s, g):
        return rb0 + s * nc + nc - 1 - g

    def specs(rowf, d):
        return [pl.BlockSpec((CHUNK, cc), lambda s, g: (rowf(s, g), rc_col0 // cc)),
                pl.BlockSpec((CHUNK, cc), lambda s, g: (rowf(s, g), vc_col0 // cc)),
                pl.BlockSpec((CHUNK, cc), lambda s, g: (rowf(s, g), 0)),
                pl.BlockSpec((None, CHUNK, cc), lambda s, g: (d, rowf(s, g), 0)),
                pl.BlockSpec((None, CHUNK, cc), lambda s, g: (d, rowf(s, g), 0)),
                pl.BlockSpec((None, CHUNK, cc), lambda s, g: (d, rowf(s, g), 0))]

    hshape = (nseq, 2, nunit, UNIT_W, UNIT_W)
    fshape = (nseq, 2, cc // N_C, N_C, N_C)
    hspec = pl.BlockSpec((1,) + hshape[1:], lambda s, g: (s, 0, 0, 0, 0))
    yshape = jax.ShapeDtypeStruct((nseq * l, cc), F32)
    init = [] if h0 is None else [h0]
    return pl.pallas_call(
        functools.partial(_rwkv_chunk_kernel, has_init=h0 is not None),
        grid=(nseq, nc),
        in_specs=specs(fwd, 0) + specs(bwd, 1) + [hspec] * len(init),
        out_specs=[pl.BlockSpec((CHUNK, cc), lambda s, g: (s * nc + g, 0)),
                   pl.BlockSpec((CHUNK, cc), lambda s, g: (s * nc + nc - 1 - g, 0)),
                   pl.BlockSpec((1,) + fshape[1:], lambda s, g: (s, 0, 0, 0, 0))],
        out_shape=[yshape, yshape, jax.ShapeDtypeStruct(fshape, F32)],
        scratch_shapes=[pltpu.VMEM(hshape[1:], F32), pltpu.VMEM((2, CHUNK, cc), F32)],
        compiler_params=_cparams(("parallel", "arbitrary")),
        name="rwkv_chunks",
    )(p, p, kk, lw, kd, bd, p, p, kk, lw, kd, bd, *init)


def _state_to_wide(s):
    nseq, _, nh, n, _ = s.shape
    g = HEADS_PER_UNIT
    ht = s.reshape(nseq, 2, nh // g, g, n, n)
    z = jnp.zeros_like(ht[:, :, :, 0])
    rows = [jnp.concatenate([ht[:, :, :, a] if a == b else z for b in range(g)], axis=-1) for a in range(g)]
    return jnp.concatenate(rows, axis=-2)


def _rwkv_post_kernel(yf_ref, yb_ref, r_ref, v_ref, kd_ref, g_ref, rk_ref, gw_ref, gb_ref, e2_ref, o_ref):
    e2 = e2_ref[...]
    y = yf_ref[...] + yb_ref[...]
    inv_n = 1.0 / N_C
    d = y - _head_sums(y, e2) * inv_n
    var = _head_sums(d * d, e2) * inv_n
    yn = d * lax.rsqrt(var + GN_EPS) * gw_ref[...] + gb_ref[...]
    bonus = (_head_sums(r_ref[...].astype(F32) * (kd_ref[0] + kd_ref[1]) * rk_ref[...], e2)
             * v_ref[...].astype(F32))
    o_ref[...] = ((yn + bonus) * _silu(g_ref[...].astype(F32))).astype(BF16)


def _rwkv_post(yf, yb, p, kd, row0, rc_col0, vc_col0, gc_col0, r_k, gn_w, gn_b, e2, tm):
    t, cc = yf.shape
    rb0 = row0 // tm
    vec = pl.BlockSpec((1, cc), lambda i: (0, 0))
    own = pl.BlockSpec((tm, cc), lambda i: (i, 0))
    return pl.pallas_call(
        _rwkv_post_kernel,
        grid=(t // tm,),
        in_specs=[own, own,
                  pl.BlockSpec((tm, cc), lambda i: (rb0 + i, rc_col0 // cc)),
                  pl.BlockSpec((tm, cc), lambda i: (rb0 + i, vc_col0 // cc)),
                  pl.BlockSpec((2, tm, cc), lambda i: (0, rb0 + i, 0)),
                  pl.BlockSpec((tm, cc), lambda i: (rb0 + i, gc_col0 // cc)),
                  vec, vec, vec,
                  pl.BlockSpec((LANES, LANES), lambda i: (0, 0))],
        out_specs=own,
        out_shape=jax.ShapeDtypeStruct((t, cc), BF16),
        compiler_params=_cparams(("parallel",)),
        name="rwkv_post",
    )(yf, yb, p, p, kd, p, r_k, gn_w, gn_b, e2)


def _merge_kernel(bap_ref, bas_ref, bb_ref, bcp_ref, bcs_ref, m0_ref, m1_ref, m2_ref, wa_ref, wb_ref, wc_ref, o_ref,
                  *, n_prompt_tiles):
    in_prompt = pl.program_id(0) < n_prompt_tiles
    ba = jnp.where(in_prompt, bap_ref[...], bas_ref[...])
    bc = jnp.where(in_prompt, bcp_ref[...], bcs_ref[...])
    o_ref[...] = (jax.nn.sigmoid(m0_ref[...].astype(F32)) * _mm(ba, wa_ref[...])
                  + jax.nn.sigmoid(m1_ref[...].astype(F32)) * _mm(bb_ref[...], wb_ref[...])
                  + jax.nn.sigmoid(m2_ref[...].astype(F32)) * _mm(bc, wc_ref[...])).astype(BF16)


def _merge(ba_p, ba_s, bb, bc_p, bc_s, p, mg_col0, wa, wb, wc, tm, tn):
    t, wbr = bb.shape
    d = wa.shape[1]
    nj = d // tn
    j0 = mg_col0 // tn
    npt = ba_p.shape[0] // tm
    prm = pl.BlockSpec((tm, wbr), lambda i, j: (jnp.minimum(i, npt - 1), 0))
    smp = pl.BlockSpec((tm, wbr), lambda i, j: (jnp.maximum(i - npt, 0), 0))
    wspec = pl.BlockSpec((wbr, tn), lambda i, j: (0, j))
    return pl.pallas_call(
        functools.partial(_merge_kernel, n_prompt_tiles=npt),
        grid=(t // tm, nj),
        in_specs=[prm, smp, pl.BlockSpec((tm, wbr), lambda i, j: (i, 0)), prm, smp,
                  pl.BlockSpec((tm, tn), lambda i, j: (i, j0 + j)),
                  pl.BlockSpec((tm, tn), lambda i, j: (i, j0 + nj + j)),
                  pl.BlockSpec((tm, tn), lambda i, j: (i, j0 + 2 * nj + j)),
                  wspec, wspec, wspec],
        out_specs=pl.BlockSpec((tm, tn), lambda i, j: (i, j)),
        out_shape=jax.ShapeDtypeStruct((t, d), BF16),
        compiler_params=_cparams(("parallel", "arbitrary")),
        name="branch_merge",
    )(ba_p, ba_s, bb, bc_p, bc_s, p, p, p, wa, wb, wc)


def _resid_kernel(m_ref, wo_ref, *refs, n_x, n_prompt_tiles):
    x_refs, (mod_ref, fw_ref), o_refs = refs[:n_x], refs[n_x:n_x + 2], refs[n_x + 2:]
    x = _rows_of(x_refs, n_prompt_tiles) + mod_ref[0, 2:3, :] * _mm(m_ref[...], wo_ref[...])
    if len(o_refs) == 1:
        o_refs[0][...] = x
        return
    x = x * lax.rsqrt(jnp.mean(x * x, axis=-1, keepdims=True) + EPS) * fw_ref[...]
    in_prompt = pl.program_id(0) < n_prompt_tiles

    @pl.when(in_prompt)
    def _():
        o_refs[0][...] = x

    @pl.when(jnp.logical_not(in_prompt))
    def _():
        o_refs[1][...] = x


def _out_proj(merged, wo, xs, mod, final_w, cond_of_tile, tm, tp, final):
    t, d = merged.shape
    npt = tp // tm
    row = pl.BlockSpec((tm, d), lambda i: (i, 0))
    if final:
        out_specs = [pl.BlockSpec((tm, d), lambda i: (jnp.minimum(i, npt - 1), 0)),
                     pl.BlockSpec((tm, d), lambda i: (jnp.maximum(i - npt, 0), 0))]
        out_shape = [jax.ShapeDtypeStruct((tp, d), F32), jax.ShapeDtypeStruct((t - tp, d), F32)]
    else:
        out_specs, out_shape = row, jax.ShapeDtypeStruct((t, d), F32)
    return pl.pallas_call(
        functools.partial(_resid_kernel, n_x=len(xs), n_prompt_tiles=npt),
        grid=(t // tm,),
        in_specs=[row, pl.BlockSpec((d, d), lambda i: (0, 0))] + _row_specs(xs, tm, d, npt, 1) + [
            pl.BlockSpec((1, 3, d), lambda i: (cond_of_tile(i), 0, 0)),
            pl.BlockSpec((1, d), lambda i: (0, 0))],
        out_specs=out_specs,
        out_shape=out_shape,
        compiler_params=_cparams(("arbitrary",)),
        name="out_proj",
    )(merged, wo, *xs, mod, final_w)


def _kv_out_kernel(k_ref, v_ref, *rest):
    ok_ref, ov_ref = rest[-2:]
    ok_ref[0] = k_ref[...].astype(F32)
    ov_ref[0] = v_ref[...].astype(F32)


def _kv_out(p, k_col0, v_col0, width, n_b, l, depth, layer, prev):
    shp = jax.ShapeDtypeStruct((n_b, depth, l, width), F32)
    ospec = pl.BlockSpec((1, None, l, width), lambda b: (b, layer, 0, 0))
    in_specs = [pl.BlockSpec((l, width), lambda b: (b, k_col0 // width)),
                pl.BlockSpec((l, width), lambda b: (b, v_col0 // width))]
    args, aliases = [p, p], {}
    if prev is not None:
        in_specs += [pl.BlockSpec(memory_space=pl.ANY)] * 2
        args += list(prev)
        aliases = {2: 0, 3: 1}
    return pl.pallas_call(
        _kv_out_kernel,
        grid=(n_b,),
        in_specs=in_specs,
        out_specs=[ospec, ospec],
        out_shape=[shp, shp],
        input_output_aliases=aliases,
        compiler_params=_cparams(("arbitrary",)),
        name="kv_out",
    )(*args)


def _largest_tile(n, cap):
    t = cap
    while n % t:
        t //= 2
    return t


def kernel(x_prompt, x_sample, cache_k, cache_v, state_rwkv, c, c_ctx, w_ada, b_ada, norm_w, w_in,
           lambda_q1, lambda_k1, lambda_q2, lambda_k2, subln_w, conv_w, conv_b, conv_ln_w, conv_ln_b,
           rwkv_w0, rwkv_w_up, rwkv_a0, rwkv_a_up, rwkv_k_k, rwkv_k_a, rwkv_r_k, rwkv_gn_w, rwkv_gn_b,
           w_br_a, w_br_b, w_br_c, w_out, final_norm_w):
    bp, lp_, d = x_prompt.shape
    bs, ls, _ = x_sample.shape
    depth = w_ada.shape[0]
    past = cache_k.shape[2]
    tp, ts = bp * lp_, bs * ls
    n_ha = d // 256
    w_a = n_ha * HEAD_A
    c_b = d // 2
    n_hc = d // (2 * N_C)
    c_c = n_hc * N_C
    n_mg = 3 * d
    assert n_ha * 2 * DH_A == w_a and w_a == c_b == c_c
    assert bs + 1 <= SUBLANES

    col = {}
    off = 0
    for name, width in (("mg", n_mg), ("q", w_a), ("k", w_a), ("v", w_a), ("g_a", w_a), ("glu", 2 * c_b),
                        ("g_b", c_b), ("r_c", c_c), ("k_c", c_c), ("v_c", c_c), ("g_c", c_c),
                        ("xw", 2 * LORA), ("xa", 2 * LORA)):
        col[name] = off
        off += width
    in_cols = off
    n_lead = in_cols - n_mg

    tm_in = _largest_tile(math.gcd(tp, ls), 1024)
    tm_out = _largest_tile(math.gcd(tp, ls), 512)
    tn_in = 768 if in_cols % 768 == 0 else 256
    tm_mg = _largest_tile(tp + ts, 1024)
    tn_mg = math.gcd(col["mg"], 512)
    tt = _largest_tile(math.gcd(lp_, ls), 256)
    tm_rw = _largest_tile(math.gcd(tp, ts), 256)

    def cond_of(tm):
        npt = tp // tm
        return lambda i: jnp.where(i < npt, 0, 1 + ((i - npt) * tm) // ls)

    xs = (x_prompt.reshape(tp, d), x_sample.reshape(ts, d))
    cond8 = jnp.zeros((SUBLANES, d), F32).at[0].set(c_ctx).at[1:1 + bs].set(c)
    rope_tabs = _rope_tables(ls)
    blk = jnp.arange(LANES) // N_C
    e2 = (blk[:, None] == blk[None, :]).astype(BF16)
    zpad = jnp.zeros((LORA, c_c), F32)

    kv_new, ss = None, []
    for l in range(depth):
        lam_init = 0.8 - 0.6 * math.exp(-0.3 * l)
        mod = _ada(cond8, w_ada, b_ada[:, None, :], l).reshape(SUBLANES, 3, d)
        w_rot = jnp.concatenate([w_in[l][:, n_lead:], w_in[l][:, :n_lead]], axis=1).astype(BF16)
        p = _inproj(xs, mod, norm_w[l][None, :], w_rot, cond_of(tm_in), tm_in, tn_in)

        lam_p = jnp.stack([lambda_q1[l], lambda_k1[l], lambda_q2[l], lambda_k2[l]], axis=0)
        sw = subln_w[l][None, :]
        cq, ck, cv, cg = (col[n] // w_a for n in ("q", "k", "v", "g_a"))
        tq_p = _largest_tile(lp_, 256)
        npq = lp_ // tq_p
        ba_p = _attention(
            lam_p, sw,
            p, lambda b, h, i: (b * npq + i, cq),
            p, lambda b, h, i: (b, ck),
            p, lambda b, h, i: (b, cv),
            p, lambda b, h, i: (b * npq + i, cg),
            None, bp, n_ha, n_ha, lp_, lp_, tq_p, lam_init, Q_SCALE)
        qk_rot = _rope_qk(p, rope_tabs, tp, ts, ls, col["q"], n_ha)
        cache = (cache_k[:, l].reshape(bs * past, w_a).astype(BF16),
                 cache_v[:, l].reshape(bs * past, w_a).astype(BF16), past)
        cv1, cg1 = col["v"] // HEAD_A, col["g_a"] // HEAD_A
        tq_s = _largest_tile(ls, 1024)
        nsq = ls // tq_s
        ba_s = _attention(
            lam_p, sw,
            qk_rot, lambda b, h, i: (b * nsq + i, h),
            qk_rot, lambda b, h, i: (b, n_ha + h),
            p, lambda b, h, i: (tp // ls + b, cv1 + h),
            p, lambda b, h, i: (tp // tq_s + b * nsq + i, cg1 + h),
            cache, bs, n_ha, 1, ls, ls, tq_s, lam_init, 1.0)

        bb = _conv(p, conv_w[l], conv_b[l][None, :], conv_ln_w[l][None, :], conv_ln_b[l][None, :],
                   col["glu"], col["g_b"], tt, tp // tt, lp_ // tt, ls // tt)

        wup_pad = jnp.stack([jnp.concatenate([rwkv_w_up[l, 0], zpad], 0), jnp.concatenate([zpad, rwkv_w_up[l, 1]], 0)])
        aup_pad = jnp.stack([jnp.concatenate([rwkv_a_up[l, 0], zpad], 0), jnp.concatenate([zpad, rwkv_a_up[l, 1]], 0)])
        kk, lw, kd, bd = _rwkv_prep(p, col["k_c"], col["xw"], col["xa"], rwkv_w0[l], wup_pad, rwkv_a0[l], aup_pad,
                                    rwkv_k_k[l][None, :], rwkv_k_a[l][None, :], e2, tm_rw)
        yf_p, yb_p, hfin_p = _rwkv_chunks(p, kk, lw, kd, bd, None, 0, bp, lp_, col["r_c"], col["v_c"])
        ss.append(hfin_p)
        yf_s, yb_s, _ = _rwkv_chunks(p, kk, lw, kd, bd, _state_to_wide(state_rwkv[:, l]), tp, bs, ls,
                                     col["r_c"], col["v_c"])
        post_args = (col["r_c"], col["v_c"], col["g_c"], rwkv_r_k[l].reshape(1, c_c),
                     rwkv_gn_w[l][None, :], rwkv_gn_b[l][None, :], e2, tm_rw)
        bc_p = _rwkv_post(yf_p, yb_p, p, kd, 0, *post_args)
        bc_s = _rwkv_post(yf_s, yb_s, p, kd, tp, *post_args)

        merged = _merge(ba_p, ba_s, bb, bc_p, bc_s, p, col["mg"], w_br_a[l].astype(BF16), w_br_b[l].astype(BF16), w_br_c[l].astype(BF16),
                        tm_mg, tn_mg)
        out = _out_proj(merged, w_out[l].astype(BF16), xs, mod, final_norm_w[None, :], cond_of(tm_out), tm_out, tp,
                        final=(l == depth - 1))
        xs = tuple(out) if l == depth - 1 else (out,)

        kv_new = _kv_out(p, col["k"], col["v"], w_a, bp, lp_, depth, l, kv_new)

    y_prompt = xs[0].reshape(bp, lp_, d)
    y_sample = xs[1].reshape(bs, ls, d)
    new_cache_k = kv_new[0].reshape(bp, depth, lp_, n_ha, 2, DH_A)
    new_cache_v = kv_new[1].reshape(bp, depth, lp_, n_ha, HEAD_A)
    return (y_prompt, y_sample, new_cache_k, new_cache_v, jnp.stack(ss, axis=1))
```
